```python
import math
import jax, jax.numpy as jnp
from jax import lax
import numpy as np

D_MODEL = 1024
BATCH = 4
SEQ = 4096
DEPTH = 2

CONV_WIDTH = 512
CONV_K = 3
H_SPARSE = 8
D_SPARSE = 64
H_IDX = 8
D_IDX = 32
TOPK_MAX = 256
H_DIFF = 4
D_DIFF = 64
H_DIL = 8
D_DIL = 64
DIL_GROUPS = ((128, 1), (512, 4), (2048, 16))
W_MAX = 2048
Q_BLK = 128
N_EXPERTS = 32
TOP_K = 4
D_FF = 1024
SWIGLU_LIMIT = 7.0
SWIGLU_ALPHA = 1.702
MOE_BLK = 128
DEEPNORM_ALPHA = (2 * DEPTH) ** 0.25
DEEPNORM_BETA = (8 * DEPTH) ** -0.25
LN_EPS = 1e-5
RMS_EPS = 1e-5

EVEN_SPLITS = (CONV_WIDTH, CONV_WIDTH, CONV_WIDTH,
               H_SPARSE * D_SPARSE, H_SPARSE * D_SPARSE, H_SPARSE * D_SPARSE,
               H_IDX * D_IDX, D_IDX, H_IDX)
ODD_SPLITS = (H_DIFF * 2 * D_DIFF, H_DIFF * 2 * D_DIFF, H_DIFF * 2 * D_DIFF,
              H_DIL * D_DIL, H_DIL * D_DIL, H_DIL * D_DIL)
EVEN_IN = sum(EVEN_SPLITS)
ODD_IN = sum(ODD_SPLITS)
EVEN_OUT = CONV_WIDTH + H_SPARSE * D_SPARSE
ODD_OUT = H_DIFF * 2 * D_DIFF + H_DIL * D_DIL

kernel_name = "hybrid_conv_dsa_diff_dilated_moe_deepnorm"


def alibi_slopes(n):
    return jnp.asarray([2.0 ** (-8.0 * (h + 1) / n) for h in range(n)], dtype=jnp.float32)


def layer_norm(x, g, b):
    xf = x.astype(jnp.float32)
    mu = jnp.mean(xf, -1, keepdims=True)
    var = jnp.mean(jnp.square(xf - mu), -1, keepdims=True)
    return ((xf - mu) * lax.rsqrt(var + LN_EPS) * g.astype(jnp.float32) + b.astype(jnp.float32)).astype(x.dtype)


def split_cols(h, sizes):
    offs = np.cumsum(sizes)[:-1].tolist()
    return jnp.split(h, offs, axis=-1)


def to_blocks(a):
    b, s = a.shape[:2]
    return jnp.moveaxis(a.reshape((b, s // Q_BLK, Q_BLK) + a.shape[2:]), 1, 0)


def from_blocks(a):
    a = jnp.moveaxis(a, 0, 1)
    return a.reshape((a.shape[0], a.shape[1] * a.shape[2]) + a.shape[3:])


def block_starts(seq):
    return jnp.arange(seq // Q_BLK, dtype=jnp.int32) * Q_BLK


def short_conv_mixer(gb, gc, xa, conv_w):
    z = gc * xa
    s = z.shape[1]
    zp = jnp.pad(z, ((0, 0), (CONV_K - 1, 0), (0, 0)))
    y = conv_w[0] * zp[:, CONV_K - 1:CONV_K - 1 + s]
    for j in range(1, CONV_K):
        y = y + conv_w[j] * zp[:, CONV_K - 1 - j:CONV_K - 1 - j + s]
    return gb * y


def dsa_sparse_attention(q, k, v, iq, ik, iw):
    seq = k.shape[1]
    k_sel = min(TOPK_MAX, seq // 4)
    slopes = alibi_slopes(H_SPARSE)
    scale = D_SPARSE ** -0.5
    idx_scale = (D_IDX ** -0.5) * (H_IDX ** -0.5)
    key_pos = jnp.arange(seq, dtype=jnp.int32)

    def block(args):
        qb, iqb, iwb, t0 = args
        tq = t0 + jnp.arange(Q_BLK, dtype=jnp.int32)
        rel = jax.nn.relu(jnp.einsum('bqhd,bsd->bqhs', iqb, ik).astype(jnp.float32))
        score = jnp.einsum('bqh,bqhs->bqs', iwb.astype(jnp.float32), rel) * idx_scale
        causal = key_pos[None, :] <= tq[:, None]
        score = jnp.where(causal[None], score, -jnp.inf)
        _, sel = lax.top_k(score, k_sel)
        ks = jax.vmap(lambda kk, ii: kk[ii])(k, sel)
        vs = jax.vmap(lambda vv, ii: vv[ii])(v, sel)
        logits = jnp.einsum('bqhd,bqkhd->bhqk', qb, ks).astype(jnp.float32) * scale
        dist = (tq[None, :, None] - sel).astype(jnp.float32)
        logits = logits - slopes[None, :, None, None] * dist[:, None]
        valid = (sel <= tq[None, :, None])[:, None]
        logits = jnp.where(valid, logits, -jnp.inf)
        p = jax.nn.softmax(logits, axis=-1).astype(v.dtype)
        return jnp.einsum('bhqk,bqkhd->bqhd', p, vs)

    out = lax.map(block, (to_blocks(q), to_blocks(iq), to_blocks(iw), block_starts(seq)))
    return from_blocks(out)


def diff_attention(q, k, v, lam, subln_g, lam_init):
    seq = k.shape[1]
    slopes = alibi_slopes(H_DIFF)
    scale = D_DIFF ** -0.5
    key_pos = jnp.arange(seq, dtype=jnp.int32)

    def block(args):
        qb, t0 = args
        tq = t0 + jnp.arange(Q_BLK, dtype=jnp.int32)
        logits = jnp.einsum('bqhcd,bshcd->bhcqs', qb, k).astype(jnp.float32) * scale
        dist = (tq[:, None] - key_pos[None, :])
        logits = logits - slopes[None, :, None, None, None] * dist.astype(jnp.float32)
        logits = jnp.where(dist >= 0, logits, -jnp.inf)
        p = jax.nn.softmax(logits, axis=-1)
        attn = (p[:, :, 0] - lam * p[:, :, 1]).astype(v.dtype)
        return jnp.einsum('bhqs,bshe->bqhe', attn, v)

    out = from_blocks(lax.map(block, (to_blocks(q), block_starts(seq))))
    of = out.astype(jnp.float32)
    of = of * lax.rsqrt(jnp.mean(of * of, -1, keepdims=True) + RMS_EPS) * subln_g.astype(jnp.float32)
    return (of * (1.0 - lam_init)).astype(v.dtype)


def dilated_attention(q, k, v):
    seq = k.shape[1]
    slopes = alibi_slopes(H_DIL)
    scale = D_DIL ** -0.5
    pad = ((0, 0), (W_MAX, 0), (0, 0), (0, 0))
    kp = jnp.pad(k, pad)
    vp = jnp.pad(v, pad)
    span = W_MAX + Q_BLK
    qi = np.arange(Q_BLK)
    group_dists = [np.arange(0, w + 1, d) for (w, d) in DIL_GROUPS]

    def block(args):
        qb, t0 = args
        kw = lax.dynamic_slice_in_dim(kp, t0, span, axis=1)
        vw = lax.dynamic_slice_in_dim(vp, t0, span, axis=1)
        tq = t0 + jnp.arange(Q_BLK, dtype=jnp.int32)
        lses = []
        outs = []
        for dists in group_dists:
            local = qi[:, None] + W_MAX - dists[None, :]
            kg = kw[:, local]
            vg = vw[:, local]
            logits = jnp.einsum('bqhd,bqnhd->bhqn', qb, kg).astype(jnp.float32) * scale
            logits = logits - slopes[None, :, None, None] * jnp.asarray(dists, jnp.float32)[None, None, None, :]
            valid = (tq[:, None] - jnp.asarray(dists, jnp.int32)[None, :]) >= 0
            logits = jnp.where(valid, logits, -jnp.inf)
            lse = jax.nn.logsumexp(logits, axis=-1)
            p = jnp.exp(logits - lse[..., None]).astype(v.dtype)
            outs.append(jnp.einsum('bhqn,bqnhd->bqhd', p, vg))
            lses.append(lse)
        wts = jax.nn.softmax(jnp.stack(lses, 0), axis=0)
        wts = jnp.swapaxes(wts, 2, 3)[..., None]
        y = wts[0].astype(v.dtype) * outs[0]
        for g in range(1, len(outs)):
            y = y + wts[g].astype(v.dtype) * outs[g]
        return y

    return from_blocks(lax.map(block, (to_blocks(q), block_starts(seq))))


def even_mixer(x, w_in, conv_w, w_out):
    b, s, _ = x.shape
    gb, gc, xa, q, k, v, iq, ik, iw = split_cols(x @ w_in, EVEN_SPLITS)
    ya = short_conv_mixer(gb, gc, xa, conv_w)
    hs = lambda t: t.reshape(b, s, H_SPARSE, D_SPARSE)
    yb = dsa_sparse_attention(hs(q), hs(k), hs(v), iq.reshape(b, s, H_IDX, D_IDX), ik, iw)
    y = jnp.concatenate([ya, yb.reshape(b, s, H_SPARSE * D_SPARSE)], axis=-1)
    return y @ w_out


def odd_mixer(x, w_in, lam_q1, lam_k1, lam_q2, lam_k2, subln_g, w_out, layer):
    b, s, _ = x.shape
    qc, kc, vc, qd, kd, vd = split_cols(x @ w_in, ODD_SPLITS)
    lam_init = 0.8 - 0.6 * math.exp(-0.3 * layer)
    f32 = jnp.float32
    lam = (jnp.exp(jnp.sum(lam_q1.astype(f32) * lam_k1.astype(f32)))
           - jnp.exp(jnp.sum(lam_q2.astype(f32) * lam_k2.astype(f32))) + lam_init)
    yc = diff_attention(qc.reshape(b, s, H_DIFF, 2, D_DIFF), kc.reshape(b, s, H_DIFF, 2, D_DIFF),
                        vc.reshape(b, s, H_DIFF, 2 * D_DIFF), lam, subln_g, lam_init)
    hd = lambda t: t.reshape(b, s, H_DIL, D_DIL)
    yd = dilated_attention(hd(qd), hd(kd), hd(vd))
    y = jnp.concatenate([yc.reshape(b, s, H_DIFF * 2 * D_DIFF), yd.reshape(b, s, H_DIL * D_DIL)], axis=-1)
    return y @ w_out


def expert_ffn(xb, wgu, bgu, wdn, bdn):
    h = xb @ wgu + bgu
    gate = jnp.minimum(h[..., :D_FF], SWIGLU_LIMIT)
    up = jnp.clip(h[..., D_FF:], -SWIGLU_LIMIT, SWIGLU_LIMIT)
    glu = gate * jax.nn.sigmoid(SWIGLU_ALPHA * gate)
    return ((up + 1.0) * glu) @ wdn + bdn


def moe(x, router_w, router_b, w_gu, b_gu, w_dn, b_dn):
    b, s, d = x.shape
    xt = x.reshape(b * s, d)
    n = xt.shape[0]
    logits = (xt @ router_w + router_b).astype(jnp.float32)
    top_val, top_idx = lax.top_k(logits, TOP_K)
    gates = jax.nn.softmax(top_val, axis=-1)
    flat_e = top_idx.reshape(-1)
    flat_tok = jnp.arange(n * TOP_K, dtype=jnp.int32) // TOP_K
    order = jnp.argsort(flat_e)
    e_sorted = flat_e[order]
    tok_sorted = flat_tok[order]
    gate_sorted = gates.reshape(-1)[order]
    counts = jnp.bincount(flat_e, length=N_EXPERTS)
    padded = ((counts + MOE_BLK - 1) // MOE_BLK) * MOE_BLK
    padded_end = jnp.cumsum(padded)
    padded_start = padded_end - padded
    start = jnp.cumsum(counts) - counts
    rank = jnp.arange(n * TOP_K, dtype=jnp.int32) - start[e_sorted]
    dest = padded_start[e_sorted] + rank
    n_rows = ((n * TOP_K + N_EXPERTS * (MOE_BLK - 1) + MOE_BLK - 1) // MOE_BLK) * MOE_BLK
    n_blocks = n_rows // MOE_BLK
    row_tok = jnp.zeros((n_rows,), jnp.int32).at[dest].set(tok_sorted)
    blk_e = jnp.clip(jnp.searchsorted(padded_end, jnp.arange(n_blocks) * MOE_BLK, side='right'),
                     0, N_EXPERTS - 1)
    xb = xt[row_tok].reshape(n_blocks, MOE_BLK, d)
    yb = lax.map(lambda a: expert_ffn(a[0], w_gu[a[1]], b_gu[a[1]], w_dn[a[1]], b_dn[a[1]]), (xb, blk_e))
    y_assign = yb.reshape(n_rows, d)[dest].astype(jnp.float32)
    out = jnp.zeros((n, d), jnp.float32).at[tok_sorted].add(gate_sorted[:, None] * y_assign)
    return out.astype(x.dtype).reshape(b, s, d)


def setup_inputs(seed: int = 0) -> dict:
    key = jax.random.key(seed)
    keys = iter(jax.random.split(key, 64))
    d = D_MODEL

    def nrm(shape, sc):
        return jax.random.normal(next(keys), shape, jnp.float32) * sc

    inp = {"x": nrm((BATCH, SEQ, d), 1.0)}
    for layer in range(DEPTH):
        sfx = "_%d" % layer
        if layer % 2 == 0:
            inp["w_in" + sfx] = nrm((d, EVEN_IN), d ** -0.5)
            inp["conv_w" + sfx] = nrm((CONV_K, CONV_WIDTH), 0.5)
            inp["w_out" + sfx] = nrm((EVEN_OUT, d), EVEN_OUT ** -0.5 * DEEPNORM_BETA)
        else:
            inp["w_in" + sfx] = nrm((d, ODD_IN), d ** -0.5)
            inp["lam_q1" + sfx] = nrm((D_DIFF,), 0.1)
            inp["lam_k1" + sfx] = nrm((D_DIFF,), 0.1)
            inp["lam_q2" + sfx] = nrm((D_DIFF,), 0.1)
            inp["lam_k2" + sfx] = nrm((D_DIFF,), 0.1)
            inp["subln_g" + sfx] = 1.0 + nrm((2 * D_DIFF,), 0.01)
            inp["w_out" + sfx] = nrm((ODD_OUT, d), ODD_OUT ** -0.5 * DEEPNORM_BETA)
        inp["ln_mix_g" + sfx] = 1.0 + nrm((d,), 0.01)
        inp["ln_mix_b" + sfx] = nrm((d,), 0.01)
        inp["router_w" + sfx] = nrm((d, N_EXPERTS), d ** -0.5)
        inp["router_b" + sfx] = nrm((N_EXPERTS,), 0.01)
        inp["w_gu" + sfx] = nrm((N_EXPERTS, d, 2 * D_FF), d ** -0.5)
        inp["b_gu" + sfx] = nrm((N_EXPERTS, 2 * D_FF), 0.01)
        inp["w_dn" + sfx] = nrm((N_EXPERTS, D_FF, d), D_FF ** -0.5 * DEEPNORM_BETA)
        inp["b_dn" + sfx] = nrm((N_EXPERTS, d), 0.01)
        inp["ln_ffn_g" + sfx] = 1.0 + nrm((d,), 0.01)
        inp["ln_ffn_b" + sfx] = nrm((d,), 0.01)
    return inp


def reference(x,
              w_in_0, conv_w_0, w_out_0, ln_mix_g_0, ln_mix_b_0,
              router_w_0, router_b_0, w_gu_0, b_gu_0, w_dn_0, b_dn_0, ln_ffn_g_0, ln_ffn_b_0,
              w_in_1, lam_q1_1, lam_k1_1, lam_q2_1, lam_k2_1, subln_g_1, w_out_1, ln_mix_g_1, ln_mix_b_1,
              router_w_1, router_b_1, w_gu_1, b_gu_1, w_dn_1, b_dn_1, ln_ffn_g_1, ln_ffn_b_1):
    mixer_params = [(w_in_0, conv_w_0, w_out_0),
                    (w_in_1, lam_q1_1, lam_k1_1, lam_q2_1, lam_k2_1, subln_g_1, w_out_1)]
    ln_mix = [(ln_mix_g_0, ln_mix_b_0), (ln_mix_g_1, ln_mix_b_1)]
    moe_params = [(router_w_0, router_b_0, w_gu_0, b_gu_0, w_dn_0, b_dn_0),
                  (router_w_1, router_b_1, w_gu_1, b_gu_1, w_dn_1, b_dn_1)]
    ln_ffn = [(ln_ffn_g_0, ln_ffn_b_0), (ln_ffn_g_1, ln_ffn_b_1)]
    for layer in range(DEPTH):
        if layer % 2 == 0:
            m = even_mixer(x, *mixer_params[layer])
        else:
            m = odd_mixer(x, *mixer_params[layer], layer)
        x = layer_norm(DEEPNORM_ALPHA * x + m, *ln_mix[layer])
        f = moe(x, *moe_params[layer])
        x = layer_norm(DEEPNORM_ALPHA * x + f, *ln_ffn[layer])
    return x
```

```python
import functools
import math

import jax
import jax.numpy as jnp
from jax import lax
from jax.experimental import pallas as pl
from jax.experimental.pallas import tpu as pltpu

F32 = jnp.float32
BF16 = jnp.bfloat16
I32 = jnp.int32

CONV_W = 512
CONV_TAPS = 3
N_HEADS_SPARSE = 8
D_HEAD = 64
N_IDX_HEADS = 8
D_IDX = 32
TOPK_LIMIT = 256
N_HEADS_DIFF = 4
N_HEADS_DIL = 8
DIL_GROUPS = ((128, 1), (512, 4), (2048, 16))
N_EXPERTS = 32
TOP_K = 4
D_FF = 1024
SWIGLU_LIMIT = 7.0
SWIGLU_ALPHA = 1.702
DEPTH = 2
DEEPNORM_ALPHA = (2 * DEPTH) ** 0.25
LN_EPS = 1e-5
RMS_EPS = 1e-5

LANES = 128
VMEM_LIMIT = 56 * 1024 * 1024
TQ = 128
KC = 512
ROW_TILE = 256
MOE_BM = 256
GATHER_T = 128
NEG_BIG = -1e30
INT_MIN = -2 ** 31


def _alibi_slopes(n):
    return [2.0 ** (-8.0 * (h + 1) / n) for h in range(n)]


def _cparams(*sem):
    return pltpu.CompilerParams(dimension_semantics=sem, vmem_limit_bytes=VMEM_LIMIT)


def _layer_norm(z, g, b):
    mu = jnp.mean(z, axis=-1, keepdims=True)
    zc = z - mu
    var = jnp.mean(zc * zc, axis=-1, keepdims=True)
    return zc * lax.rsqrt(var + LN_EPS) * g + b


def _proj_kernel(*refs, n_out):
    x_ref = refs[0]
    w_refs = refs[1:1 + n_out]
    o_refs = refs[1 + n_out:]
    xb = x_ref[...].astype(BF16)
    for w_ref, o_ref in zip(w_refs, o_refs):
        o_ref[...] = jnp.dot(xb, w_ref[...], preferred_element_type=F32).astype(o_ref.dtype)


def _project(x2d, weights, out_dtypes):
    n, d = x2d.shape
    n_out = len(weights)
    in_specs = [pl.BlockSpec((ROW_TILE, d), lambda i: (i, 0))]
    in_specs += [pl.BlockSpec(w.shape, lambda i: (0, 0)) for w in weights]
    out_specs = [pl.BlockSpec((ROW_TILE, w.shape[1]), lambda i: (i, 0)) for w in weights]
    out_shape = [jax.ShapeDtypeStruct((n, w.shape[1]), dt) for w, dt in zip(weights, out_dtypes)]
    return pl.pallas_call(
        functools.partial(_proj_kernel, n_out=n_out),
        grid=(n // ROW_TILE,),
        in_specs=in_specs, out_specs=out_specs, out_shape=out_shape,
        compiler_params=_cparams("parallel"),
        name="in_proj",
    )(x2d, *weights)


def _conv_kernel(gb_ref, gc_ref, xa_ref, w_ref, o_ref, prev_ref):
    j = pl.program_id(1)
    t = gb_ref.shape[0]

    @pl.when(j == 0)
    def _():
        prev_ref[...] = jnp.zeros_like(prev_ref)

    z = gc_ref[...] * xa_ref[...]
    row = lax.broadcasted_iota(I32, z.shape, 0)
    prev2 = prev_ref[0:1, :]
    prev1 = prev_ref[1:2, :]
    z1 = jnp.where(row == 0, prev1, pltpu.roll(z, 1, 0))
    z2 = jnp.where(row == 0, prev2, jnp.where(row == 1, prev1, pltpu.roll(z, 2, 0)))
    w = w_ref[...]
    y = w[0:1, :] * z + w[1:2, :] * z1 + w[2:3, :] * z2
    o_ref[...] = (gb_ref[...] * y).astype(o_ref.dtype)
    prev_ref[0:1, :] = z[t - 2:t - 1, :]
    prev_ref[1:2, :] = z[t - 1:t, :]


def _short_conv(ha, conv_w, batch, seq):
    n = ha.shape[0]
    t = min(512, seq)
    nj = seq // t
    spec = lambda c: pl.BlockSpec((t, CONV_W), lambda b, j, c=c: (b * nj + j, c))
    return pl.pallas_call(
        _conv_kernel,
        grid=(batch, nj),
        in_specs=[spec(0), spec(1), spec(2), pl.BlockSpec((CONV_TAPS, CONV_W), lambda b, j: (0, 0))],
        out_specs=pl.BlockSpec((t, CONV_W), lambda b, j: (b * nj + j, 0)),
        out_shape=jax.ShapeDtypeStruct((n, CONV_W), BF16),
        scratch_shapes=[pltpu.VMEM((8, CONV_W), F32)],
        compiler_params=_cparams("arbitrary", "arbitrary"),
        name="short_conv",
    )(ha, ha, ha, conv_w)


def _flash_head(q, k_ref, v_ref, dmat_ref, amat_ref, c_lo, c_hi, slope, k_cols, v_cols):
    tq = q.shape[0]
    dv = v_cols[1] - v_cols[0]

    def body(c, carry):
        m, l, acc = carry
        k0 = pl.multiple_of(c * KC, KC)
        kc = k_ref[pl.ds(k0, KC), k_cols[0]:k_cols[1]]
        s = lax.dot_general(q, kc, (((1,), (1,)), ((), ())), preferred_element_type=F32)
        s = s - slope * dmat_ref[:, pl.ds(k0, KC)]
        if amat_ref is not None:
            s = s + amat_ref[:, pl.ds(k0, KC)]
        m_new = jnp.maximum(m, jnp.max(s, axis=1, keepdims=True))
        p = jnp.exp(s - m_new)
        alpha = jnp.exp(m - m_new)
        l = alpha * l + jnp.sum(p, axis=1, keepdims=True)
        vc = v_ref[pl.ds(k0, KC), v_cols[0]:v_cols[1]]
        acc = alpha * acc + jnp.dot(p.astype(BF16), vc, preferred_element_type=F32)
        return m_new, l, acc

    init = (jnp.full((tq, 1), NEG_BIG, F32), jnp.zeros((tq, 1), F32), jnp.zeros((tq, dv), F32))
    _, l, acc = lax.fori_loop(c_lo, c_hi, body, init)
    return acc, l


def _dsa_kernel(q_ref, k_ref, v_ref, iq_ref, ik_ref, iw_ref, o_ref, key_ref, dmat_ref, *, k_sel):
    i = pl.program_id(1)
    t0 = i * TQ
    n_chunks = (t0 + TQ + KC - 1) // KC
    idx_scale = (D_IDX ** -0.5) * (N_IDX_HEADS ** -0.5)
    row = t0 + lax.broadcasted_iota(I32, (TQ, KC), 0)
    col_in_chunk = lax.broadcasted_iota(I32, (TQ, KC), 1)

    iq = iq_ref[...]
    iw = iw_ref[:, 0:N_IDX_HEADS]

    def score_body(c, _):
        k0 = pl.multiple_of(c * KC, KC)
        ik = ik_ref[pl.ds(k0, KC), 0:D_IDX]
        sc = jnp.zeros((TQ, KC), F32)
        for h in range(N_IDX_HEADS):
            rel = lax.dot_general(iq[:, h * D_IDX:(h + 1) * D_IDX], ik,
                                  (((1,), (1,)), ((), ())), preferred_element_type=F32)
            sc = sc + iw[:, h:h + 1] * jnp.maximum(rel, 0.0)
        sc = sc * idx_scale + 0.0
        causal = (k0 + col_in_chunk) <= row
        sc = jnp.where(causal, sc, -jnp.inf)
        bits = lax.bitcast_convert_type(sc, I32)
        key_ref[:, pl.ds(k0, KC)] = bits ^ (lax.shift_right_arithmetic(bits, 31) & 0x7FFFFFFF)
        return 0

    lax.fori_loop(0, n_chunks, score_body, 0)

    def count_ge(cand):
        def body(c, acc):
            k0 = pl.multiple_of(c * KC, KC)
            ge = jnp.where(key_ref[:, pl.ds(k0, KC)] >= cand, 1.0, 0.0)
            part = ge[:, 0:LANES]
            for u in range(1, KC // LANES):
                part = part + ge[:, u * LANES:(u + 1) * LANES]
            return acc + part
        acc = lax.fori_loop(0, n_chunks, body, jnp.zeros((TQ, LANES), F32))
        return jnp.sum(acc, axis=1, keepdims=True)

    kf = float(k_sel)
    thr = jnp.where(count_ge(jnp.zeros((TQ, 1), I32)) >= kf, 0, INT_MIN).astype(I32)

    def bit_body(b, thr):
        cand = thr + lax.shift_left(jnp.int32(1), 30 - b)
        return jnp.where(count_ge(cand) >= kf, cand, thr)

    thr = lax.fori_loop(0, 31, bit_body, thr)
    need = kf - count_ge(thr + 1)

    tri = (lax.broadcasted_iota(I32, (KC, KC), 0) <= lax.broadcasted_iota(I32, (KC, KC), 1)).astype(BF16)

    def mask_body(c, ties_before):
        k0 = pl.multiple_of(c * KC, KC)
        key = key_ref[:, pl.ds(k0, KC)]
        eq = key == thr
        eqf = jnp.where(eq, 1.0, 0.0)
        rank = ties_before + jnp.dot(eqf.astype(BF16), tri, preferred_element_type=F32)
        dist = row - (k0 + col_in_chunk)
        sel = ((key > thr) | (eq & (rank <= need))) & (dist >= 0)
        dmat_ref[:, pl.ds(k0, KC)] = jnp.where(sel, dist.astype(F32), jnp.inf)
        return ties_before + jnp.sum(eqf, axis=1, keepdims=True)

    lax.fori_loop(0, n_chunks, mask_body, jnp.zeros((TQ, 1), F32))

    slopes = _alibi_slopes(N_HEADS_SPARSE)
    scale = D_HEAD ** -0.5
    for h in range(N_HEADS_SPARSE):
        cols = (h * D_HEAD, (h + 1) * D_HEAD)
        q = q_ref[:, cols[0]:cols[1]] * scale
        acc, l = _flash_head(q.astype(BF16), k_ref, v_ref, dmat_ref, None, 0, n_chunks,
                             slopes[h], cols, cols)
        o_ref[:, cols[0]:cols[1]] = (acc / l).astype(o_ref.dtype)


def _dsa_attention(ha, hb, batch, seq):
    n = hb.shape[0]
    nq = seq // TQ
    k_sel = min(TOPK_LIMIT, seq // 4)
    width = N_HEADS_SPARSE * D_HEAD
    iq_w = N_IDX_HEADS * D_IDX
    return pl.pallas_call(
        functools.partial(_dsa_kernel, k_sel=k_sel),
        grid=(batch, nq),
        in_specs=[
            pl.BlockSpec((TQ, width), lambda b, i: (b * nq + i, 0)),
            pl.BlockSpec((seq, width), lambda b, i: (b, 1)),
            pl.BlockSpec((seq, width), lambda b, i: (b, 2)),
            pl.BlockSpec((TQ, iq_w), lambda b, i: (b * nq + i, 3 * width // iq_w)),
            pl.BlockSpec((seq, LANES), lambda b, i: (b, (3 * width + iq_w) // LANES)),
            pl.BlockSpec((TQ, LANES), lambda b, i: (b * nq + i, 3 * CONV_W // LANES)),
        ],
        out_specs=pl.BlockSpec((TQ, width), lambda b, i: (b * nq + i, 0)),
        out_shape=jax.ShapeDtypeStruct((n, width), BF16),
        scratch_shapes=[pltpu.VMEM((TQ, seq), I32), pltpu.VMEM((TQ, seq), F32)],
        compiler_params=_cparams("arbitrary", "arbitrary"),
        name="dsa_attention",
    )(hb, hb, hb, hb, hb, ha)


def _diff_kernel(q_ref, k_ref, v_ref, lam_ref, g_ref, o_ref, dmat_ref, *, lam_init):
    i = pl.program_id(1)
    t0 = i * TQ
    n_chunks = (t0 + TQ + KC - 1) // KC
    row = t0 + lax.broadcasted_iota(I32, (TQ, KC), 0)
    col_in_chunk = lax.broadcasted_iota(I32, (TQ, KC), 1)

    def mask_body(c, _):
        k0 = pl.multiple_of(c * KC, KC)
        dist = row - (k0 + col_in_chunk)
        dmat_ref[:, pl.ds(k0, KC)] = jnp.where(dist >= 0, dist.astype(F32), jnp.inf)
        return 0

    lax.fori_loop(0, n_chunks, mask_body, 0)

    lv = lam_ref[...]
    lam = (jnp.exp(jnp.sum(lv[0:1, :] * lv[1:2, :], axis=1, keepdims=True))
           - jnp.exp(jnp.sum(lv[2:3, :] * lv[3:4, :], axis=1, keepdims=True)) + lam_init)

    slopes = _alibi_slopes(N_HEADS_DIFF)
    scale = D_HEAD ** -0.5
    dv = 2 * D_HEAD
    g = g_ref[...]
    for h in range(N_HEADS_DIFF):
        v_cols = (h * dv, (h + 1) * dv)
        outs = []
        for c in range(2):
            cols = ((2 * h + c) * D_HEAD, (2 * h + c + 1) * D_HEAD)
            q = q_ref[:, cols[0]:cols[1]] * scale
            acc, l = _flash_head(q.astype(BF16), k_ref, v_ref, dmat_ref, None, 0, n_chunks,
                                 slopes[h], cols, v_cols)
            outs.append(acc / l)
        of = outs[0] - lam * outs[1]
        of = of * lax.rsqrt(jnp.mean(of * of, axis=1, keepdims=True) + RMS_EPS) * g
        o_ref[:, v_cols[0]:v_cols[1]] = (of * (1.0 - lam_init)).astype(o_ref.dtype)


def _diff_attention(hb, lam_vecs, subln_g, batch, seq, layer):
    n = hb.shape[0]
    nq = seq // TQ
    width = N_HEADS_DIFF * 2 * D_HEAD
    lam_init = 0.8 - 0.6 * math.exp(-0.3 * layer)
    return pl.pallas_call(
        functools.partial(_diff_kernel, lam_init=lam_init),
        grid=(batch, nq),
        in_specs=[
            pl.BlockSpec((TQ, width), lambda b, i: (b * nq + i, 0)),
            pl.BlockSpec((seq, width), lambda b, i: (b, 1)),
            pl.BlockSpec((seq, width), lambda b, i: (b, 2)),
            pl.BlockSpec(lam_vecs.shape, lambda b, i: (0, 0)),
            pl.BlockSpec(subln_g.shape, lambda b, i: (0, 0)),
        ],
        out_specs=pl.BlockSpec((TQ, width), lambda b, i: (b * nq + i, 0)),
        out_shape=jax.ShapeDtypeStruct((n, width), BF16),
        scratch_shapes=[pltpu.VMEM((TQ, seq), F32)],
        compiler_params=_cparams("arbitrary", "arbitrary"),
        name="diff_attention",
    )(hb, hb, hb, lam_vecs, subln_g)


def _dilated_kernel(q_ref, k_ref, v_ref, o_ref, dmat_ref, amat_ref):
    i = pl.program_id(1)
    t0 = i * TQ
    w_max = max(w for w, _ in DIL_GROUPS)
    c_lo = jnp.maximum(t0 - w_max, 0) // KC
    c_hi = (t0 + TQ + KC - 1) // KC
    row = t0 + lax.broadcasted_iota(I32, (TQ, KC), 0)
    col_in_chunk = lax.broadcasted_iota(I32, (TQ, KC), 1)

    def mask_body(c, _):
        k0 = pl.multiple_of(c * KC, KC)
        dist = row - (k0 + col_in_chunk)
        mult = jnp.zeros((TQ, KC), F32)
        for w, d in DIL_GROUPS:
            member = (dist >= 0) & (dist <= w) & ((dist & (d - 1)) == 0)
            mult = mult + jnp.where(member, 1.0, 0.0)
        on = mult > 0.0
        dmat_ref[:, pl.ds(k0, KC)] = jnp.where(on, dist.astype(F32), jnp.inf)
        amat_ref[:, pl.ds(k0, KC)] = jnp.log(jnp.where(on, mult, 1.0))
        return 0

    lax.fori_loop(c_lo, c_hi, mask_body, 0)

    slopes = _alibi_slopes(N_HEADS_DIL)
    scale = D_HEAD ** -0.5
    for h in range(N_HEADS_DIL):
        cols = (h * D_HEAD, (h + 1) * D_HEAD)
        q = q_ref[:, cols[0]:cols[1]] * scale
        acc, l = _flash_head(q.astype(BF16), k_ref, v_ref, dmat_ref, amat_ref, c_lo, c_hi,
                             slopes[h], cols, cols)
        o_ref[:, cols[0]:cols[1]] = (acc / l).astype(o_ref.dtype)


def _dilated_attention(hb, batch, seq):
    n = hb.shape[0]
    nq = seq // TQ
    width = N_HEADS_DIL * D_HEAD
    return pl.pallas_call(
        _dilated_kernel,
        grid=(batch, nq),
        in_specs=[
            pl.BlockSpec((TQ, width), lambda b, i: (b * nq + i, 3)),
            pl.BlockSpec((seq, width), lambda b, i: (b, 4)),
            pl.BlockSpec((seq, width), lambda b, i: (b, 5)),
        ],
        out_specs=pl.BlockSpec((TQ, width), lambda b, i: (b * nq + i, 0)),
        out_shape=jax.ShapeDtypeStruct((n, width), BF16),
        scratch_shapes=[pltpu.VMEM((TQ, seq), F32), pltpu.VMEM((TQ, seq), F32)],
        compiler_params=_cparams("arbitrary", "arbitrary"),
        name="dilated_attention",
    )(hb, hb, hb)


def _mix_out_kernel(ya_ref, yb_ref, wa_ref, wb_ref, x_ref, g_ref, b_ref, rw_ref, rb_ref,
                    x1_ref, eidx_ref, gate_ref, rank_ref, cnt_ref, carry_ref):
    step = pl.program_id(0)

    @pl.when(step == 0)
    def _():
        carry_ref[...] = jnp.zeros_like(carry_ref)

    m = (jnp.dot(ya_ref[...], wa_ref[...], preferred_element_type=F32)
         + jnp.dot(yb_ref[...], wb_ref[...], preferred_element_type=F32))
    x1 = _layer_norm(DEEPNORM_ALPHA * x_ref[...] + m, g_ref[...], b_ref[...])
    x1_ref[...] = x1

    t = x1.shape[0]
    logits = jnp.dot(x1.astype(BF16), rw_ref[...], preferred_element_type=F32) + rb_ref[...]
    lane = lax.broadcasted_iota(I32, (t, LANES), 1)
    vals, idxs = [], []
    lg = logits
    for _ in range(TOP_K):
        mx = jnp.max(lg, axis=1, keepdims=True)
        ix = jnp.min(jnp.where(lg == mx, lane, LANES), axis=1, keepdims=True)
        vals.append(mx)
        idxs.append(ix)
        lg = jnp.where(lane == ix, -jnp.inf, lg)
    exps = [jnp.exp(v - vals[0]) for v in vals]
    denom = exps[0]
    for e in exps[1:]:
        denom = denom + e

    onehot = jnp.zeros((t, LANES), F32)
    for ix in idxs:
        onehot = onehot + jnp.where(lane == ix, 1.0, 0.0)
    strict = (lax.broadcasted_iota(I32, (t, t), 1) < lax.broadcasted_iota(I32, (t, t), 0)).astype(BF16)
    before = jnp.dot(strict, onehot.astype(BF16), preferred_element_type=F32) + carry_ref[...]

    eidx = jnp.zeros((t, LANES), I32)
    gate = jnp.zeros((t, LANES), F32)
    rank = jnp.zeros((t, LANES), I32)
    for k in range(TOP_K):
        rk = jnp.sum(jnp.where(lane == idxs[k], before, 0.0), axis=1, keepdims=True)
        eidx = jnp.where(lane == k, idxs[k], eidx)
        gate = jnp.where(lane == k, exps[k] / denom, gate)
        rank = jnp.where(lane == k, rk.astype(I32), rank)
    eidx_ref[...] = eidx
    gate_ref[...] = gate
    rank_ref[...] = rank
    carry_ref[...] = carry_ref[...] + jnp.sum(onehot, axis=0, keepdims=True)
    cnt_ref[...] = carry_ref[...]


def _mix_out(ya, yb, w_out, x2d, ln_g, ln_b, router_w, router_b):
    n, d = x2d.shape
    half = ya.shape[1]
    wa = w_out[:half].astype(BF16)
    wb = w_out[half:].astype(BF16)
    rw = jnp.zeros((d, LANES), BF16).at[:, :N_EXPERTS].set(router_w.astype(BF16))
    rb = jnp.full((1, LANES), -jnp.inf, F32).at[0, :N_EXPERTS].set(router_b)
    t = ROW_TILE
    full = lambda a: pl.BlockSpec(a.shape, lambda i: (0, 0))
    rows = lambda w: pl.BlockSpec((t, w), lambda i: (i, 0))
    g2, b2 = ln_g.reshape(1, d), ln_b.reshape(1, d)
    return pl.pallas_call(
        _mix_out_kernel,
        grid=(n // t,),
        in_specs=[rows(half), rows(half), full(wa), full(wb), rows(d), full(g2), full(b2), full(rw), full(rb)],
        out_specs=[rows(d), rows(LANES), rows(LANES), rows(LANES), pl.BlockSpec((1, LANES), lambda i: (0, 0))],
        out_shape=[jax.ShapeDtypeStruct((n, d), F32), jax.ShapeDtypeStruct((n, LANES), I32),
                   jax.ShapeDtypeStruct((n, LANES), F32), jax.ShapeDtypeStruct((n, LANES), I32),
                   jax.ShapeDtypeStruct((1, LANES), F32)],
        scratch_shapes=[pltpu.VMEM((1, LANES), F32)],
        compiler_params=_cparams("arbitrary"),
        name="mix_out_router",
    )(ya, yb, wa, wb, x2d, g2, b2, rw, rb)


def _row_copy(src_hbm, dst_ref, src_row, dst_row, sem):
    return pltpu.make_async_copy(src_hbm.at[pl.ds(src_row, 1)], dst_ref.at[pl.ds(dst_row, 1)], sem)


def _gather_kernel(tok_ref, x_hbm, o_ref, buf_ref, sem):
    t = buf_ref.shape[0]

    def issue(r, _):
        _row_copy(x_hbm, buf_ref, tok_ref[r], r, sem).start()
        return 0

    lax.fori_loop(0, t, issue, 0)

    def drain(r, _):
        _row_copy(x_hbm, buf_ref, tok_ref[r], r, sem).wait()
        return 0

    lax.fori_loop(0, t, drain, 0)
    o_ref[...] = buf_ref[...].astype(o_ref.dtype)


def _gather_rows(x2d, row_tok):
    n_rows = row_tok.shape[0]
    d = x2d.shape[1]
    t = GATHER_T
    return pl.pallas_call(
        _gather_kernel,
        grid=(n_rows // t,),
        in_specs=[pl.BlockSpec((t,), lambda i: (i,), memory_space=pltpu.SMEM),
                  pl.BlockSpec(memory_space=pl.ANY)],
        out_specs=pl.BlockSpec((t, d), lambda i: (i, 0)),
        out_shape=jax.ShapeDtypeStruct((n_rows, d), BF16),
        scratch_shapes=[pltpu.VMEM((t, d), F32), pltpu.SemaphoreType.DMA(())],
        compiler_params=_cparams("arbitrary"),
        name="moe_gather",
    )(row_tok, x2d)


def _expert_kernel(blk_e_ref, n_used_ref, x_ref, wgu_ref, bgu_ref, wdn_ref, bdn_ref, o_ref,
                   wgu_bf, wdn_bf):
    i = pl.program_id(0)
    e = blk_e_ref[i]
    e_prev = blk_e_ref[jnp.maximum(i - 1, 0)]

    @pl.when((i == 0) | (e != e_prev))
    def _():
        wgu_bf[...] = wgu_ref[...].astype(BF16)
        wdn_bf[...] = wdn_ref[...].astype(BF16)

    @pl.when(i < n_used_ref[0])
    def _():
        h = jnp.dot(x_ref[...], wgu_bf[...], preferred_element_type=F32) + bgu_ref[...]
        gate = jnp.minimum(h[:, :D_FF], SWIGLU_LIMIT)
        up = jnp.clip(h[:, D_FF:], -SWIGLU_LIMIT, SWIGLU_LIMIT)
        glu = gate * (1.0 / (1.0 + jnp.exp(-SWIGLU_ALPHA * gate)))
        act = ((up + 1.0) * glu).astype(BF16)
        o_ref[...] = jnp.dot(act, wdn_bf[...], preferred_element_type=F32) + bdn_ref[...]

    @pl.when(i >= n_used_ref[0])
    def _():
        o_ref[...] = jnp.zeros_like(o_ref)


def _expert_ffn(xs, blk_e, n_used, w_gu, b_gu, w_dn, b_dn):
    n_rows, d = xs.shape
    n_blocks = n_rows // MOE_BM
    ne = w_gu.shape[0]
    grid_spec = pltpu.PrefetchScalarGridSpec(
        num_scalar_prefetch=2,
        grid=(n_blocks,),
        in_specs=[
            pl.BlockSpec((MOE_BM, d), lambda i, be, nu: (i, 0)),
            pl.BlockSpec((None, d, 2 * D_FF), lambda i, be, nu: (be[i], 0, 0)),
            pl.BlockSpec((None, 1, 2 * D_FF), lambda i, be, nu: (be[i], 0, 0)),
            pl.BlockSpec((None, D_FF, d), lambda i, be, nu: (be[i], 0, 0)),
            pl.BlockSpec((None, 1, d), lambda i, be, nu: (be[i], 0, 0)),
        ],
        out_specs=pl.BlockSpec((MOE_BM, d), lambda i, be, nu: (i, 0)),
        scratch_shapes=[pltpu.VMEM((d, 2 * D_FF), BF16), pltpu.VMEM((D_FF, d), BF16)],
    )
    return pl.pallas_call(
        _expert_kernel,
        grid_spec=grid_spec,
        out_shape=jax.ShapeDtypeStruct((n_rows, d), F32),
        compiler_params=_cparams("arbitrary"),
        name="moe_experts",
    )(blk_e, n_used, xs, w_gu, b_gu.reshape(ne, 1, 2 * D_FF), w_dn, b_dn.reshape(ne, 1, d))


def _combine_kernel(dest_ref, y_hbm, gate_ref, x_ref, g_ref, b_ref, o_ref, buf_ref, sem):
    t = x_ref.shape[0]

    def issue(r, _):
        for k in range(TOP_K):
            _row_copy(y_hbm, buf_ref.at[k], dest_ref[r * TOP_K + k], r, sem).start()
        return 0

    lax.fori_loop(0, t, issue, 0)

    def drain(r, _):
        for k in range(TOP_K):
            _row_copy(y_hbm, buf_ref.at[k], dest_ref[r * TOP_K + k], r, sem).wait()
        return 0

    lax.fori_loop(0, t, drain, 0)

    gate = gate_ref[...]
    f = gate[:, 0:1] * buf_ref[0]
    for k in range(1, TOP_K):
        f = f + gate[:, k:k + 1] * buf_ref[k]
    o_ref[...] = _layer_norm(DEEPNORM_ALPHA * x_ref[...] + f, g_ref[...], b_ref[...])


def _combine(ys, dest_flat, gates, x2d, ln_g, ln_b):
    n, d = x2d.shape
    t = GATHER_T
    g2, b2 = ln_g.reshape(1, d), ln_b.reshape(1, d)
    full = lambda a: pl.BlockSpec(a.shape, lambda i: (0, 0))
    return pl.pallas_call(
        _combine_kernel,
        grid=(n // t,),
        in_specs=[pl.BlockSpec((t * TOP_K,), lambda i: (i,), memory_space=pltpu.SMEM),
                  pl.BlockSpec(memory_space=pl.ANY),
                  pl.BlockSpec((t, LANES), lambda i: (i, 0)),
                  pl.BlockSpec((t, d), lambda i: (i, 0)),
                  full(g2), full(b2)],
        out_specs=pl.BlockSpec((t, d), lambda i: (i, 0)),
        out_shape=jax.ShapeDtypeStruct((n, d), F32),
        scratch_shapes=[pltpu.VMEM((TOP_K, t, d), F32), pltpu.SemaphoreType.DMA(())],
        compiler_params=_cparams("arbitrary"),
        name="moe_combine",
    )(dest_flat, ys, gates, x2d, g2, b2)


def _moe_block(x1, eidx, gates, rank, counts, w_gu, b_gu, w_dn, b_dn, ln_g, ln_b):
    n = x1.shape[0]
    e_sel = eidx[:, :TOP_K]
    cnt = counts[0, :N_EXPERTS].astype(I32)
    padded = ((cnt + MOE_BM - 1) // MOE_BM) * MOE_BM
    pad_end = jnp.cumsum(padded)
    pad_start = pad_end - padded
    dest = pad_start[e_sel] + rank[:, :TOP_K]
    n_rows = ((n * TOP_K + N_EXPERTS * (MOE_BM - 1) + MOE_BM - 1) // MOE_BM) * MOE_BM
    n_blocks = n_rows // MOE_BM
    dest_flat = dest.reshape(-1)
    tok = jnp.arange(n * TOP_K, dtype=I32) // TOP_K
    row_tok = jnp.zeros((n_rows,), I32).at[dest_flat].set(tok)
    blk_e = jnp.clip(jnp.searchsorted(pad_end, jnp.arange(n_blocks, dtype=I32) * MOE_BM, side='right'),
                     0, N_EXPERTS - 1).astype(I32)
    n_used = (pad_end[-1:] // MOE_BM).astype(I32)
    xs = _gather_rows(x1, row_tok)
    ys = _expert_ffn(xs, blk_e, n_used, w_gu, b_gu, w_dn, b_dn)
    return _combine(ys, dest_flat, gates, x1, ln_g, ln_b)


def _pad_cols(w, width):
    return jnp.pad(w, ((0, 0), (0, width - w.shape[1])))


def kernel(x, w_in_0, conv_w_0, w_out_0, ln_mix_g_0, ln_mix_b_0, router_w_0, router_b_0, w_gu_0, b_gu_0, w_dn_0, b_dn_0, ln_ffn_g_0, ln_ffn_b_0, w_in_1, lam_q1_1, lam_k1_1, lam_q2_1, lam_k2_1, subln_g_1, w_out_1, ln_mix_g_1, ln_mix_b_1, router_w_1, router_b_1, w_gu_1, b_gu_1, w_dn_1, b_dn_1, ln_ffn_g_1, ln_ffn_b_1):
    batch, seq, d = x.shape
    x0 = x.reshape(batch * seq, d)

    n_f32 = 3 * CONV_W
    n_attn = 3 * N_HEADS_SPARSE * D_HEAD + N_IDX_HEADS * D_IDX + D_IDX
    w_a = _pad_cols(jnp.concatenate([w_in_0[:, :n_f32], w_in_0[:, n_f32 + n_attn:]], axis=1),
                    n_f32 + LANES)
    w_b = _pad_cols(w_in_0[:, n_f32:n_f32 + n_attn], 3 * N_HEADS_SPARSE * D_HEAD + N_IDX_HEADS * D_IDX + LANES)
    ha, hb = _project(x0, [w_a.astype(BF16), w_b.astype(BF16)], [F32, BF16])
    ya = _short_conv(ha, conv_w_0, batch, seq)
    yb = _dsa_attention(ha, hb, batch, seq)
    x1, eidx, gates, rank, counts = _mix_out(ya, yb, w_out_0, x0, ln_mix_g_0, ln_mix_b_0, router_w_0, router_b_0)
    x2 = _moe_block(x1, eidx, gates, rank, counts, w_gu_0, b_gu_0, w_dn_0, b_dn_0, ln_ffn_g_0, ln_ffn_b_0)

    (hc,) = _project(x2, [w_in_1.astype(BF16)], [BF16])
    lam_vecs = jnp.stack([lam_q1_1, lam_k1_1, lam_q2_1, lam_k2_1]).astype(F32)
    yc = _diff_attention(hc, lam_vecs, subln_g_1.reshape(1, -1).astype(F32), batch, seq, 1)
    yd = _dilated_attention(hc, batch, seq)
    x3, eidx, gates, rank, counts = _mix_out(yc, yd, w_out_1, x2, ln_mix_g_1, ln_mix_b_1, router_w_1, router_b_1)
    x4 = _moe_block(x3, eidx, gates, rank, counts, w_gu_1, b_gu_1, w_dn_1, b_dn_1, ln_ffn_g_1, ln_ffn_b_1)
    return x4.reshape(batch, seq, d)
```

```python
import functools
import math

import jax
import jax.numpy as jnp
from jax import lax
from jax.experimental import pallas as pl
from jax.experimental.pallas import tpu as pltpu

F32 = jnp.float32
BF16 = jnp.bfloat16
I32 = jnp.int32

CONV_W = 512
CONV_TAPS = 3
N_HEADS_SPARSE = 8
D_HEAD = 64
N_IDX_HEADS = 8
D_IDX = 32
TOPK_LIMIT = 256
N_HEADS_DIFF = 4
N_HEADS_DIL = 8
DIL_GROUPS = ((128, 1), (512, 4), (2048, 16))
N_EXPERTS = 32
TOP_K = 4
D_FF = 1024
SWIGLU_LIMIT = 7.0
SWIGLU_ALPHA = 1.702
DEPTH = 2
DEEPNORM_ALPHA = (2 * DEPTH) ** 0.25
LN_EPS = 1e-5
RMS_EPS = 1e-5

LANES = 128
VMEM_LIMIT = 56 * 1024 * 1024
TQ = 128
KC = 512
ROW_TILE = 256
MOE_BM = 256
GATHER_T = 128
LOG2_E = math.log2(math.e)
INT_MIN = -2 ** 31


def _alibi_slopes(n):
    return [2.0 ** (-8.0 * (h + 1) / n) for h in range(n)]


def _cparams(*sem):
    return pltpu.CompilerParams(dimension_semantics=sem, vmem_limit_bytes=VMEM_LIMIT)


def _layer_norm(z, g, b):
    mu = jnp.mean(z, axis=-1, keepdims=True)
    zc = z - mu
    var = jnp.mean(zc * zc, axis=-1, keepdims=True)
    return zc * lax.rsqrt(var + LN_EPS) * g + b


def _proj_kernel(*refs, n_out):
    x_ref = refs[0]
    w_refs = refs[1:1 + n_out]
    o_refs = refs[1 + n_out:]
    xb = x_ref[...].astype(BF16)
    for w_ref, o_ref in zip(w_refs, o_refs):
        o_ref[...] = jnp.dot(xb, w_ref[...], preferred_element_type=F32).astype(o_ref.dtype)


def _project(x2d, weights, out_dtypes):
    n, d = x2d.shape
    n_out = len(weights)
    in_specs = [pl.BlockSpec((ROW_TILE, d), lambda i: (i, 0))]
    in_specs += [pl.BlockSpec(w.shape, lambda i: (0, 0)) for w in weights]
    out_specs = [pl.BlockSpec((ROW_TILE, w.shape[1]), lambda i: (i, 0)) for w in weights]
    out_shape = [jax.ShapeDtypeStruct((n, w.shape[1]), dt) for w, dt in zip(weights, out_dtypes)]
    return pl.pallas_call(
        functools.partial(_proj_kernel, n_out=n_out),
        grid=(n // ROW_TILE,),
        in_specs=in_specs, out_specs=out_specs, out_shape=out_shape,
        compiler_params=_cparams("parallel"),
        name="in_proj",
    )(x2d, *weights)


def _conv_kernel(gb_ref, gc_ref, xa_ref, w_ref, o_ref, prev_ref):
    j = pl.program_id(1)
    t = gb_ref.shape[0]

    @pl.when(j == 0)
    def _():
        prev_ref[...] = jnp.zeros_like(prev_ref)

    z = gc_ref[...] * xa_ref[...]
    row = lax.broadcasted_iota(I32, z.shape, 0)
    prev2 = prev_ref[0:1, :]
    prev1 = prev_ref[1:2, :]
    z1 = jnp.where(row == 0, prev1, pltpu.roll(z, 1, 0))
    z2 = jnp.where(row == 0, prev2, jnp.where(row == 1, prev1, pltpu.roll(z, 2, 0)))
    w = w_ref[...]
    y = w[0:1, :] * z + w[1:2, :] * z1 + w[2:3, :] * z2
    o_ref[...] = (gb_ref[...] * y).astype(o_ref.dtype)
    prev_ref[0:1, :] = z[t - 2:t - 1, :]
    prev_ref[1:2, :] = z[t - 1:t, :]


def _short_conv(ha, conv_w, batch, seq):
    n = ha.shape[0]
    t = min(512, seq)
    nj = seq // t
    spec = lambda c: pl.BlockSpec((t, CONV_W), lambda b, j, c=c: (b * nj + j, c))
    return pl.pallas_call(
        _conv_kernel,
        grid=(batch, nj),
        in_specs=[spec(0), spec(1), spec(2), pl.BlockSpec((CONV_TAPS, CONV_W), lambda b, j: (0, 0))],
        out_specs=pl.BlockSpec((t, CONV_W), lambda b, j: (b * nj + j, 0)),
        out_shape=jax.ShapeDtypeStruct((n, CONV_W), BF16),
        scratch_shapes=[pltpu.VMEM((8, CONV_W), F32)],
        compiler_params=_cparams("arbitrary", "arbitrary"),
        name="short_conv",
    )(ha, ha, ha, conv_w)


def _lane_tiles(x):
    return [x[:, u * LANES:(u + 1) * LANES] for u in range(x.shape[1] // LANES)]


def _flash_chains(chains, k_ref, v_ref, dmat_ref, amat_ref, c_lo, c_hi, scratch):
    qm_ref, s_ref, mx_ref, lp_ref, l_ref, acc_ref = scratch
    mx_ref[...] = jnp.full(mx_ref.shape, -jnp.inf, F32)

    def pass_a(c, _):
        k0 = pl.multiple_of(c * KC, KC)
        for n, (k_tile, _, slope) in enumerate(chains):
            kc = k_ref[pl.ds(k0, KC), k_tile * LANES:(k_tile + 1) * LANES]
            s = lax.dot_general(qm_ref[n], kc, (((1,), (1,)), ((), ())), preferred_element_type=F32)
            s = s * LOG2_E - (slope * LOG2_E) * dmat_ref[:, pl.ds(k0, KC)]
            if amat_ref is not None:
                s = s + amat_ref[:, pl.ds(k0, KC)]
            s_ref[n, :, pl.ds(k0, KC)] = s
            part = mx_ref[n]
            for t in _lane_tiles(s):
                part = jnp.maximum(part, t)
            mx_ref[n] = part
        return 0

    lax.fori_loop(c_lo, c_hi, pass_a, 0)

    for n in range(len(chains)):
        row_max = jnp.max(mx_ref[n], axis=1, keepdims=True)
        mx_ref[n] = jnp.broadcast_to(row_max, (TQ, LANES))
    lp_ref[...] = jnp.zeros(lp_ref.shape, F32)
    acc_ref[...] = jnp.zeros(acc_ref.shape, F32)

    def pass_b(c, _):
        k0 = pl.multiple_of(c * KC, KC)
        for n, (_, v_tile, _) in enumerate(chains):
            row_max = mx_ref[n]
            p_tiles = [jnp.exp2(t - row_max) for t in _lane_tiles(s_ref[n, :, pl.ds(k0, KC)])]
            part = lp_ref[n]
            for t in p_tiles:
                part = part + t
            lp_ref[n] = part
            p = jnp.concatenate(p_tiles, axis=1).astype(BF16)
            vc = v_ref[pl.ds(k0, KC), v_tile * LANES:(v_tile + 1) * LANES]
            acc_ref[n] = acc_ref[n] + jnp.dot(p, vc, preferred_element_type=F32)
        return 0

    lax.fori_loop(c_lo, c_hi, pass_b, 0)

    for n in range(len(chains)):
        l_ref[n] = jnp.sum(lp_ref[n], axis=1, keepdims=True)


def _split_head_pairs(q_ref, qm_ref, n_tiles, scale):
    low = lax.broadcasted_iota(I32, (TQ, LANES), 1) < D_HEAD
    for j in range(n_tiles):
        qt = q_ref[:, j * LANES:(j + 1) * LANES] * scale
        qm_ref[2 * j] = jnp.where(low, qt, 0).astype(BF16)
        qm_ref[2 * j + 1] = jnp.where(low, 0, qt).astype(BF16)


def _merge_head_pairs(o_ref, l_ref, acc_ref, n_tiles):
    low = lax.broadcasted_iota(I32, (TQ, LANES), 1) < D_HEAD
    for j in range(n_tiles):
        out = jnp.where(low, acc_ref[2 * j] / l_ref[2 * j], acc_ref[2 * j + 1] / l_ref[2 * j + 1])
        o_ref[:, j * LANES:(j + 1) * LANES] = out.astype(o_ref.dtype)


def _flash_scratch(n_chains, seq):
    return [pltpu.VMEM((n_chains, TQ, LANES), BF16), pltpu.VMEM((n_chains, TQ, seq), F32),
            pltpu.VMEM((n_chains, TQ, LANES), F32), pltpu.VMEM((n_chains, TQ, LANES), F32),
            pltpu.VMEM((n_chains, TQ, 1), F32), pltpu.VMEM((n_chains, TQ, LANES), F32)]


def _dsa_kernel(q_ref, k_ref, v_ref, iq_ref, ik_ref, iw_ref, o_ref, sc_ref, dmat_ref, *scratch, k_sel):
    i = pl.program_id(1)
    seq_len = sc_ref.shape[1]
    t0 = i * TQ
    n_chunks = (t0 + TQ + KC - 1) // KC
    idx_scale = (D_IDX ** -0.5) * (N_IDX_HEADS ** -0.5)
    row = t0 + lax.broadcasted_iota(I32, (TQ, KC), 0)
    col_in_chunk = lax.broadcasted_iota(I32, (TQ, KC), 1)

    iq = iq_ref[...]
    iw = iw_ref[:, 0:N_IDX_HEADS]

    def score_body(c, _):
        k0 = pl.multiple_of(c * KC, KC)
        ik = ik_ref[pl.ds(k0, KC), 0:D_IDX]
        sc = jnp.zeros((TQ, KC), F32)
        for h in range(N_IDX_HEADS):
            rel = lax.dot_general(iq[:, h * D_IDX:(h + 1) * D_IDX], ik,
                                  (((1,), (1,)), ((), ())), preferred_element_type=F32)
            sc = sc + iw[:, h:h + 1] * jnp.maximum(rel, 0.0)
        causal = (k0 + col_in_chunk) <= row
        sc_ref[:, pl.ds(k0, KC)] = jnp.where(causal, sc * idx_scale, -jnp.inf)
        return 0

    lax.fori_loop(0, n_chunks, score_body, 0)

    def code_to_float(code):
        return lax.bitcast_convert_type(jnp.where(code < 0, code ^ 0x7FFFFFFF, code), F32)

    def count(pred):
        def body(c, acc):
            k0 = pl.multiple_of(c * KC, KC)
            for t in _lane_tiles(sc_ref[:, pl.ds(k0, KC)]):
                acc = acc + jnp.where(pred(t), 1.0, 0.0)
            return acc
        acc = lax.fori_loop(0, n_chunks, body, jnp.zeros((TQ, LANES), F32))
        return jnp.broadcast_to(jnp.sum(acc, axis=1, keepdims=True), (TQ, LANES))

    kf = float(k_sel)
    has_k = count(lambda t: t > -jnp.inf) >= kf
    code = jnp.where(count(lambda t: t >= 0.0) >= kf, 0, INT_MIN).astype(I32)

    def bit_body(b, code):
        cand = code + lax.shift_left(jnp.int32(1), 30 - b)
        cand_f = code_to_float(cand)
        return jnp.where(count(lambda t: t >= cand_f) >= kf, cand, code)

    code = lax.fori_loop(0, 31, bit_body, code)
    thr = jnp.where(has_k, code_to_float(code), -jnp.inf)
    need = jnp.where(has_k, kf - count(lambda t: t > thr), float(seq_len))

    tri = (lax.broadcasted_iota(I32, (KC, KC), 0) <= lax.broadcasted_iota(I32, (KC, KC), 1)).astype(BF16)
    thr_w = jnp.concatenate([thr] * (KC // LANES), axis=1)
    need_w = jnp.concatenate([need] * (KC // LANES), axis=1)

    def mask_body(c, ties_before):
        k0 = pl.multiple_of(c * KC, KC)
        sc = sc_ref[:, pl.ds(k0, KC)]
        eq = sc == thr_w
        eqf = jnp.where(eq, 1.0, 0.0)
        rank = ties_before + jnp.dot(eqf.astype(BF16), tri, preferred_element_type=F32)
        dist = row - (k0 + col_in_chunk)
        sel = ((sc > thr_w) | (eq & (rank <= need_w))) & (dist >= 0)
        dmat_ref[:, pl.ds(k0, KC)] = jnp.where(sel, dist.astype(F32), jnp.inf)
        return ties_before + jnp.sum(eqf, axis=1, keepdims=True)

    lax.fori_loop(0, n_chunks, mask_body, jnp.zeros((TQ, 1), F32))

    slopes = _alibi_slopes(N_HEADS_SPARSE)
    n_tiles = N_HEADS_SPARSE * D_HEAD // LANES
    _split_head_pairs(q_ref, scratch[0], n_tiles, D_HEAD ** -0.5)
    chains = [(h // 2, h // 2, slopes[h]) for h in range(N_HEADS_SPARSE)]
    _flash_chains(chains, k_ref, v_ref, dmat_ref, None, 0, n_chunks, scratch)
    _merge_head_pairs(o_ref, scratch[-2], scratch[-1], n_tiles)


def _dsa_attention(ha, hb, batch, seq):
    n = hb.shape[0]
    nq = seq // TQ
    k_sel = min(TOPK_LIMIT, seq // 4)
    width = N_HEADS_SPARSE * D_HEAD
    iq_w = N_IDX_HEADS * D_IDX
    return pl.pallas_call(
        functools.partial(_dsa_kernel, k_sel=k_sel),
        grid=(batch, nq),
        in_specs=[
            pl.BlockSpec((TQ, width), lambda b, i: (b * nq + i, 0)),
            pl.BlockSpec((seq, width), lambda b, i: (b, 1)),
            pl.BlockSpec((seq, width), lambda b, i: (b, 2)),
            pl.BlockSpec((TQ, iq_w), lambda b, i: (b * nq + i, 3 * width // iq_w)),
            pl.BlockSpec((seq, LANES), lambda b, i: (b, (3 * width + iq_w) // LANES)),
            pl.BlockSpec((TQ, LANES), lambda b, i: (b * nq + i, 3 * CONV_W // LANES)),
        ],
        out_specs=pl.BlockSpec((TQ, width), lambda b, i: (b * nq + i, 0)),
        out_shape=jax.ShapeDtypeStruct((n, width), BF16),
        scratch_shapes=[pltpu.VMEM((TQ, seq), F32), pltpu.VMEM((TQ, seq), F32)] + _flash_scratch(N_HEADS_SPARSE, seq),
        compiler_params=_cparams("arbitrary", "arbitrary"),
        name="dsa_attention",
    )(hb, hb, hb, hb, hb, ha)


def _diff_kernel(q_ref, k_ref, v_ref, lam_ref, g_ref, o_ref, dmat_ref, *scratch, lam_init):
    i = pl.program_id(1)
    t0 = i * TQ
    n_chunks = (t0 + TQ + KC - 1) // KC
    row = t0 + lax.broadcasted_iota(I32, (TQ, KC), 0)
    col_in_chunk = lax.broadcasted_iota(I32, (TQ, KC), 1)

    def mask_body(c, _):
        k0 = pl.multiple_of(c * KC, KC)
        dist = row - (k0 + col_in_chunk)
        dmat_ref[:, pl.ds(k0, KC)] = jnp.where(dist >= 0, dist.astype(F32), jnp.inf)
        return 0

    lax.fori_loop(0, n_chunks, mask_body, 0)

    lv = lam_ref[...]
    lam = (jnp.exp(jnp.sum(lv[0:1, :] * lv[1:2, :], axis=1, keepdims=True))
           - jnp.exp(jnp.sum(lv[2:3, :] * lv[3:4, :], axis=1, keepdims=True)) + lam_init)

    slopes = _alibi_slopes(N_HEADS_DIFF)
    _split_head_pairs(q_ref, scratch[0], N_HEADS_DIFF, D_HEAD ** -0.5)
    chains = [(n // 2, n // 2, slopes[n // 2]) for n in range(2 * N_HEADS_DIFF)]
    _flash_chains(chains, k_ref, v_ref, dmat_ref, None, 0, n_chunks, scratch)
    l_ref, acc_ref = scratch[-2], scratch[-1]
    g = g_ref[...]
    for h in range(N_HEADS_DIFF):
        of = acc_ref[2 * h] / l_ref[2 * h] - lam * (acc_ref[2 * h + 1] / l_ref[2 * h + 1])
        of = of * lax.rsqrt(jnp.mean(of * of, axis=1, keepdims=True) + RMS_EPS) * g
        o_ref[:, h * LANES:(h + 1) * LANES] = (of * (1.0 - lam_init)).astype(o_ref.dtype)


def _diff_attention(hb, lam_vecs, subln_g, batch, seq, layer):
    n = hb.shape[0]
    nq = seq // TQ
    width = N_HEADS_DIFF * 2 * D_HEAD
    lam_init = 0.8 - 0.6 * math.exp(-0.3 * layer)
    return pl.pallas_call(
        functools.partial(_diff_kernel, lam_init=lam_init),
        grid=(batch, nq),
        in_specs=[
            pl.BlockSpec((TQ, width), lambda b, i: (b * nq + i, 0)),
            pl.BlockSpec((seq, width), lambda b, i: (b, 1)),
            pl.BlockSpec((seq, width), lambda b, i: (b, 2)),
            pl.BlockSpec(lam_vecs.shape, lambda b, i: (0, 0)),
            pl.BlockSpec(subln_g.shape, lambda b, i: (0, 0)),
        ],
        out_specs=pl.BlockSpec((TQ, width), lambda b, i: (b * nq + i, 0)),
        out_shape=jax.ShapeDtypeStruct((n, width), BF16),
        scratch_shapes=[pltpu.VMEM((TQ, seq), F32)] + _flash_scratch(2 * N_HEADS_DIFF, seq),
        compiler_params=_cparams("arbitrary", "arbitrary"),
        name="diff_attention",
    )(hb, hb, hb, lam_vecs, subln_g)


def _dilated_kernel(q_ref, k_ref, v_ref, o_ref, dmat_ref, amat_ref, *scratch):
    i = pl.program_id(1)
    t0 = i * TQ
    w_max = max(w for w, _ in DIL_GROUPS)
    c_lo = jnp.maximum(t0 - w_max, 0) // KC
    c_hi = (t0 + TQ + KC - 1) // KC
    row = t0 + lax.broadcasted_iota(I32, (TQ, KC), 0)
    col_in_chunk = lax.broadcasted_iota(I32, (TQ, KC), 1)

    def mask_body(c, _):
        k0 = pl.multiple_of(c * KC, KC)
        dist = row - (k0 + col_in_chunk)
        mult = jnp.zeros((TQ, KC), F32)
        for w, d in DIL_GROUPS:
            member = (dist >= 0) & (dist <= w) & ((dist & (d - 1)) == 0)
            mult = mult + jnp.where(member, 1.0, 0.0)
        on = mult > 0.0
        dmat_ref[:, pl.ds(k0, KC)] = jnp.where(on, dist.astype(F32), jnp.inf)
        amat_ref[:, pl.ds(k0, KC)] = jnp.log2(jnp.where(on, mult, 1.0))
        return 0

    lax.fori_loop(c_lo, c_hi, mask_body, 0)

    slopes = _alibi_slopes(N_HEADS_DIL)
    n_tiles = N_HEADS_DIL * D_HEAD // LANES
    _split_head_pairs(q_ref, scratch[0], n_tiles, D_HEAD ** -0.5)
    chains = [(h // 2, h // 2, slopes[h]) for h in range(N_HEADS_DIL)]
    _flash_chains(chains, k_ref, v_ref, dmat_ref, amat_ref, c_lo, c_hi, scratch)
    _merge_head_pairs(o_ref, scratch[-2], scratch[-1], n_tiles)


def _dilated_attention(hb, batch, seq):
    n = hb.shape[0]
    nq = seq // TQ
    width = N_HEADS_DIL * D_HEAD
    return pl.pallas_call(
        _dilated_kernel,
        grid=(batch, nq),
        in_specs=[
            pl.BlockSpec((TQ, width), lambda b, i: (b * nq + i, 3)),
            pl.BlockSpec((seq, width), lambda b, i: (b, 4)),
            pl.BlockSpec((seq, width), lambda b, i: (b, 5)),
        ],
        out_specs=pl.BlockSpec((TQ, width), lambda b, i: (b * nq + i, 0)),
        out_shape=jax.ShapeDtypeStruct((n, width), BF16),
        scratch_shapes=[pltpu.VMEM((TQ, seq), F32), pltpu.VMEM((TQ, seq), F32)] + _flash_scratch(N_HEADS_DIL, seq),
        compiler_params=_cparams("arbitrary", "arbitrary"),
        name="dilated_attention",
    )(hb, hb, hb)


def _mix_out_kernel(ya_ref, yb_ref, wa_ref, wb_ref, x_ref, g_ref, b_ref, rw_ref, rb_ref,
                    x1_ref, eidx_ref, gate_ref, rank_ref, cnt_ref, carry_ref):
    step = pl.program_id(0)

    @pl.when(step == 0)
    def _():
        carry_ref[...] = jnp.zeros_like(carry_ref)

    m = (jnp.dot(ya_ref[...], wa_ref[...], preferred_element_type=F32)
         + jnp.dot(yb_ref[...], wb_ref[...], preferred_element_type=F32))
    x1 = _layer_norm(DEEPNORM_ALPHA * x_ref[...] + m, g_ref[...], b_ref[...])
    x1_ref[...] = x1

    t = x1.shape[0]
    logits = jnp.dot(x1.astype(BF16), rw_ref[...], preferred_element_type=F32) + rb_ref[...]
    lane = lax.broadcasted_iota(I32, (t, LANES), 1)
    vals, idxs = [], []
    lg = logits
    for _ in range(TOP_K):
        mx = jnp.max(lg, axis=1, keepdims=True)
        ix = jnp.min(jnp.where(lg == mx, lane, LANES), axis=1, keepdims=True)
        vals.append(mx)
        idxs.append(ix)
        lg = jnp.where(lane == ix, -jnp.inf, lg)
    exps = [jnp.exp(v - vals[0]) for v in vals]
    denom = exps[0]
    for e in exps[1:]:
        denom = denom + e

    onehot = jnp.zeros((t, LANES), F32)
    for ix in idxs:
        onehot = onehot + jnp.where(lane == ix, 1.0, 0.0)
    strict = (lax.broadcasted_iota(I32, (t, t), 1) < lax.broadcasted_iota(I32, (t, t), 0)).astype(BF16)
    before = jnp.dot(strict, onehot.astype(BF16), preferred_element_type=F32) + carry_ref[...]

    eidx = jnp.zeros((t, LANES), I32)
    gate = jnp.zeros((t, LANES), F32)
    rank = jnp.zeros((t, LANES), I32)
    for k in range(TOP_K):
        rk = jnp.sum(jnp.where(lane == idxs[k], before, 0.0), axis=1, keepdims=True)
        eidx = jnp.where(lane == k, idxs[k], eidx)
        gate = jnp.where(lane == k, exps[k] / denom, gate)
        rank = jnp.where(lane == k, rk.astype(I32), rank)
    eidx_ref[...] = eidx
    gate_ref[...] = gate
    rank_ref[...] = rank
    carry_ref[...] = carry_ref[...] + jnp.sum(onehot, axis=0, keepdims=True)
    cnt_ref[...] = carry_ref[...]


def _mix_out(ya, yb, w_out, x2d, ln_g, ln_b, router_w, router_b):
    n, d = x2d.shape
    half = ya.shape[1]
    wa = w_out[:half].astype(BF16)
    wb = w_out[half:].astype(BF16)
    rw = jnp.zeros((d, LANES), BF16).at[:, :N_EXPERTS].set(router_w.astype(BF16))
    rb = jnp.full((1, LANES), -jnp.inf, F32).at[0, :N_EXPERTS].set(router_b)
    t = ROW_TILE
    full = lambda a: pl.BlockSpec(a.shape, lambda i: (0, 0))
    rows = lambda w: pl.BlockSpec((t, w), lambda i: (i, 0))
    g2, b2 = ln_g.reshape(1, d), ln_b.reshape(1, d)
    return pl.pallas_call(
        _mix_out_kernel,
        grid=(n // t,),
        in_specs=[rows(half), rows(half), full(wa), full(wb), rows(d), full(g2), full(b2), full(rw), full(rb)],
        out_specs=[rows(d), rows(LANES), rows(LANES), rows(LANES), pl.BlockSpec((1, LANES), lambda i: (0, 0))],
        out_shape=[jax.ShapeDtypeStruct((n, d), F32), jax.ShapeDtypeStruct((n, LANES), I32),
                   jax.ShapeDtypeStruct((n, LANES), F32), jax.ShapeDtypeStruct((n, LANES), I32),
                   jax.ShapeDtypeStruct((1, LANES), F32)],
        scratch_shapes=[pltpu.VMEM((1, LANES), F32)],
        compiler_params=_cparams("arbitrary"),
        name="mix_out_router",
    )(ya, yb, wa, wb, x2d, g2, b2, rw, rb)


def _row_copy(src_hbm, dst_ref, src_row, dst_row, sem):
    return pltpu.make_async_copy(src_hbm.at[pl.ds(src_row, 1)], dst_ref.at[pl.ds(dst_row, 1)], sem)


def _rows_wait(src_hbm, dst_ref, sem):
    pltpu.make_async_copy(src_hbm.at[pl.ds(0, dst_ref.shape[0])], dst_ref, sem).wait()


def _expert_kernel(blk_e_ref, n_used_ref, tok_ref, tok_next_ref, x_hbm, wgu_ref, bgu_ref, wdn_ref, bdn_ref,
                   o_ref, xbuf, wgu_bf, wdn_bf, sem):
    i = pl.program_id(0)
    n_used = n_used_ref[0]
    slot = i % 2
    e = blk_e_ref[i]
    e_prev = blk_e_ref[jnp.maximum(i - 1, 0)]

    def gather(rows_ref, s):
        def body(r, _):
            _row_copy(x_hbm, xbuf.at[s], rows_ref[r], r, sem.at[s]).start()
            return 0
        lax.fori_loop(0, MOE_BM, body, 0, unroll=8)

    def gather_inline(rows_ref, s):
        for r in range(MOE_BM):
            _row_copy(x_hbm, xbuf.at[s], rows_ref[r], r, sem.at[s]).start()

    @pl.when(i == 0)
    def _():
        gather(tok_ref, 0)

    @pl.when((i == 0) | (e != e_prev))
    def _():
        wgu_bf[...] = wgu_ref[...].astype(BF16)
        wdn_bf[...] = wdn_ref[...].astype(BF16)

    @pl.when(i < n_used)
    def _():
        _rows_wait(x_hbm, xbuf.at[slot], sem.at[slot])
        x = xbuf[slot].astype(BF16)
        gather_inline(tok_next_ref, 1 - slot)
        h = jnp.dot(x, wgu_bf[...], preferred_element_type=F32) + bgu_ref[...]
        gate = jnp.minimum(h[:, :D_FF], SWIGLU_LIMIT)
        up = jnp.clip(h[:, D_FF:], -SWIGLU_LIMIT, SWIGLU_LIMIT)
        glu = gate * (1.0 / (1.0 + jnp.exp(-SWIGLU_ALPHA * gate)))
        act = ((up + 1.0) * glu).astype(BF16)
        o_ref[...] = jnp.dot(act, wdn_bf[...], preferred_element_type=F32) + bdn_ref[...]

    @pl.when(i >= n_used)
    def _():
        _rows_wait(x_hbm, xbuf.at[slot], sem.at[slot])
        gather(tok_next_ref, 1 - slot)
        o_ref[...] = jnp.zeros_like(o_ref)

    @pl.when(i == pl.num_programs(0) - 1)
    def _():
        _rows_wait(x_hbm, xbuf.at[1 - slot], sem.at[1 - slot])


def _expert_ffn(x2d, row_tok, blk_e, n_used, w_gu, b_gu, w_dn, b_dn):
    n_rows = row_tok.shape[0]
    d = x2d.shape[1]
    n_blocks = n_rows // MOE_BM
    ne = w_gu.shape[0]
    grid_spec = pltpu.PrefetchScalarGridSpec(
        num_scalar_prefetch=2,
        grid=(n_blocks,),
        in_specs=[
            pl.BlockSpec((MOE_BM,), lambda i, be, nu: (i,), memory_space=pltpu.SMEM),
            pl.BlockSpec((MOE_BM,), lambda i, be, nu: (jnp.minimum(i + 1, n_blocks - 1),),
                         memory_space=pltpu.SMEM),
            pl.BlockSpec(memory_space=pl.ANY),
            pl.BlockSpec((None, d, 2 * D_FF), lambda i, be, nu: (be[i], 0, 0)),
            pl.BlockSpec((None, 1, 2 * D_FF), lambda i, be, nu: (be[i], 0, 0)),
            pl.BlockSpec((None, D_FF, d), lambda i, be, nu: (be[i], 0, 0)),
            pl.BlockSpec((None, 1, d), lambda i, be, nu: (be[i], 0, 0)),
        ],
        out_specs=pl.BlockSpec((MOE_BM, d), lambda i, be, nu: (i, 0)),
        scratch_shapes=[pltpu.VMEM((2, MOE_BM, d), F32), pltpu.VMEM((d, 2 * D_FF), BF16),
                        pltpu.VMEM((D_FF, d), BF16), pltpu.SemaphoreType.DMA((2,))],
    )
    return pl.pallas_call(
        _expert_kernel,
        grid_spec=grid_spec,
        out_shape=jax.ShapeDtypeStruct((n_rows, d), F32),
        compiler_params=_cparams("arbitrary"),
        name="moe_experts",
    )(blk_e, n_used, row_tok, row_tok, x2d, w_gu, b_gu.reshape(ne, 1, 2 * D_FF), w_dn, b_dn.reshape(ne, 1, d))


def _combine_kernel(dest_ref, dest_next_ref, y_hbm, gate_ref, x_ref, g_ref, b_ref, o_ref, buf_ref, sem):
    i = pl.program_id(0)
    t = x_ref.shape[0]
    slot = i % 2

    def gather(rows_ref, s):
        def body(r, _):
            for k in range(TOP_K):
                _row_copy(y_hbm, buf_ref.at[s, k], rows_ref[r * TOP_K + k], r, sem.at[s]).start()
            return 0
        lax.fori_loop(0, t, body, 0, unroll=2)

    @pl.when(i == 0)
    def _():
        gather(dest_ref, 0)

    @pl.when(i + 1 < pl.num_programs(0))
    def _():
        gather(dest_next_ref, 1 - slot)

    for k in range(TOP_K):
        _rows_wait(y_hbm, buf_ref.at[slot, k], sem.at[slot])
    gate = gate_ref[...]
    f = gate[:, 0:1] * buf_ref[slot, 0]
    for k in range(1, TOP_K):
        f = f + gate[:, k:k + 1] * buf_ref[slot, k]
    o_ref[...] = _layer_norm(DEEPNORM_ALPHA * x_ref[...] + f, g_ref[...], b_ref[...])


def _combine(ys, dest_flat, gates, x2d, ln_g, ln_b):
    n, d = x2d.shape
    t = GATHER_T
    g2, b2 = ln_g.reshape(1, d), ln_b.reshape(1, d)
    full = lambda a: pl.BlockSpec(a.shape, lambda i: (0, 0))
    n_steps = n // t
    return pl.pallas_call(
        _combine_kernel,
        grid=(n_steps,),
        in_specs=[pl.BlockSpec((t * TOP_K,), lambda i: (i,), memory_space=pltpu.SMEM),
                  pl.BlockSpec((t * TOP_K,), lambda i: (jnp.minimum(i + 1, n_steps - 1),),
                               memory_space=pltpu.SMEM),
                  pl.BlockSpec(memory_space=pl.ANY),
                  pl.BlockSpec((t, LANES), lambda i: (i, 0)),
                  pl.BlockSpec((t, d), lambda i: (i, 0)),
                  full(g2), full(b2)],
        out_specs=pl.BlockSpec((t, d), lambda i: (i, 0)),
        out_shape=jax.ShapeDtypeStruct((n, d), F32),
        scratch_shapes=[pltpu.VMEM((2, TOP_K, t, d), F32), pltpu.SemaphoreType.DMA((2,))],
        compiler_params=_cparams("arbitrary"),
        name="moe_combine",
    )(dest_flat, dest_flat, ys, gates, x2d, g2, b2)


def _moe_block(x1, eidx, gates, rank, counts, w_gu, b_gu, w_dn, b_dn, ln_g, ln_b):
    n = x1.shape[0]
    e_sel = eidx[:, :TOP_K]
    cnt = counts[0, :N_EXPERTS].astype(I32)
    padded = ((cnt + MOE_BM - 1) // MOE_BM) * MOE_BM
    pad_end = jnp.cumsum(padded)
    pad_start = pad_end - padded
    dest = pad_start[e_sel] + rank[:, :TOP_K]
    n_rows = ((n * TOP_K + N_EXPERTS * (MOE_BM - 1) + MOE_BM - 1) // MOE_BM) * MOE_BM
    n_blocks = n_rows // MOE_BM
    dest_flat = dest.reshape(-1)
    tok = jnp.arange(n * TOP_K, dtype=I32) // TOP_K
    row_tok = jnp.zeros((n_rows,), I32).at[dest_flat].set(tok)
    blk_start = jnp.arange(n_blocks, dtype=I32) * MOE_BM
    blk_e = jnp.minimum(jnp.sum((pad_end[None, :] <= blk_start[:, None]).astype(I32), axis=1), N_EXPERTS - 1)
    n_used = (pad_end[-1:] // MOE_BM).astype(I32)
    ys = _expert_ffn(x1, row_tok, blk_e, n_used, w_gu, b_gu, w_dn, b_dn)
    return _combine(ys, dest_flat, gates, x1, ln_g, ln_b)


def _pad_cols(w, width):
    return jnp.pad(w, ((0, 0), (0, width - w.shape[1])))


def kernel(x, w_in_0, conv_w_0, w_out_0, ln_mix_g_0, ln_mix_b_0, router_w_0, router_b_0, w_gu_0, b_gu_0, w_dn_0, b_dn_0, ln_ffn_g_0, ln_ffn_b_0, w_in_1, lam_q1_1, lam_k1_1, lam_q2_1, lam_k2_1, subln_g_1, w_out_1, ln_mix_g_1, ln_mix_b_1, router_w_1, router_b_1, w_gu_1, b_gu_1, w_dn_1, b_dn_1, ln_ffn_g_1, ln_ffn_b_1):
    batch, seq, d = x.shape
    x0 = x.reshape(batch * seq, d)

    n_f32 = 3 * CONV_W
    n_attn = 3 * N_HEADS_SPARSE * D_HEAD + N_IDX_HEADS * D_IDX + D_IDX
    w_a = _pad_cols(jnp.concatenate([w_in_0[:, :n_f32], w_in_0[:, n_f32 + n_attn:]], axis=1),
                    n_f32 + LANES)
    w_b = _pad_cols(w_in_0[:, n_f32:n_f32 + n_attn], 3 * N_HEADS_SPARSE * D_HEAD + N_IDX_HEADS * D_IDX + LANES)
    ha, hb = _project(x0, [w_a.astype(BF16), w_b.astype(BF16)], [F32, BF16])
    ya = _short_conv(ha, conv_w_0, batch, seq)
    yb = _dsa_attention(ha, hb, batch, seq)
    x1, eidx, gates, rank, counts = _mix_out(ya, yb, w_out_0, x0, ln_mix_g_0, ln_mix_b_0, router_w_0, router_b_0)
    x2 = _moe_block(x1, eidx, gates, rank, counts, w_gu_0, b_gu_0, w_dn_0, b_dn_0, ln_ffn_g_0, ln_ffn_b_0)

    (hc,) = _project(x2, [w_in_1.astype(BF16)], [BF16])
    lam_vecs = jnp.stack([lam_q1_1, lam_k1_1, lam_q2_1, lam_k2_1]).astype(F32)
    yc = _diff_attention(hc, lam_vecs, subln_g_1.reshape(1, -1).astype(F32), batch, seq, 1)
    yd = _dilated_attention(hc, batch, seq)
    x3, eidx, gates, rank, counts = _mix_out(yc, yd, w_out_1, x2, ln_mix_g_1, ln_mix_b_1, router_w_1, router_b_1)
    x4 = _moe_block(x3, eidx, gates, rank, counts, w_gu_1, b_gu_1, w_dn_1, b_dn_1, ln_ffn_g_1, ln_ffn_b_1)
    return x4.reshape(batch, seq, d)
```

```python
import functools
import math

import jax
import jax.numpy as jnp
from jax import lax
from jax.experimental import pallas as pl
from jax.experimental.pallas import tpu as pltpu

F32 = jnp.float32
BF16 = jnp.bfloat16
I32 = jnp.int32

CONV_W = 512
CONV_TAPS = 3
N_HEADS_SPARSE = 8
D_HEAD = 64
N_IDX_HEADS = 8
D_IDX = 32
TOPK_LIMIT = 256
N_HEADS_DIFF = 4
N_HEADS_DIL = 8
DIL_GROUPS = ((128, 1), (512, 4), (2048, 16))
N_EXPERTS = 32
TOP_K = 4
D_FF = 1024
SWIGLU_LIMIT = 7.0
SWIGLU_ALPHA = 1.702
DEPTH = 2
DEEPNORM_ALPHA = (2 * DEPTH) ** 0.25
LN_EPS = 1e-5
RMS_EPS = 1e-5

LANES = 128
VMEM_LIMIT = 56 * 1024 * 1024
TQ = 128
KC = 512
ROW_TILE = 256
MOE_BM = 256
GATHER_T = 128
COUNT_ROWS = 64
LOG2_E = math.log2(math.e)
INT_MIN = -2 ** 31


def _alibi_slopes(n):
    return [2.0 ** (-8.0 * (h + 1) / n) for h in range(n)]


def _cparams(*sem):
    return pltpu.CompilerParams(dimension_semantics=sem, vmem_limit_bytes=VMEM_LIMIT)


def _layer_norm(z, g, b):
    mu = jnp.mean(z, axis=-1, keepdims=True)
    zc = z - mu
    var = jnp.mean(zc * zc, axis=-1, keepdims=True)
    return zc * lax.rsqrt(var + LN_EPS) * g + b


def _proj_kernel(*refs, n_out):
    x_ref = refs[0]
    w_refs = refs[1:1 + n_out]
    o_refs = refs[1 + n_out:]
    xb = x_ref[...].astype(BF16)
    for w_ref, o_ref in zip(w_refs, o_refs):
        o_ref[...] = jnp.dot(xb, w_ref[...], preferred_element_type=F32).astype(o_ref.dtype)


def _project(x2d, weights, out_dtypes):
    n, d = x2d.shape
    n_out = len(weights)
    in_specs = [pl.BlockSpec((ROW_TILE, d), lambda i: (i, 0))]
    in_specs += [pl.BlockSpec(w.shape, lambda i: (0, 0)) for w in weights]
    out_specs = [pl.BlockSpec((ROW_TILE, w.shape[1]), lambda i: (i, 0)) for w in weights]
    out_shape = [jax.ShapeDtypeStruct((n, w.shape[1]), dt) for w, dt in zip(weights, out_dtypes)]
    return pl.pallas_call(
        functools.partial(_proj_kernel, n_out=n_out),
        grid=(n // ROW_TILE,),
        in_specs=in_specs, out_specs=out_specs, out_shape=out_shape,
        compiler_params=_cparams("parallel"),
        name="in_proj",
    )(x2d, *weights)


def _conv_kernel(gb_ref, gc_ref, xa_ref, w_ref, o_ref, prev_ref):
    j = pl.program_id(1)
    t = gb_ref.shape[0]

    @pl.when(j == 0)
    def _():
        prev_ref[...] = jnp.zeros_like(prev_ref)

    z = gc_ref[...] * xa_ref[...]
    row = lax.broadcasted_iota(I32, z.shape, 0)
    prev2 = prev_ref[0:1, :]
    prev1 = prev_ref[1:2, :]
    z1 = jnp.where(row == 0, prev1, pltpu.roll(z, 1, 0))
    z2 = jnp.where(row == 0, prev2, jnp.where(row == 1, prev1, pltpu.roll(z, 2, 0)))
    w = w_ref[...]
    y = w[0:1, :] * z + w[1:2, :] * z1 + w[2:3, :] * z2
    o_ref[...] = (gb_ref[...] * y).astype(o_ref.dtype)
    prev_ref[0:1, :] = z[t - 2:t - 1, :]
    prev_ref[1:2, :] = z[t - 1:t, :]


def _short_conv(ha, conv_w, batch, seq):
    n = ha.shape[0]
    t = min(512, seq)
    nj = seq // t
    spec = lambda c: pl.BlockSpec((t, CONV_W), lambda b, j, c=c: (b * nj + j, c))
    return pl.pallas_call(
        _conv_kernel,
        grid=(batch, nj),
        in_specs=[spec(0), spec(1), spec(2), pl.BlockSpec((CONV_TAPS, CONV_W), lambda b, j: (0, 0))],
        out_specs=pl.BlockSpec((t, CONV_W), lambda b, j: (b * nj + j, 0)),
        out_shape=jax.ShapeDtypeStruct((n, CONV_W), BF16),
        scratch_shapes=[pltpu.VMEM((8, CONV_W), F32)],
        compiler_params=_cparams("arbitrary", "arbitrary"),
        name="short_conv",
    )(ha, ha, ha, conv_w)


def _lane_tiles(x):
    return [x[:, u * LANES:(u + 1) * LANES] for u in range(x.shape[1] // LANES)]


def _flash_chains(chains, k_ref, v_ref, dmat_ref, amat_ref, c_lo, c_hi, scratch):
    qm_ref, s_ref, mx_ref, lp_ref, l_ref, acc_ref = scratch
    mx_ref[...] = jnp.full(mx_ref.shape, -jnp.inf, F32)

    def pass_a(c, _):
        k0 = pl.multiple_of(c * KC, KC)
        for n, (k_tile, _, slope) in enumerate(chains):
            kc = k_ref[pl.ds(k0, KC), k_tile * LANES:(k_tile + 1) * LANES]
            s = lax.dot_general(qm_ref[n], kc, (((1,), (1,)), ((), ())), preferred_element_type=F32)
            s = s * LOG2_E - (slope * LOG2_E) * dmat_ref[:, pl.ds(k0, KC)]
            if amat_ref is not None:
                s = s + amat_ref[:, pl.ds(k0, KC)]
            s_ref[n, :, pl.ds(k0, KC)] = s
            part = mx_ref[n]
            for t in _lane_tiles(s):
                part = jnp.maximum(part, t)
            mx_ref[n] = part
        return 0

    lax.fori_loop(c_lo, c_hi, pass_a, 0)

    for n in range(len(chains)):
        row_max = jnp.max(mx_ref[n], axis=1, keepdims=True)
        mx_ref[n] = jnp.broadcast_to(row_max, (TQ, LANES))
    lp_ref[...] = jnp.zeros(lp_ref.shape, F32)
    acc_ref[...] = jnp.zeros(acc_ref.shape, F32)

    def pass_b(c, _):
        k0 = pl.multiple_of(c * KC, KC)
        for n, (_, v_tile, _) in enumerate(chains):
            row_max = mx_ref[n]
            p_tiles = [jnp.exp2(t - row_max) for t in _lane_tiles(s_ref[n, :, pl.ds(k0, KC)])]
            part = lp_ref[n]
            for t in p_tiles:
                part = part + t
            lp_ref[n] = part
            p = jnp.concatenate(p_tiles, axis=1).astype(BF16)
            vc = v_ref[pl.ds(k0, KC), v_tile * LANES:(v_tile + 1) * LANES]
            acc_ref[n] = acc_ref[n] + jnp.dot(p, vc, preferred_element_type=F32)
        return 0

    lax.fori_loop(c_lo, c_hi, pass_b, 0)

    for n in range(len(chains)):
        l_ref[n] = jnp.sum(lp_ref[n], axis=1, keepdims=True)


def _split_head_pairs(q_ref, qm_ref, n_tiles, scale):
    low = lax.broadcasted_iota(I32, (TQ, LANES), 1) < D_HEAD
    for j in range(n_tiles):
        qt = q_ref[:, j * LANES:(j + 1) * LANES] * scale
        qm_ref[2 * j] = jnp.where(low, qt, 0).astype(BF16)
        qm_ref[2 * j + 1] = jnp.where(low, 0, qt).astype(BF16)


def _merge_head_pairs(o_ref, l_ref, acc_ref, n_tiles):
    low = lax.broadcasted_iota(I32, (TQ, LANES), 1) < D_HEAD
    for j in range(n_tiles):
        out = jnp.where(low, acc_ref[2 * j] / l_ref[2 * j], acc_ref[2 * j + 1] / l_ref[2 * j + 1])
        o_ref[:, j * LANES:(j + 1) * LANES] = out.astype(o_ref.dtype)


def _flash_scratch(n_chains, seq):
    return [pltpu.VMEM((n_chains, TQ, LANES), BF16), pltpu.VMEM((n_chains, TQ, seq), F32),
            pltpu.VMEM((n_chains, TQ, LANES), F32), pltpu.VMEM((n_chains, TQ, LANES), F32),
            pltpu.VMEM((n_chains, TQ, 1), F32), pltpu.VMEM((n_chains, TQ, LANES), F32)]


def _dsa_kernel(q_ref, k_ref, v_ref, iq_ref, ik_ref, iw_ref, o_ref, sc_ref, sct_ref, dmat_ref, *scratch, k_sel):
    i = pl.program_id(1)
    seq_len = sc_ref.shape[1]
    t0 = i * TQ
    n_chunks = (t0 + TQ + KC - 1) // KC
    idx_scale = (D_IDX ** -0.5) * (N_IDX_HEADS ** -0.5)
    row = t0 + lax.broadcasted_iota(I32, (TQ, KC), 0)
    col_in_chunk = lax.broadcasted_iota(I32, (TQ, KC), 1)

    iq = iq_ref[...]
    iw = iw_ref[:, 0:N_IDX_HEADS]

    def score_body(c, _):
        k0 = pl.multiple_of(c * KC, KC)
        ik = ik_ref[pl.ds(k0, KC), 0:D_IDX]
        sc = jnp.zeros((TQ, KC), F32)
        for h in range(N_IDX_HEADS):
            rel = lax.dot_general(iq[:, h * D_IDX:(h + 1) * D_IDX], ik,
                                  (((1,), (1,)), ((), ())), preferred_element_type=F32)
            sc = sc + iw[:, h:h + 1] * jnp.maximum(rel, 0.0)
        causal = (k0 + col_in_chunk) <= row
        sc = jnp.where(causal, sc * idx_scale, -jnp.inf)
        sc_ref[:, pl.ds(k0, KC)] = sc
        sct_ref[pl.ds(k0, KC), :] = sc.T
        return 0

    lax.fori_loop(0, n_chunks, score_body, 0)

    def code_to_float(code):
        return lax.bitcast_convert_type(jnp.where(code < 0, code ^ 0x7FFFFFFF, code), F32)

    def count(pred):
        def body(c, acc):
            k0 = pl.multiple_of(c * KC, KC)
            hit = jnp.where(pred(sct_ref[pl.ds(k0, KC), :]), 1.0, 0.0)
            return acc + jnp.sum(hit.reshape(KC // COUNT_ROWS, COUNT_ROWS, TQ), axis=0)
        acc = lax.fori_loop(0, n_chunks, body, jnp.zeros((COUNT_ROWS, TQ), F32))
        return jnp.sum(acc, axis=0, keepdims=True)

    kf = float(k_sel)
    has_k = count(lambda t: t > -jnp.inf) >= kf
    code = jnp.where(count(lambda t: t >= 0.0) >= kf, 0, INT_MIN).astype(I32)

    def bit_body(b, code):
        cand = code + lax.shift_left(jnp.int32(1), 30 - b)
        cand_f = code_to_float(cand)
        return jnp.where(count(lambda t: t >= cand_f) >= kf, cand, code)

    code = lax.fori_loop(0, 31, bit_body, code)
    thr_q = jnp.where(has_k, code_to_float(code), -jnp.inf)
    need_q = jnp.where(has_k, kf - count(lambda t: t > thr_q), float(seq_len))
    thr = jnp.broadcast_to(thr_q, (LANES, TQ)).T
    need = jnp.broadcast_to(need_q, (LANES, TQ)).T

    tri = (lax.broadcasted_iota(I32, (KC, KC), 0) <= lax.broadcasted_iota(I32, (KC, KC), 1)).astype(BF16)
    thr_w = jnp.concatenate([thr] * (KC // LANES), axis=1)
    need_w = jnp.concatenate([need] * (KC // LANES), axis=1)

    def mask_body(c, ties_before):
        k0 = pl.multiple_of(c * KC, KC)
        sc = sc_ref[:, pl.ds(k0, KC)]
        eq = sc == thr_w
        eqf = jnp.where(eq, 1.0, 0.0)
        rank = ties_before + jnp.dot(eqf.astype(BF16), tri, preferred_element_type=F32)
        dist = row - (k0 + col_in_chunk)
        sel = ((sc > thr_w) | (eq & (rank <= need_w))) & (dist >= 0)
        dmat_ref[:, pl.ds(k0, KC)] = jnp.where(sel, dist.astype(F32), jnp.inf)
        return ties_before + jnp.sum(eqf, axis=1, keepdims=True)

    lax.fori_loop(0, n_chunks, mask_body, jnp.zeros((TQ, 1), F32))

    slopes = _alibi_slopes(N_HEADS_SPARSE)
    n_tiles = N_HEADS_SPARSE * D_HEAD // LANES
    _split_head_pairs(q_ref, scratch[0], n_tiles, D_HEAD ** -0.5)
    chains = [(h // 2, h // 2, slopes[h]) for h in range(N_HEADS_SPARSE)]
    _flash_chains(chains, k_ref, v_ref, dmat_ref, None, 0, n_chunks, scratch)
    _merge_head_pairs(o_ref, scratch[-2], scratch[-1], n_tiles)


def _dsa_attention(ha, hb, batch, seq):
    n = hb.shape[0]
    nq = seq // TQ
    k_sel = min(TOPK_LIMIT, seq // 4)
    width = N_HEADS_SPARSE * D_HEAD
    iq_w = N_IDX_HEADS * D_IDX
    return pl.pallas_call(
        functools.partial(_dsa_kernel, k_sel=k_sel),
        grid=(batch, nq),
        in_specs=[
            pl.BlockSpec((TQ, width), lambda b, i: (b * nq + i, 0)),
            pl.BlockSpec((seq, width), lambda b, i: (b, 1)),
            pl.BlockSpec((seq, width), lambda b, i: (b, 2)),
            pl.BlockSpec((TQ, iq_w), lambda b, i: (b * nq + i, 3 * width // iq_w)),
            pl.BlockSpec((seq, LANES), lambda b, i: (b, (3 * width + iq_w) // LANES)),
            pl.BlockSpec((TQ, LANES), lambda b, i: (b * nq + i, 3 * CONV_W // LANES)),
        ],
        out_specs=pl.BlockSpec((TQ, width), lambda b, i: (b * nq + i, 0)),
        out_shape=jax.ShapeDtypeStruct((n, width), BF16),
        scratch_shapes=[pltpu.VMEM((TQ, seq), F32), pltpu.VMEM((seq, TQ), F32), pltpu.VMEM((TQ, seq), F32)]
        + _flash_scratch(N_HEADS_SPARSE, seq),
        compiler_params=_cparams("arbitrary", "arbitrary"),
        name="dsa_attention",
    )(hb, hb, hb, hb, hb, ha)


def _diff_kernel(q_ref, k_ref, v_ref, lam_ref, g_ref, o_ref, dmat_ref, *scratch, lam_init):
    i = pl.program_id(1)
    t0 = i * TQ
    n_chunks = (t0 + TQ + KC - 1) // KC
    row = t0 + lax.broadcasted_iota(I32, (TQ, KC), 0)
    col_in_chunk = lax.broadcasted_iota(I32, (TQ, KC), 1)

    def mask_body(c, _):
        k0 = pl.multiple_of(c * KC, KC)
        dist = row - (k0 + col_in_chunk)
        dmat_ref[:, pl.ds(k0, KC)] = jnp.where(dist >= 0, dist.astype(F32), jnp.inf)
        return 0

    lax.fori_loop(0, n_chunks, mask_body, 0)

    lv = lam_ref[...]
    lam = (jnp.exp(jnp.sum(lv[0:1, :] * lv[1:2, :], axis=1, keepdims=True))
           - jnp.exp(jnp.sum(lv[2:3, :] * lv[3:4, :], axis=1, keepdims=True)) + lam_init)

    slopes = _alibi_slopes(N_HEADS_DIFF)
    _split_head_pairs(q_ref, scratch[0], N_HEADS_DIFF, D_HEAD ** -0.5)
    chains = [(n // 2, n // 2, slopes[n // 2]) for n in range(2 * N_HEADS_DIFF)]
    _flash_chains(chains, k_ref, v_ref, dmat_ref, None, 0, n_chunks, scratch)
    l_ref, acc_ref = scratch[-2], scratch[-1]
    g = g_ref[...]
    for h in range(N_HEADS_DIFF):
        of = acc_ref[2 * h] / l_ref[2 * h] - lam * (acc_ref[2 * h + 1] / l_ref[2 * h + 1])
        of = of * lax.rsqrt(jnp.mean(of * of, axis=1, keepdims=True) + RMS_EPS) * g
        o_ref[:, h * LANES:(h + 1) * LANES] = (of * (1.0 - lam_init)).astype(o_ref.dtype)


def _diff_attention(hb, lam_vecs, subln_g, batch, seq, layer):
    n = hb.shape[0]
    nq = seq // TQ
    width = N_HEADS_DIFF * 2 * D_HEAD
    lam_init = 0.8 - 0.6 * math.exp(-0.3 * layer)
    return pl.pallas_call(
        functools.partial(_diff_kernel, lam_init=lam_init),
        grid=(batch, nq),
        in_specs=[
            pl.BlockSpec((TQ, width), lambda b, i: (b * nq + i, 0)),
            pl.BlockSpec((seq, width), lambda b, i: (b, 1)),
            pl.BlockSpec((seq, width), lambda b, i: (b, 2)),
            pl.BlockSpec(lam_vecs.shape, lambda b, i: (0, 0)),
            pl.BlockSpec(subln_g.shape, lambda b, i: (0, 0)),
        ],
        out_specs=pl.BlockSpec((TQ, width), lambda b, i: (b * nq + i, 0)),
        out_shape=jax.ShapeDtypeStruct((n, width), BF16),
        scratch_shapes=[pltpu.VMEM((TQ, seq), F32)] + _flash_scratch(2 * N_HEADS_DIFF, seq),
        compiler_params=_cparams("arbitrary", "arbitrary"),
        name="diff_attention",
    )(hb, hb, hb, lam_vecs, subln_g)


def _dilated_kernel(q_ref, k_ref, v_ref, o_ref, dmat_ref, amat_ref, *scratch):
    i = pl.program_id(1)
    t0 = i * TQ
    w_max = max(w for w, _ in DIL_GROUPS)
    c_lo = jnp.maximum(t0 - w_max, 0) // KC
    c_hi = (t0 + TQ + KC - 1) // KC
    row = t0 + lax.broadcasted_iota(I32, (TQ, KC), 0)
    col_in_chunk = lax.broadcasted_iota(I32, (TQ, KC), 1)

    def mask_body(c, _):
        k0 = pl.multiple_of(c * KC, KC)
        dist = row - (k0 + col_in_chunk)
        mult = jnp.zeros((TQ, KC), F32)
        for w, d in DIL_GROUPS:
            member = (dist >= 0) & (dist <= w) & ((dist & (d - 1)) == 0)
            mult = mult + jnp.where(member, 1.0, 0.0)
        on = mult > 0.0
        dmat_ref[:, pl.ds(k0, KC)] = jnp.where(on, dist.astype(F32), jnp.inf)
        amat_ref[:, pl.ds(k0, KC)] = jnp.log2(jnp.where(on, mult, 1.0))
        return 0

    lax.fori_loop(c_lo, c_hi, mask_body, 0)

    slopes = _alibi_slopes(N_HEADS_DIL)
    n_tiles = N_HEADS_DIL * D_HEAD // LANES
    _split_head_pairs(q_ref, scratch[0], n_tiles, D_HEAD ** -0.5)
    chains = [(h // 2, h // 2, slopes[h]) for h in range(N_HEADS_DIL)]
    _flash_chains(chains, k_ref, v_ref, dmat_ref, amat_ref, c_lo, c_hi, scratch)
    _merge_head_pairs(o_ref, scratch[-2], scratch[-1], n_tiles)


def _dilated_attention(hb, batch, seq):
    n = hb.shape[0]
    nq = seq // TQ
    width = N_HEADS_DIL * D_HEAD
    return pl.pallas_call(
        _dilated_kernel,
        grid=(batch, nq),
        in_specs=[
            pl.BlockSpec((TQ, width), lambda b, i: (b * nq + i, 3)),
            pl.BlockSpec((seq, width), lambda b, i: (b, 4)),
            pl.BlockSpec((seq, width), lambda b, i: (b, 5)),
        ],
        out_specs=pl.BlockSpec((TQ, width), lambda b, i: (b * nq + i, 0)),
        out_shape=jax.ShapeDtypeStruct((n, width), BF16),
        scratch_shapes=[pltpu.VMEM((TQ, seq), F32), pltpu.VMEM((TQ, seq), F32)] + _flash_scratch(N_HEADS_DIL, seq),
        compiler_params=_cparams("arbitrary", "arbitrary"),
        name="dilated_attention",
    )(hb, hb, hb)


def _mix_out_kernel(ya_ref, yb_ref, wa_ref, wb_ref, x_ref, g_ref, b_ref, rw_ref, rb_ref,
                    x1_ref, x1t_ref, eidx_ref, gate_ref, rank_ref, cnt_ref, carry_ref):
    step = pl.program_id(0)

    @pl.when(step == 0)
    def _():
        carry_ref[...] = jnp.zeros_like(carry_ref)

    m = (jnp.dot(ya_ref[...], wa_ref[...], preferred_element_type=F32)
         + jnp.dot(yb_ref[...], wb_ref[...], preferred_element_type=F32))
    x1 = _layer_norm(DEEPNORM_ALPHA * x_ref[...] + m, g_ref[...], b_ref[...])
    x1_ref[...] = x1
    _to_token_tiles(x1t_ref, x1)

    t = x1.shape[0]
    logits = jnp.dot(x1.astype(BF16), rw_ref[...], preferred_element_type=F32) + rb_ref[...]
    lane = lax.broadcasted_iota(I32, (t, LANES), 1)
    vals, idxs = [], []
    lg = logits
    for _ in range(TOP_K):
        mx = jnp.max(lg, axis=1, keepdims=True)
        ix = jnp.min(jnp.where(lg == mx, lane, LANES), axis=1, keepdims=True)
        vals.append(mx)
        idxs.append(ix)
        lg = jnp.where(lane == ix, -jnp.inf, lg)
    exps = [jnp.exp(v - vals[0]) for v in vals]
    denom = exps[0]
    for e in exps[1:]:
        denom = denom + e

    onehot = jnp.zeros((t, LANES), F32)
    for ix in idxs:
        onehot = onehot + jnp.where(lane == ix, 1.0, 0.0)
    strict = (lax.broadcasted_iota(I32, (t, t), 1) < lax.broadcasted_iota(I32, (t, t), 0)).astype(BF16)
    before = jnp.dot(strict, onehot.astype(BF16), preferred_element_type=F32) + carry_ref[...]

    eidx = jnp.zeros((t, LANES), I32)
    gate = jnp.zeros((t, LANES), F32)
    rank = jnp.zeros((t, LANES), I32)
    for k in range(TOP_K):
        rk = jnp.sum(jnp.where(lane == idxs[k], before, 0.0), axis=1, keepdims=True)
        eidx = jnp.where(lane == k, idxs[k], eidx)
        gate = jnp.where(lane == k, exps[k] / denom, gate)
        rank = jnp.where(lane == k, rk.astype(I32), rank)
    eidx_ref[...] = eidx
    gate_ref[...] = gate
    rank_ref[...] = rank
    carry_ref[...] = carry_ref[...] + jnp.sum(onehot, axis=0, keepdims=True)
    cnt_ref[...] = carry_ref[...]


def _mix_out(ya, yb, w_out, x2d, ln_g, ln_b, router_w, router_b):
    n, d = x2d.shape
    half = ya.shape[1]
    wa = w_out[:half].astype(BF16)
    wb = w_out[half:].astype(BF16)
    rw = jnp.zeros((d, LANES), BF16).at[:, :N_EXPERTS].set(router_w.astype(BF16))
    rb = jnp.full((1, LANES), -jnp.inf, F32).at[0, :N_EXPERTS].set(router_b)
    t = ROW_TILE
    full = lambda a: pl.BlockSpec(a.shape, lambda i: (0, 0))
    rows = lambda w: pl.BlockSpec((t, w), lambda i: (i, 0))
    g2, b2 = ln_g.reshape(1, d), ln_b.reshape(1, d)
    return pl.pallas_call(
        _mix_out_kernel,
        grid=(n // t,),
        in_specs=[rows(half), rows(half), full(wa), full(wb), rows(d), full(g2), full(b2), full(rw), full(rb)],
        out_specs=[rows(d), pl.BlockSpec((t * ROW_SUBLANES, LANES), lambda i: (i, 0)),
                   rows(LANES), rows(LANES), rows(LANES), pl.BlockSpec((1, LANES), lambda i: (0, 0))],
        out_shape=[jax.ShapeDtypeStruct((n, d), F32), jax.ShapeDtypeStruct((n * ROW_SUBLANES, LANES), F32),
                   jax.ShapeDtypeStruct((n, LANES), I32),
                   jax.ShapeDtypeStruct((n, LANES), F32), jax.ShapeDtypeStruct((n, LANES), I32),
                   jax.ShapeDtypeStruct((1, LANES), F32)],
        scratch_shapes=[pltpu.VMEM((1, LANES), F32)],
        compiler_params=_cparams("arbitrary"),
        name="mix_out_router",
    )(ya, yb, wa, wb, x2d, g2, b2, rw, rb)


ROW_SUBLANES = 8


def _to_token_tiles(o_ref, x):
    for j in range(ROW_SUBLANES):
        o_ref[pl.ds(j, x.shape[0], stride=ROW_SUBLANES), :] = x[:, j * LANES:(j + 1) * LANES]


def _from_token_tiles(buf_ref, n_rows):
    return [buf_ref[pl.ds(j, n_rows, stride=ROW_SUBLANES), :] for j in range(ROW_SUBLANES)]


def _row_copy(src_hbm, dst_ref, src_row, dst_row, sem):
    src = src_hbm.at[pl.ds(pl.multiple_of(src_row * ROW_SUBLANES, ROW_SUBLANES), ROW_SUBLANES)]
    dst = dst_ref.at[pl.ds(pl.multiple_of(dst_row * ROW_SUBLANES, ROW_SUBLANES), ROW_SUBLANES)]
    return pltpu.make_async_copy(src, dst, sem)


def _rows_wait(src_hbm, dst_ref, sem):
    pltpu.make_async_copy(src_hbm.at[pl.ds(0, dst_ref.shape[0])], dst_ref, sem).wait()


def _expert_kernel(blk_e_ref, n_used_ref, tok_ref, tok_next_ref, x_hbm, wgu_ref, bgu_ref, wdn_ref, bdn_ref,
                   o_ref, xbuf, wgu_bf, wdn_bf, sem):
    i = pl.program_id(0)
    n_used = n_used_ref[0]
    slot = i % 2
    e = blk_e_ref[i]
    e_prev = blk_e_ref[jnp.maximum(i - 1, 0)]

    def gather(rows_ref, s):
        def body(r, _):
            _row_copy(x_hbm, xbuf.at[s], rows_ref[r], r, sem.at[s]).start()
            return 0
        lax.fori_loop(0, MOE_BM, body, 0, unroll=8)

    @pl.when(i == 0)
    def _():
        gather(tok_ref, 0)

    @pl.when((i == 0) | (e != e_prev))
    def _():
        wgu_bf[...] = wgu_ref[...].astype(BF16)
        wdn_bf[...] = wdn_ref[...].astype(BF16)

    gather(tok_next_ref, 1 - slot)
    _rows_wait(x_hbm, xbuf.at[slot], sem.at[slot])

    @pl.when(i < n_used)
    def _():
        x = jnp.concatenate(_from_token_tiles(xbuf.at[slot], MOE_BM), axis=1).astype(BF16)
        h = jnp.dot(x, wgu_bf[...], preferred_element_type=F32) + bgu_ref[...]
        gate = jnp.minimum(h[:, :D_FF], SWIGLU_LIMIT)
        up = jnp.clip(h[:, D_FF:], -SWIGLU_LIMIT, SWIGLU_LIMIT)
        glu = gate * (1.0 / (1.0 + jnp.exp(-SWIGLU_ALPHA * gate)))
        act = ((up + 1.0) * glu).astype(BF16)
        _to_token_tiles(o_ref, jnp.dot(act, wdn_bf[...], preferred_element_type=F32) + bdn_ref[...])

    @pl.when(i >= n_used)
    def _():
        o_ref[...] = jnp.zeros_like(o_ref)

    @pl.when(i == pl.num_programs(0) - 1)
    def _():
        _rows_wait(x_hbm, xbuf.at[1 - slot], sem.at[1 - slot])


def _expert_ffn(x_tiles, row_tok, blk_e, n_used, w_gu, b_gu, w_dn, b_dn):
    n_rows = row_tok.shape[0]
    d = w_gu.shape[1]
    n_blocks = n_rows // MOE_BM
    ne = w_gu.shape[0]
    grid_spec = pltpu.PrefetchScalarGridSpec(
        num_scalar_prefetch=2,
        grid=(n_blocks,),
        in_specs=[
            pl.BlockSpec((MOE_BM,), lambda i, be, nu: (i,), memory_space=pltpu.SMEM),
            pl.BlockSpec((MOE_BM,), lambda i, be, nu: (jnp.minimum(i + 1, n_blocks - 1),),
                         memory_space=pltpu.SMEM),
            pl.BlockSpec(memory_space=pl.ANY),
            pl.BlockSpec((None, d, 2 * D_FF), lambda i, be, nu: (be[i], 0, 0)),
            pl.BlockSpec((None, 1, 2 * D_FF), lambda i, be, nu: (be[i], 0, 0)),
            pl.BlockSpec((None, D_FF, d), lambda i, be, nu: (be[i], 0, 0)),
            pl.BlockSpec((None, 1, d), lambda i, be, nu: (be[i], 0, 0)),
        ],
        out_specs=pl.BlockSpec((MOE_BM * ROW_SUBLANES, LANES), lambda i, be, nu: (i, 0)),
        scratch_shapes=[pltpu.VMEM((2, MOE_BM * ROW_SUBLANES, LANES), F32), pltpu.VMEM((d, 2 * D_FF), BF16),
                        pltpu.VMEM((D_FF, d), BF16), pltpu.SemaphoreType.DMA((2,))],
    )
    return pl.pallas_call(
        _expert_kernel,
        grid_spec=grid_spec,
        out_shape=jax.ShapeDtypeStruct((n_rows * ROW_SUBLANES, LANES), F32),
        compiler_params=_cparams("arbitrary"),
        name="moe_experts",
    )(blk_e, n_used, row_tok, row_tok, x_tiles, w_gu, b_gu.reshape(ne, 1, 2 * D_FF), w_dn, b_dn.reshape(ne, 1, d))


def _combine_kernel(dest_ref, dest_next_ref, y_hbm, gate_ref, x_ref, g_ref, b_ref, o_ref, buf_ref, sem):
    i = pl.program_id(0)
    t = x_ref.shape[0]
    slot = i % 2

    def gather(rows_ref, s):
        def body(r, _):
            for k in range(TOP_K):
                _row_copy(y_hbm, buf_ref.at[s, k], rows_ref[r * TOP_K + k], r, sem.at[s]).start()
            return 0
        lax.fori_loop(0, t, body, 0, unroll=2)

    @pl.when(i == 0)
    def _():
        gather(dest_ref, 0)

    @pl.when(i + 1 < pl.num_programs(0))
    def _():
        gather(dest_next_ref, 1 - slot)

    for k in range(TOP_K):
        _rows_wait(y_hbm, buf_ref.at[slot, k], sem.at[slot])
    gate = gate_ref[...]
    f_tiles = None
    for k in range(TOP_K):
        y_tiles = [gate[:, k:k + 1] * yt for yt in _from_token_tiles(buf_ref.at[slot, k], t)]
        f_tiles = y_tiles if f_tiles is None else [a + b for a, b in zip(f_tiles, y_tiles)]
    f = jnp.concatenate(f_tiles, axis=1)
    o_ref[...] = _layer_norm(DEEPNORM_ALPHA * x_ref[...] + f, g_ref[...], b_ref[...])


def _combine(ys, dest_flat, gates, x2d, ln_g, ln_b):
    n, d = x2d.shape
    t = GATHER_T
    g2, b2 = ln_g.reshape(1, d), ln_b.reshape(1, d)
    full = lambda a: pl.BlockSpec(a.shape, lambda i: (0, 0))
    n_steps = n // t
    return pl.pallas_call(
        _combine_kernel,
        grid=(n_steps,),
        in_specs=[pl.BlockSpec((t * TOP_K,), lambda i: (i,), memory_space=pltpu.SMEM),
                  pl.BlockSpec((t * TOP_K,), lambda i: (jnp.minimum(i + 1, n_steps - 1),),
                               memory_space=pltpu.SMEM),
                  pl.BlockSpec(memory_space=pl.ANY),
                  pl.BlockSpec((t, LANES), lambda i: (i, 0)),
                  pl.BlockSpec((t, d), lambda i: (i, 0)),
                  full(g2), full(b2)],
        out_specs=pl.BlockSpec((t, d), lambda i: (i, 0)),
        out_shape=jax.ShapeDtypeStruct((n, d), F32),
        scratch_shapes=[pltpu.VMEM((2, TOP_K, t * ROW_SUBLANES, LANES), F32), pltpu.SemaphoreType.DMA((2,))],
        compiler_params=_cparams("arbitrary"),
        name="moe_combine",
    )(dest_flat, dest_flat, ys, gates, x2d, g2, b2)


def _moe_block(x1, x1_tiles, eidx, gates, rank, counts, w_gu, b_gu, w_dn, b_dn, ln_g, ln_b):
    n = x1.shape[0]
    e_sel = eidx[:, :TOP_K]
    cnt = counts[0, :N_EXPERTS].astype(I32)
    padded = ((cnt + MOE_BM - 1) // MOE_BM) * MOE_BM
    pad_end = jnp.cumsum(padded)
    pad_start = pad_end - padded
    dest = pad_start[e_sel] + rank[:, :TOP_K]
    n_rows = ((n * TOP_K + N_EXPERTS * (MOE_BM - 1) + MOE_BM - 1) // MOE_BM) * MOE_BM
    n_blocks = n_rows // MOE_BM
    dest_flat = dest.reshape(-1)
    tok = jnp.arange(n * TOP_K, dtype=I32) // TOP_K
    row_tok = jnp.zeros((n_rows,), I32).at[dest_flat].set(tok)
    blk_start = jnp.arange(n_blocks, dtype=I32) * MOE_BM
    blk_e = jnp.minimum(jnp.sum((pad_end[None, :] <= blk_start[:, None]).astype(I32), axis=1), N_EXPERTS - 1)
    n_used = (pad_end[-1:] // MOE_BM).astype(I32)
    ys = _expert_ffn(x1_tiles, row_tok, blk_e, n_used, w_gu, b_gu, w_dn, b_dn)
    return _combine(ys, dest_flat, gates, x1, ln_g, ln_b)


def _pad_cols(w, width):
    return jnp.pad(w, ((0, 0), (0, width - w.shape[1])))


def kernel(x, w_in_0, conv_w_0, w_out_0, ln_mix_g_0, ln_mix_b_0, router_w_0, router_b_0, w_gu_0, b_gu_0, w_dn_0, b_dn_0, ln_ffn_g_0, ln_ffn_b_0, w_in_1, lam_q1_1, lam_k1_1, lam_q2_1, lam_k2_1, subln_g_1, w_out_1, ln_mix_g_1, ln_mix_b_1, router_w_1, router_b_1, w_gu_1, b_gu_1, w_dn_1, b_dn_1, ln_ffn_g_1, ln_ffn_b_1):
    batch, seq, d = x.shape
    x0 = x.reshape(batch * seq, d)

    n_f32 = 3 * CONV_W
    n_attn = 3 * N_HEADS_SPARSE * D_HEAD + N_IDX_HEADS * D_IDX + D_IDX
    w_a = _pad_cols(jnp.concatenate([w_in_0[:, :n_f32], w_in_0[:, n_f32 + n_attn:]], axis=1),
                    n_f32 + LANES)
    w_b = _pad_cols(w_in_0[:, n_f32:n_f32 + n_attn], 3 * N_HEADS_SPARSE * D_HEAD + N_IDX_HEADS * D_IDX + LANES)
    ha, hb = _project(x0, [w_a.astype(BF16), w_b.astype(BF16)], [F32, BF16])
    ya = _short_conv(ha, conv_w_0, batch, seq)
    yb = _dsa_attention(ha, hb, batch, seq)
    x1, x1t, eidx, gates, rank, counts = _mix_out(ya, yb, w_out_0, x0, ln_mix_g_0, ln_mix_b_0, router_w_0, router_b_0)
    x2 = _moe_block(x1, x1t, eidx, gates, rank, counts, w_gu_0, b_gu_0, w_dn_0, b_dn_0, ln_ffn_g_0, ln_ffn_b_0)

    (hc,) = _project(x2, [w_in_1.astype(BF16)], [BF16])
    lam_vecs = jnp.stack([lam_q1_1, lam_k1_1, lam_q2_1, lam_k2_1]).astype(F32)
    yc = _diff_attention(hc, lam_vecs, subln_g_1.reshape(1, -1).astype(F32), batch, seq, 1)
    yd = _dilated_attention(hc, batch, seq)
    x3, x3t, eidx, gates, rank, counts = _mix_out(yc, yd, w_out_1, x2, ln_mix_g_1, ln_mix_b_1, router_w_1, router_b_1)
    x4 = _moe_block(x3, x3t, eidx, gates, rank, counts, w_gu_1, b_gu_1, w_dn_1, b_dn_1, ln_ffn_g_1, ln_ffn_b_1)
    return x4.reshape(batch, seq, d)
```

```python
import functools
import math

import jax
import jax.numpy as jnp
from jax import lax
from jax.experimental import pallas as pl
from jax.experimental.pallas import tpu as pltpu

F32 = jnp.float32
BF16 = jnp.bfloat16
I32 = jnp.int32

CONV_W = 512
CONV_TAPS = 3
N_HEADS_SPARSE = 8
D_HEAD = 64
N_IDX_HEADS = 8
D_IDX = 32
TOPK_LIMIT = 256
N_HEADS_DIFF = 4
N_HEADS_DIL = 8
DIL_GROUPS = ((128, 1), (512, 4), (2048, 16))
N_EXPERTS = 32
TOP_K = 4
D_FF = 1024
SWIGLU_LIMIT = 7.0
SWIGLU_ALPHA = 1.702
DEPTH = 2
DEEPNORM_ALPHA = (2 * DEPTH) ** 0.25
LN_EPS = 1e-5
RMS_EPS = 1e-5

LANES = 128
VMEM_LIMIT = 56 * 1024 * 1024
TQ = 128
KC = 512
ROW_TILE = 256
MOE_BM = 256
GATHER_T = 128
ISSUE_UNROLL = 8
COUNT_ROWS = 64
LOG2_E = math.log2(math.e)
INT_MIN = -2 ** 31


def _alibi_slopes(n):
    return [2.0 ** (-8.0 * (h + 1) / n) for h in range(n)]


def _cparams(*sem):
    return pltpu.CompilerParams(dimension_semantics=sem, vmem_limit_bytes=VMEM_LIMIT)


def _layer_norm(z, g, b):
    mu = jnp.mean(z, axis=-1, keepdims=True)
    zc = z - mu
    var = jnp.mean(zc * zc, axis=-1, keepdims=True)
    return zc * lax.rsqrt(var + LN_EPS) * g + b


def _proj_kernel(*refs, n_out):
    x_ref = refs[0]
    w_refs = refs[1:1 + n_out]
    o_refs = refs[1 + n_out:]
    xb = x_ref[...].astype(BF16)
    for w_ref, o_ref in zip(w_refs, o_refs):
        o_ref[...] = jnp.dot(xb, w_ref[...], preferred_element_type=F32).astype(o_ref.dtype)


def _project(x2d, weights, out_dtypes):
    n, d = x2d.shape
    n_out = len(weights)
    in_specs = [pl.BlockSpec((ROW_TILE, d), lambda i: (i, 0))]
    in_specs += [pl.BlockSpec(w.shape, lambda i: (0, 0)) for w in weights]
    out_specs = [pl.BlockSpec((ROW_TILE, w.shape[1]), lambda i: (i, 0)) for w in weights]
    out_shape = [jax.ShapeDtypeStruct((n, w.shape[1]), dt) for w, dt in zip(weights, out_dtypes)]
    return pl.pallas_call(
        functools.partial(_proj_kernel, n_out=n_out),
        grid=(n // ROW_TILE,),
        in_specs=in_specs, out_specs=out_specs, out_shape=out_shape,
        compiler_params=_cparams("parallel"),
        name="in_proj",
    )(x2d, *weights)


def _conv_kernel(gb_ref, gc_ref, xa_ref, w_ref, o_ref, prev_ref):
    j = pl.program_id(1)
    t = gb_ref.shape[0]

    @pl.when(j == 0)
    def _():
        prev_ref[...] = jnp.zeros_like(prev_ref)

    z = gc_ref[...] * xa_ref[...]
    row = lax.broadcasted_iota(I32, z.shape, 0)
    prev2 = prev_ref[0:1, :]
    prev1 = prev_ref[1:2, :]
    z1 = jnp.where(row == 0, prev1, pltpu.roll(z, 1, 0))
    z2 = jnp.where(row == 0, prev2, jnp.where(row == 1, prev1, pltpu.roll(z, 2, 0)))
    w = w_ref[...]
    y = w[0:1, :] * z + w[1:2, :] * z1 + w[2:3, :] * z2
    o_ref[...] = (gb_ref[...] * y).astype(o_ref.dtype)
    prev_ref[0:1, :] = z[t - 2:t - 1, :]
    prev_ref[1:2, :] = z[t - 1:t, :]


def _short_conv(ha, conv_w, batch, seq):
    n = ha.shape[0]
    t = min(512, seq)
    nj = seq // t
    spec = lambda c: pl.BlockSpec((t, CONV_W), lambda b, j, c=c: (b * nj + j, c))
    return pl.pallas_call(
        _conv_kernel,
        grid=(batch, nj),
        in_specs=[spec(0), spec(1), spec(2), pl.BlockSpec((CONV_TAPS, CONV_W), lambda b, j: (0, 0))],
        out_specs=pl.BlockSpec((t, CONV_W), lambda b, j: (b * nj + j, 0)),
        out_shape=jax.ShapeDtypeStruct((n, CONV_W), BF16),
        scratch_shapes=[pltpu.VMEM((8, CONV_W), F32)],
        compiler_params=_cparams("arbitrary", "arbitrary"),
        name="short_conv",
    )(ha, ha, ha, conv_w)


def _lane_tiles(x):
    return [x[:, u * LANES:(u + 1) * LANES] for u in range(x.shape[1] // LANES)]


def _flash_chains(chains, k_ref, v_ref, dmat_ref, amat_ref, c_lo, c_hi, scratch):
    qm_ref, s_ref, mx_ref, lp_ref, l_ref, acc_ref = scratch
    mx_ref[...] = jnp.full(mx_ref.shape, -jnp.inf, F32)

    def pass_a(c, _):
        k0 = pl.multiple_of(c * KC, KC)
        for n, (k_tile, _, slope) in enumerate(chains):
            kc = k_ref[pl.ds(k0, KC), k_tile * LANES:(k_tile + 1) * LANES]
            s = lax.dot_general(qm_ref[n], kc, (((1,), (1,)), ((), ())), preferred_element_type=F32)
            s = s * LOG2_E - (slope * LOG2_E) * dmat_ref[:, pl.ds(k0, KC)]
            if amat_ref is not None:
                s = s + amat_ref[:, pl.ds(k0, KC)]
            s_ref[n, :, pl.ds(k0, KC)] = s
            part = mx_ref[n]
            for t in _lane_tiles(s):
                part = jnp.maximum(part, t)
            mx_ref[n] = part
        return 0

    lax.fori_loop(c_lo, c_hi, pass_a, 0)

    for n in range(len(chains)):
        row_max = jnp.max(mx_ref[n], axis=1, keepdims=True)
        mx_ref[n] = jnp.broadcast_to(row_max, (TQ, LANES))
    lp_ref[...] = jnp.zeros(lp_ref.shape, F32)
    acc_ref[...] = jnp.zeros(acc_ref.shape, F32)

    def pass_b(c, _):
        k0 = pl.multiple_of(c * KC, KC)
        for n, (_, v_tile, _) in enumerate(chains):
            row_max = mx_ref[n]
            p_tiles = [jnp.exp2(t - row_max) for t in _lane_tiles(s_ref[n, :, pl.ds(k0, KC)])]
            part = lp_ref[n]
            for t in p_tiles:
                part = part + t
            lp_ref[n] = part
            p = jnp.concatenate(p_tiles, axis=1).astype(BF16)
            vc = v_ref[pl.ds(k0, KC), v_tile * LANES:(v_tile + 1) * LANES]
            acc_ref[n] = acc_ref[n] + jnp.dot(p, vc, preferred_element_type=F32)
        return 0

    lax.fori_loop(c_lo, c_hi, pass_b, 0)

    for n in range(len(chains)):
        l_ref[n] = jnp.sum(lp_ref[n], axis=1, keepdims=True)


def _split_head_pairs(q_ref, qm_ref, n_tiles, scale):
    low = lax.broadcasted_iota(I32, (TQ, LANES), 1) < D_HEAD
    for j in range(n_tiles):
        qt = q_ref[:, j * LANES:(j + 1) * LANES] * scale
        qm_ref[2 * j] = jnp.where(low, qt, 0).astype(BF16)
        qm_ref[2 * j + 1] = jnp.where(low, 0, qt).astype(BF16)


def _merge_head_pairs(o_ref, l_ref, acc_ref, n_tiles):
    low = lax.broadcasted_iota(I32, (TQ, LANES), 1) < D_HEAD
    for j in range(n_tiles):
        out = jnp.where(low, acc_ref[2 * j] / l_ref[2 * j], acc_ref[2 * j + 1] / l_ref[2 * j + 1])
        o_ref[:, j * LANES:(j + 1) * LANES] = out.astype(o_ref.dtype)


def _flash_scratch(n_chains, seq):
    return [pltpu.VMEM((n_chains, TQ, LANES), BF16), pltpu.VMEM((n_chains, TQ, seq), F32),
            pltpu.VMEM((n_chains, TQ, LANES), F32), pltpu.VMEM((n_chains, TQ, LANES), F32),
            pltpu.VMEM((n_chains, TQ, 1), F32), pltpu.VMEM((n_chains, TQ, LANES), F32)]


def _dsa_kernel(q_ref, k_ref, v_ref, iq_ref, ik_ref, iw_ref, o_ref, sc_ref, sct_ref, dmat_ref, *scratch, k_sel):
    i = pl.program_id(1)
    seq_len = sc_ref.shape[1]
    t0 = i * TQ
    n_chunks = (t0 + TQ + KC - 1) // KC
    idx_scale = (D_IDX ** -0.5) * (N_IDX_HEADS ** -0.5)
    row = t0 + lax.broadcasted_iota(I32, (TQ, KC), 0)
    col_in_chunk = lax.broadcasted_iota(I32, (TQ, KC), 1)

    iq = iq_ref[...]
    iw = iw_ref[:, 0:N_IDX_HEADS]

    def score_body(c, _):
        k0 = pl.multiple_of(c * KC, KC)
        ik = ik_ref[pl.ds(k0, KC), 0:D_IDX]
        sc = jnp.zeros((TQ, KC), F32)
        for h in range(N_IDX_HEADS):
            rel = lax.dot_general(iq[:, h * D_IDX:(h + 1) * D_IDX], ik,
                                  (((1,), (1,)), ((), ())), preferred_element_type=F32)
            sc = sc + iw[:, h:h + 1] * jnp.maximum(rel, 0.0)
        causal = (k0 + col_in_chunk) <= row
        sc = jnp.where(causal, sc * idx_scale, -jnp.inf)
        sc_ref[:, pl.ds(k0, KC)] = sc
        sct_ref[pl.ds(k0, KC), :] = sc.T
        return 0

    lax.fori_loop(0, n_chunks, score_body, 0)

    def code_to_float(code):
        return lax.bitcast_convert_type(jnp.where(code < 0, code ^ 0x7FFFFFFF, code), F32)

    def count(pred):
        def body(c, acc):
            k0 = pl.multiple_of(c * KC, KC)
            hit = jnp.where(pred(sct_ref[pl.ds(k0, KC), :]), 1.0, 0.0)
            return acc + jnp.sum(hit.reshape(KC // COUNT_ROWS, COUNT_ROWS, TQ), axis=0)
        acc = lax.fori_loop(0, n_chunks, body, jnp.zeros((COUNT_ROWS, TQ), F32))
        return jnp.sum(acc, axis=0, keepdims=True)

    kf = float(k_sel)
    has_k = count(lambda t: t > -jnp.inf) >= kf
    code = jnp.where(count(lambda t: t >= 0.0) >= kf, 0, INT_MIN).astype(I32)

    def bit_body(b, code):
        cand = code + lax.shift_left(jnp.int32(1), 30 - b)
        cand_f = code_to_float(cand)
        return jnp.where(count(lambda t: t >= cand_f) >= kf, cand, code)

    code = lax.fori_loop(0, 31, bit_body, code)
    thr_q = jnp.where(has_k, code_to_float(code), -jnp.inf)
    need_q = jnp.where(has_k, kf - count(lambda t: t > thr_q), float(seq_len))
    thr = jnp.broadcast_to(thr_q, (LANES, TQ)).T
    need = jnp.broadcast_to(need_q, (LANES, TQ)).T

    tri = (lax.broadcasted_iota(I32, (KC, KC), 0) <= lax.broadcasted_iota(I32, (KC, KC), 1)).astype(BF16)
    thr_w = jnp.concatenate([thr] * (KC // LANES), axis=1)
    need_w = jnp.concatenate([need] * (KC // LANES), axis=1)

    def mask_body(c, ties_before):
        k0 = pl.multiple_of(c * KC, KC)
        sc = sc_ref[:, pl.ds(k0, KC)]
        eq = sc == thr_w
        eqf = jnp.where(eq, 1.0, 0.0)
        rank = ties_before + jnp.dot(eqf.astype(BF16), tri, preferred_element_type=F32)
        dist = row - (k0 + col_in_chunk)
        sel = ((sc > thr_w) | (eq & (rank <= need_w))) & (dist >= 0)
        dmat_ref[:, pl.ds(k0, KC)] = jnp.where(sel, dist.astype(F32), jnp.inf)
        return ties_before + jnp.sum(eqf, axis=1, keepdims=True)

    lax.fori_loop(0, n_chunks, mask_body, jnp.zeros((TQ, 1), F32))

    slopes = _alibi_slopes(N_HEADS_SPARSE)
    n_tiles = N_HEADS_SPARSE * D_HEAD // LANES
    _split_head_pairs(q_ref, scratch[0], n_tiles, D_HEAD ** -0.5)
    chains = [(h // 2, h // 2, slopes[h]) for h in range(N_HEADS_SPARSE)]
    _flash_chains(chains, k_ref, v_ref, dmat_ref, None, 0, n_chunks, scratch)
    _merge_head_pairs(o_ref, scratch[-2], scratch[-1], n_tiles)


def _dsa_attention(ha, hb, batch, seq):
    n = hb.shape[0]
    nq = seq // TQ
    k_sel = min(TOPK_LIMIT, seq // 4)
    width = N_HEADS_SPARSE * D_HEAD
    iq_w = N_IDX_HEADS * D_IDX
    return pl.pallas_call(
        functools.partial(_dsa_kernel, k_sel=k_sel),
        grid=(batch, nq),
        in_specs=[
            pl.BlockSpec((TQ, width), lambda b, i: (b * nq + i, 0)),
            pl.BlockSpec((seq, width), lambda b, i: (b, 1)),
            pl.BlockSpec((seq, width), lambda b, i: (b, 2)),
            pl.BlockSpec((TQ, iq_w), lambda b, i: (b * nq + i, 3 * width // iq_w)),
            pl.BlockSpec((seq, LANES), lambda b, i: (b, (3 * width + iq_w) // LANES)),
            pl.BlockSpec((TQ, LANES), lambda b, i: (b * nq + i, 3 * CONV_W // LANES)),
        ],
        out_specs=pl.BlockSpec((TQ, width), lambda b, i: (b * nq + i, 0)),
        out_shape=jax.ShapeDtypeStruct((n, width), BF16),
        scratch_shapes=[pltpu.VMEM((TQ, seq), F32), pltpu.VMEM((seq, TQ), F32), pltpu.VMEM((TQ, seq), F32)]
        + _flash_scratch(N_HEADS_SPARSE, seq),
        compiler_params=_cparams("arbitrary", "arbitrary"),
        name="dsa_attention",
    )(hb, hb, hb, hb, hb, ha)


def _diff_kernel(q_ref, k_ref, v_ref, lam_ref, g_ref, o_ref, dmat_ref, *scratch, lam_init):
    i = pl.program_id(1)
    t0 = i * TQ
    n_chunks = (t0 + TQ + KC - 1) // KC
    row = t0 + lax.broadcasted_iota(I32, (TQ, KC), 0)
    col_in_chunk = lax.broadcasted_iota(I32, (TQ, KC), 1)

    def mask_body(c, _):
        k0 = pl.multiple_of(c * KC, KC)
        dist = row - (k0 + col_in_chunk)
        dmat_ref[:, pl.ds(k0, KC)] = jnp.where(dist >= 0, dist.astype(F32), jnp.inf)
        return 0

    lax.fori_loop(0, n_chunks, mask_body, 0)

    lv = lam_ref[...]
    lam = (jnp.exp(jnp.sum(lv[0:1, :] * lv[1:2, :], axis=1, keepdims=True))
           - jnp.exp(jnp.sum(lv[2:3, :] * lv[3:4, :], axis=1, keepdims=True)) + lam_init)

    slopes = _alibi_slopes(N_HEADS_DIFF)
    _split_head_pairs(q_ref, scratch[0], N_HEADS_DIFF, D_HEAD ** -0.5)
    chains = [(n // 2, n // 2, slopes[n // 2]) for n in range(2 * N_HEADS_DIFF)]
    _flash_chains(chains, k_ref, v_ref, dmat_ref, None, 0, n_chunks, scratch)
    l_ref, acc_ref = scratch[-2], scratch[-1]
    g = g_ref[...]
    for h in range(N_HEADS_DIFF):
        of = acc_ref[2 * h] / l_ref[2 * h] - lam * (acc_ref[2 * h + 1] / l_ref[2 * h + 1])
        of = of * lax.rsqrt(jnp.mean(of * of, axis=1, keepdims=True) + RMS_EPS) * g
        o_ref[:, h * LANES:(h + 1) * LANES] = (of * (1.0 - lam_init)).astype(o_ref.dtype)


def _diff_attention(hb, lam_vecs, subln_g, batch, seq, layer):
    n = hb.shape[0]
    nq = seq // TQ
    width = N_HEADS_DIFF * 2 * D_HEAD
    lam_init = 0.8 - 0.6 * math.exp(-0.3 * layer)
    return pl.pallas_call(
        functools.partial(_diff_kernel, lam_init=lam_init),
        grid=(batch, nq),
        in_specs=[
            pl.BlockSpec((TQ, width), lambda b, i: (b * nq + i, 0)),
            pl.BlockSpec((seq, width), lambda b, i: (b, 1)),
            pl.BlockSpec((seq, width), lambda b, i: (b, 2)),
            pl.BlockSpec(lam_vecs.shape, lambda b, i: (0, 0)),
            pl.BlockSpec(subln_g.shape, lambda b, i: (0, 0)),
        ],
        out_specs=pl.BlockSpec((TQ, width), lambda b, i: (b * nq + i, 0)),
        out_shape=jax.ShapeDtypeStruct((n, width), BF16),
        scratch_shapes=[pltpu.VMEM((TQ, seq), F32)] + _flash_scratch(2 * N_HEADS_DIFF, seq),
        compiler_params=_cparams("arbitrary", "arbitrary"),
        name="diff_attention",
    )(hb, hb, hb, lam_vecs, subln_g)


def _dilated_kernel(q_ref, k_ref, v_ref, o_ref, dmat_ref, amat_ref, *scratch):
    i = pl.program_id(1)
    t0 = i * TQ
    w_max = max(w for w, _ in DIL_GROUPS)
    c_lo = jnp.maximum(t0 - w_max, 0) // KC
    c_hi = (t0 + TQ + KC - 1) // KC
    row = t0 + lax.broadcasted_iota(I32, (TQ, KC), 0)
    col_in_chunk = lax.broadcasted_iota(I32, (TQ, KC), 1)

    def mask_body(c, _):
        k0 = pl.multiple_of(c * KC, KC)
        dist = row - (k0 + col_in_chunk)
        mult = jnp.zeros((TQ, KC), F32)
        for w, d in DIL_GROUPS:
            member = (dist >= 0) & (dist <= w) & ((dist & (d - 1)) == 0)
            mult = mult + jnp.where(member, 1.0, 0.0)
        on = mult > 0.0
        dmat_ref[:, pl.ds(k0, KC)] = jnp.where(on, dist.astype(F32), jnp.inf)
        amat_ref[:, pl.ds(k0, KC)] = jnp.log2(jnp.where(on, mult, 1.0))
        return 0

    lax.fori_loop(c_lo, c_hi, mask_body, 0)

    slopes = _alibi_slopes(N_HEADS_DIL)
    n_tiles = N_HEADS_DIL * D_HEAD // LANES
    _split_head_pairs(q_ref, scratch[0], n_tiles, D_HEAD ** -0.5)
    chains = [(h // 2, h // 2, slopes[h]) for h in range(N_HEADS_DIL)]
    _flash_chains(chains, k_ref, v_ref, dmat_ref, amat_ref, c_lo, c_hi, scratch)
    _merge_head_pairs(o_ref, scratch[-2], scratch[-1], n_tiles)


def _dilated_attention(hb, batch, seq):
    n = hb.shape[0]
    nq = seq // TQ
    width = N_HEADS_DIL * D_HEAD
    return pl.pallas_call(
        _dilated_kernel,
        grid=(batch, nq),
        in_specs=[
            pl.BlockSpec((TQ, width), lambda b, i: (b * nq + i, 3)),
            pl.BlockSpec((seq, width), lambda b, i: (b, 4)),
            pl.BlockSpec((seq, width), lambda b, i: (b, 5)),
        ],
        out_specs=pl.BlockSpec((TQ, width), lambda b, i: (b * nq + i, 0)),
        out_shape=jax.ShapeDtypeStruct((n, width), BF16),
        scratch_shapes=[pltpu.VMEM((TQ, seq), F32), pltpu.VMEM((TQ, seq), F32)] + _flash_scratch(N_HEADS_DIL, seq),
        compiler_params=_cparams("arbitrary", "arbitrary"),
        name="dilated_attention",
    )(hb, hb, hb)


def _mix_out_kernel(ya_ref, yb_ref, wa_ref, wb_ref, x_ref, g_ref, b_ref, rw_ref, rb_ref,
                    x1_ref, x1t_ref, eidx_ref, gate_ref, rank_ref, cnt_ref, carry_ref):
    step = pl.program_id(0)

    @pl.when(step == 0)
    def _():
        carry_ref[...] = jnp.zeros_like(carry_ref)

    m = (jnp.dot(ya_ref[...], wa_ref[...], preferred_element_type=F32)
         + jnp.dot(yb_ref[...], wb_ref[...], preferred_element_type=F32))
    x1 = _layer_norm(DEEPNORM_ALPHA * x_ref[...] + m, g_ref[...], b_ref[...])
    x1_ref[...] = x1
    _to_token_tiles(x1t_ref, x1)

    t = x1.shape[0]
    logits = jnp.dot(x1.astype(BF16), rw_ref[...], preferred_element_type=F32) + rb_ref[...]
    lane = lax.broadcasted_iota(I32, (t, LANES), 1)
    vals, idxs = [], []
    lg = logits
    for _ in range(TOP_K):
        mx = jnp.max(lg, axis=1, keepdims=True)
        ix = jnp.min(jnp.where(lg == mx, lane, LANES), axis=1, keepdims=True)
        vals.append(mx)
        idxs.append(ix)
        lg = jnp.where(lane == ix, -jnp.inf, lg)
    exps = [jnp.exp(v - vals[0]) for v in vals]
    denom = exps[0]
    for e in exps[1:]:
        denom = denom + e

    onehot = jnp.zeros((t, LANES), F32)
    for ix in idxs:
        onehot = onehot + jnp.where(lane == ix, 1.0, 0.0)
    strict = (lax.broadcasted_iota(I32, (t, t), 1) < lax.broadcasted_iota(I32, (t, t), 0)).astype(BF16)
    before = jnp.dot(strict, onehot.astype(BF16), preferred_element_type=F32) + carry_ref[...]

    eidx = jnp.zeros((t, LANES), I32)
    gate = jnp.zeros((t, LANES), F32)
    rank = jnp.zeros((t, LANES), I32)
    for k in range(TOP_K):
        rk = jnp.sum(jnp.where(lane == idxs[k], before, 0.0), axis=1, keepdims=True)
        eidx = jnp.where(lane == k, idxs[k], eidx)
        gate = jnp.where(lane == k, exps[k] / denom, gate)
        rank = jnp.where(lane == k, rk.astype(I32), rank)
    eidx_ref[...] = eidx
    gate_ref[...] = gate
    rank_ref[...] = rank
    carry_ref[...] = carry_ref[...] + jnp.sum(onehot, axis=0, keepdims=True)
    cnt_ref[...] = carry_ref[...]


def _mix_out(ya, yb, w_out, x2d, ln_g, ln_b, router_w, router_b):
    n, d = x2d.shape
    half = ya.shape[1]
    wa = w_out[:half].astype(BF16)
    wb = w_out[half:].astype(BF16)
    rw = jnp.zeros((d, LANES), BF16).at[:, :N_EXPERTS].set(router_w.astype(BF16))
    rb = jnp.full((1, LANES), -jnp.inf, F32).at[0, :N_EXPERTS].set(router_b)
    t = ROW_TILE
    full = lambda a: pl.BlockSpec(a.shape, lambda i: (0, 0))
    rows = lambda w: pl.BlockSpec((t, w), lambda i: (i, 0))
    g2, b2 = ln_g.reshape(1, d), ln_b.reshape(1, d)
    return pl.pallas_call(
        _mix_out_kernel,
        grid=(n // t,),
        in_specs=[rows(half), rows(half), full(wa), full(wb), rows(d), full(g2), full(b2), full(rw), full(rb)],
        out_specs=[rows(d), pl.BlockSpec((t * ROW_SUBLANES, LANES), lambda i: (i, 0)),
                   rows(LANES), rows(LANES), rows(LANES), pl.BlockSpec((1, LANES), lambda i: (0, 0))],
        out_shape=[jax.ShapeDtypeStruct((n, d), F32), jax.ShapeDtypeStruct((n * ROW_SUBLANES, LANES), F32),
                   jax.ShapeDtypeStruct((n, LANES), I32),
                   jax.ShapeDtypeStruct((n, LANES), F32), jax.ShapeDtypeStruct((n, LANES), I32),
                   jax.ShapeDtypeStruct((1, LANES), F32)],
        scratch_shapes=[pltpu.VMEM((1, LANES), F32)],
        compiler_params=_cparams("arbitrary"),
        name="mix_out_router",
    )(ya, yb, wa, wb, x2d, g2, b2, rw, rb)


ROW_SUBLANES = 8


def _to_token_tiles(o_ref, x):
    for j in range(ROW_SUBLANES):
        o_ref[pl.ds(j, x.shape[0], stride=ROW_SUBLANES), :] = x[:, j * LANES:(j + 1) * LANES]


def _from_token_tiles(buf_ref, n_rows):
    return [buf_ref[pl.ds(j, n_rows, stride=ROW_SUBLANES), :] for j in range(ROW_SUBLANES)]


def _row_copy(src_hbm, dst_ref, src_row, dst_row, sem):
    src = src_hbm.at[pl.ds(pl.multiple_of(src_row * ROW_SUBLANES, ROW_SUBLANES), ROW_SUBLANES)]
    dst = dst_ref.at[pl.ds(pl.multiple_of(dst_row * ROW_SUBLANES, ROW_SUBLANES), ROW_SUBLANES)]
    return pltpu.make_async_copy(src, dst, sem)


def _rows_wait(src_hbm, dst_ref, sem):
    pltpu.make_async_copy(src_hbm.at[pl.ds(0, dst_ref.shape[0])], dst_ref, sem).wait()


def _expert_kernel(blk_e_ref, n_used_ref, tok_ref, tok_next_ref, x_hbm, wgu_ref, bgu_ref, wdn_ref, bdn_ref,
                   o_ref, xbuf, wgu_bf, wdn_bf, sem):
    i = pl.program_id(0)
    n_used = n_used_ref[0]
    slot = i % 2
    e = blk_e_ref[i]
    e_prev = blk_e_ref[jnp.maximum(i - 1, 0)]

    def gather(rows_ref, s):
        def body(g, _):
            for u in range(ISSUE_UNROLL):
                r = g * ISSUE_UNROLL + u
                _row_copy(x_hbm, xbuf.at[s], rows_ref[r], r, sem.at[s]).start(priority=u % 2)
            return 0
        lax.fori_loop(0, MOE_BM // ISSUE_UNROLL, body, 0)

    @pl.when(i == 0)
    def _():
        gather(tok_ref, 0)

    @pl.when((i == 0) | (e != e_prev))
    def _():
        wgu_bf[...] = wgu_ref[...].astype(BF16)
        wdn_bf[...] = wdn_ref[...].astype(BF16)

    gather(tok_next_ref, 1 - slot)
    _rows_wait(x_hbm, xbuf.at[slot], sem.at[slot])

    @pl.when(i < n_used)
    def _():
        x = jnp.concatenate(_from_token_tiles(xbuf.at[slot], MOE_BM), axis=1).astype(BF16)
        h = jnp.dot(x, wgu_bf[...], preferred_element_type=F32) + bgu_ref[...]
        gate = jnp.minimum(h[:, :D_FF], SWIGLU_LIMIT)
        up = jnp.clip(h[:, D_FF:], -SWIGLU_LIMIT, SWIGLU_LIMIT)
        glu = gate * (1.0 / (1.0 + jnp.exp(-SWIGLU_ALPHA * gate)))
        act = ((up + 1.0) * glu).astype(BF16)
        _to_token_tiles(o_ref, jnp.dot(act, wdn_bf[...], preferred_element_type=F32) + bdn_ref[...])

    @pl.when(i >= n_used)
    def _():
        o_ref[...] = jnp.zeros_like(o_ref)

    @pl.when(i == pl.num_programs(0) - 1)
    def _():
        _rows_wait(x_hbm, xbuf.at[1 - slot], sem.at[1 - slot])


def _expert_ffn(x_tiles, row_tok, blk_e, n_used, w_gu, b_gu, w_dn, b_dn):
    n_rows = row_tok.shape[0]
    d = w_gu.shape[1]
    n_blocks = n_rows // MOE_BM
    ne = w_gu.shape[0]
    grid_spec = pltpu.PrefetchScalarGridSpec(
        num_scalar_prefetch=2,
        grid=(n_blocks,),
        in_specs=[
            pl.BlockSpec((MOE_BM,), lambda i, be, nu: (i,), memory_space=pltpu.SMEM),
            pl.BlockSpec((MOE_BM,), lambda i, be, nu: (jnp.minimum(i + 1, n_blocks - 1),),
                         memory_space=pltpu.SMEM),
            pl.BlockSpec(memory_space=pl.ANY),
            pl.BlockSpec((None, d, 2 * D_FF), lambda i, be, nu: (be[i], 0, 0)),
            pl.BlockSpec((None, 1, 2 * D_FF), lambda i, be, nu: (be[i], 0, 0)),
            pl.BlockSpec((None, D_FF, d), lambda i, be, nu: (be[i], 0, 0)),
            pl.BlockSpec((None, 1, d), lambda i, be, nu: (be[i], 0, 0)),
        ],
        out_specs=pl.BlockSpec((MOE_BM * ROW_SUBLANES, LANES), lambda i, be, nu: (i, 0)),
        scratch_shapes=[pltpu.VMEM((2, MOE_BM * ROW_SUBLANES, LANES), F32), pltpu.VMEM((d, 2 * D_FF), BF16),
                        pltpu.VMEM((D_FF, d), BF16), pltpu.SemaphoreType.DMA((2,))],
    )
    return pl.pallas_call(
        _expert_kernel,
        grid_spec=grid_spec,
        out_shape=jax.ShapeDtypeStruct((n_rows * ROW_SUBLANES, LANES), F32),
        compiler_params=_cparams("arbitrary"),
        name="moe_experts",
    )(blk_e, n_used, row_tok, row_tok, x_tiles, w_gu, b_gu.reshape(ne, 1, 2 * D_FF), w_dn, b_dn.reshape(ne, 1, d))


def _combine_kernel(dest_ref, dest_next_ref, y_hbm, gate_ref, x_ref, g_ref, b_ref, o_ref, buf_ref, sem):
    i = pl.program_id(0)
    t = x_ref.shape[0]
    slot = i % 2

    def gather(rows_ref, s):
        def body(g, _):
            for u in range(ISSUE_UNROLL // TOP_K):
                r = g * (ISSUE_UNROLL // TOP_K) + u
                for k in range(TOP_K):
                    _row_copy(y_hbm, buf_ref.at[s, k], rows_ref[r * TOP_K + k], r, sem.at[s]).start(priority=k % 2)
            return 0
        lax.fori_loop(0, t * TOP_K // ISSUE_UNROLL, body, 0)

    @pl.when(i == 0)
    def _():
        gather(dest_ref, 0)

    @pl.when(i + 1 < pl.num_programs(0))
    def _():
        gather(dest_next_ref, 1 - slot)

    for k in range(TOP_K):
        _rows_wait(y_hbm, buf_ref.at[slot, k], sem.at[slot])
    gate = gate_ref[...]
    f_tiles = None
    for k in range(TOP_K):
        y_tiles = [gate[:, k:k + 1] * yt for yt in _from_token_tiles(buf_ref.at[slot, k], t)]
        f_tiles = y_tiles if f_tiles is None else [a + b for a, b in zip(f_tiles, y_tiles)]
    f = jnp.concatenate(f_tiles, axis=1)
    o_ref[...] = _layer_norm(DEEPNORM_ALPHA * x_ref[...] + f, g_ref[...], b_ref[...])


def _combine(ys, dest_flat, gates, x2d, ln_g, ln_b):
    n, d = x2d.shape
    t = GATHER_T
    g2, b2 = ln_g.reshape(1, d), ln_b.reshape(1, d)
    full = lambda a: pl.BlockSpec(a.shape, lambda i: (0, 0))
    n_steps = n // t
    return pl.pallas_call(
        _combine_kernel,
        grid=(n_steps,),
        in_specs=[pl.BlockSpec((t * TOP_K,), lambda i: (i,), memory_space=pltpu.SMEM),
                  pl.BlockSpec((t * TOP_K,), lambda i: (jnp.minimum(i + 1, n_steps - 1),),
                               memory_space=pltpu.SMEM),
                  pl.BlockSpec(memory_space=pl.ANY),
                  pl.BlockSpec((t, LANES), lambda i: (i, 0)),
                  pl.BlockSpec((t, d), lambda i: (i, 0)),
                  full(g2), full(b2)],
        out_specs=pl.BlockSpec((t, d), lambda i: (i, 0)),
        out_shape=jax.ShapeDtypeStruct((n, d), F32),
        scratch_shapes=[pltpu.VMEM((2, TOP_K, t * ROW_SUBLANES, LANES), F32), pltpu.SemaphoreType.DMA((2,))],
        compiler_params=_cparams("arbitrary"),
        name="moe_combine",
    )(dest_flat, dest_flat, ys, gates, x2d, g2, b2)


def _row_token_kernel(dest_ref, pad_lo_ref, pad_hi_ref, o_ref):
    def clear_segment(e, _):
        def clear(r, _):
            o_ref[r] = 0
            return 0
        lax.fori_loop(pad_lo_ref[e], pad_hi_ref[e], clear, 0)
        return 0

    lax.fori_loop(0, pad_lo_ref.shape[0], clear_segment, 0)

    def put(g, _):
        base = g * ISSUE_UNROLL
        rows = [dest_ref[base + u] for u in range(ISSUE_UNROLL)]
        tok0 = g * (ISSUE_UNROLL // TOP_K)
        for u in range(ISSUE_UNROLL):
            o_ref[rows[u]] = tok0 + u // TOP_K
        return 0

    lax.fori_loop(0, dest_ref.shape[0] // ISSUE_UNROLL, put, 0)


def _row_tokens(dest_flat, pad_lo, pad_hi, n_rows):
    smem = pl.BlockSpec(memory_space=pltpu.SMEM)
    return pl.pallas_call(
        _row_token_kernel,
        in_specs=[smem, smem, smem],
        out_specs=smem,
        out_shape=jax.ShapeDtypeStruct((n_rows,), I32),
        name="moe_row_tokens",
    )(dest_flat, pad_lo, pad_hi)


def _moe_block(x1, x1_tiles, eidx, gates, rank, counts, w_gu, b_gu, w_dn, b_dn, ln_g, ln_b):
    n = x1.shape[0]
    e_sel = eidx[:, :TOP_K]
    cnt = counts[0, :N_EXPERTS].astype(I32)
    padded = ((cnt + MOE_BM - 1) // MOE_BM) * MOE_BM
    pad_end = jnp.cumsum(padded)
    pad_start = pad_end - padded
    dest = pad_start[e_sel] + rank[:, :TOP_K]
    n_rows = ((n * TOP_K + N_EXPERTS * (MOE_BM - 1) + MOE_BM - 1) // MOE_BM) * MOE_BM
    n_blocks = n_rows // MOE_BM
    dest_flat = dest.reshape(-1)
    pad_lo = jnp.concatenate([pad_start + cnt, pad_end[-1:]]).astype(I32)
    pad_hi = jnp.concatenate([pad_end, jnp.full((1,), n_rows)]).astype(I32)
    row_tok = _row_tokens(dest_flat, pad_lo, pad_hi, n_rows)
    blk_start = jnp.arange(n_blocks, dtype=I32) * MOE_BM
    blk_e = jnp.minimum(jnp.sum((pad_end[None, :] <= blk_start[:, None]).astype(I32), axis=1), N_EXPERTS - 1)
    n_used = (pad_end[-1:] // MOE_BM).astype(I32)
    ys = _expert_ffn(x1_tiles, row_tok, blk_e, n_used, w_gu, b_gu, w_dn, b_dn)
    return _combine(ys, dest_flat, gates, x1, ln_g, ln_b)


def _pad_cols(w, width):
    return jnp.pad(w, ((0, 0), (0, width - w.shape[1])))


def kernel(x, w_in_0, conv_w_0, w_out_0, ln_mix_g_0, ln_mix_b_0, router_w_0, router_b_0, w_gu_0, b_gu_0, w_dn_0, b_dn_0, ln_ffn_g_0, ln_ffn_b_0, w_in_1, lam_q1_1, lam_k1_1, lam_q2_1, lam_k2_1, subln_g_1, w_out_1, ln_mix_g_1, ln_mix_b_1, router_w_1, router_b_1, w_gu_1, b_gu_1, w_dn_1, b_dn_1, ln_ffn_g_1, ln_ffn_b_1):
    batch, seq, d = x.shape
    x0 = x.reshape(batch * seq, d)

    n_f32 = 3 * CONV_W
    n_attn = 3 * N_HEADS_SPARSE * D_HEAD + N_IDX_HEADS * D_IDX + D_IDX
    w_a = _pad_cols(jnp.concatenate([w_in_0[:, :n_f32], w_in_0[:, n_f32 + n_attn:]], axis=1),
                    n_f32 + LANES)
    w_b = _pad_cols(w_in_0[:, n_f32:n_f32 + n_attn], 3 * N_HEADS_SPARSE * D_HEAD + N_IDX_HEADS * D_IDX + LANES)
    ha, hb = _project(x0, [w_a.astype(BF16), w_b.astype(BF16)], [F32, BF16])
    ya = _short_conv(ha, conv_w_0, batch, seq)
    yb = _dsa_attention(ha, hb, batch, seq)
    x1, x1t, eidx, gates, rank, counts = _mix_out(ya, yb, w_out_0, x0, ln_mix_g_0, ln_mix_b_0, router_w_0, router_b_0)
    x2 = _moe_block(x1, x1t, eidx, gates, rank, counts, w_gu_0, b_gu_0, w_dn_0, b_dn_0, ln_ffn_g_0, ln_ffn_b_0)

    (hc,) = _project(x2, [w_in_1.astype(BF16)], [BF16])
    lam_vecs = jnp.stack([lam_q1_1, lam_k1_1, lam_q2_1, lam_k2_1]).astype(F32)
    yc = _diff_attention(hc, lam_vecs, subln_g_1.reshape(1, -1).astype(F32), batch, seq, 1)
    yd = _dilated_attention(hc, batch, seq)
    x3, x3t, eidx, gates, rank, counts = _mix_out(yc, yd, w_out_1, x2, ln_mix_g_1, ln_mix_b_1, router_w_1, router_b_1)
    x4 = _moe_block(x3, x3t, eidx, gates, rank, counts, w_gu_1, b_gu_1, w_dn_1, b_dn_1, ln_ffn_g_1, ln_ffn_b_1)
    return x4.reshape(batch, seq, d)
```

```python
import functools
import math

import jax
import jax.numpy as jnp
from jax import lax
from jax.experimental import pallas as pl
from jax.experimental.pallas import tpu as pltpu

F32 = jnp.float32
BF16 = jnp.bfloat16
I32 = jnp.int32

CONV_W = 512
CONV_TAPS = 3
N_HEADS_SPARSE = 8
D_HEAD = 64
N_IDX_HEADS = 8
D_IDX = 32
TOPK_LIMIT = 256
N_HEADS_DIFF = 4
N_HEADS_DIL = 8
DIL_GROUPS = ((128, 1), (512, 4), (2048, 16))
N_EXPERTS = 32
TOP_K = 4
D_FF = 1024
SWIGLU_LIMIT = 7.0
SWIGLU_ALPHA = 1.702
DEPTH = 2
DEEPNORM_ALPHA = (2 * DEPTH) ** 0.25
LN_EPS = 1e-5
RMS_EPS = 1e-5

LANES = 128
VMEM_LIMIT = 56 * 1024 * 1024
TQ = 256
CHAIN_GROUP = 4
KC = 512
ROW_TILE = 256
MOE_BM = 256
GATHER_T = 128
ISSUE_UNROLL = 8
COUNT_ROWS = 64
LOG2_E = math.log2(math.e)
INT_MIN = -2 ** 31


def _alibi_slopes(n):
    return [2.0 ** (-8.0 * (h + 1) / n) for h in range(n)]


def _cparams(*sem):
    return pltpu.CompilerParams(dimension_semantics=sem, vmem_limit_bytes=VMEM_LIMIT)


def _layer_norm(z, g, b):
    mu = jnp.mean(z, axis=-1, keepdims=True)
    zc = z - mu
    var = jnp.mean(zc * zc, axis=-1, keepdims=True)
    return zc * lax.rsqrt(var + LN_EPS) * g + b


def _proj_kernel(*refs, n_out):
    x_ref = refs[0]
    w_refs = refs[1:1 + n_out]
    o_refs = refs[1 + n_out:]
    xb = x_ref[...].astype(BF16)
    for w_ref, o_ref in zip(w_refs, o_refs):
        o_ref[...] = jnp.dot(xb, w_ref[...], preferred_element_type=F32).astype(o_ref.dtype)


def _project(x2d, weights, out_dtypes):
    n, d = x2d.shape
    n_out = len(weights)
    in_specs = [pl.BlockSpec((ROW_TILE, d), lambda i: (i, 0))]
    in_specs += [pl.BlockSpec(w.shape, lambda i: (0, 0)) for w in weights]
    out_specs = [pl.BlockSpec((ROW_TILE, w.shape[1]), lambda i: (i, 0)) for w in weights]
    out_shape = [jax.ShapeDtypeStruct((n, w.shape[1]), dt) for w, dt in zip(weights, out_dtypes)]
    return pl.pallas_call(
        functools.partial(_proj_kernel, n_out=n_out),
        grid=(n // ROW_TILE,),
        in_specs=in_specs, out_specs=out_specs, out_shape=out_shape,
        compiler_params=_cparams("parallel"),
        name="in_proj",
    )(x2d, *weights)


def _conv_kernel(gb_ref, gc_ref, xa_ref, w_ref, o_ref, prev_ref):
    j = pl.program_id(1)
    t = gb_ref.shape[0]

    @pl.when(j == 0)
    def _():
        prev_ref[...] = jnp.zeros_like(prev_ref)

    z = gc_ref[...] * xa_ref[...]
    row = lax.broadcasted_iota(I32, z.shape, 0)
    prev2 = prev_ref[0:1, :]
    prev1 = prev_ref[1:2, :]
    z1 = jnp.where(row == 0, prev1, pltpu.roll(z, 1, 0))
    z2 = jnp.where(row == 0, prev2, jnp.where(row == 1, prev1, pltpu.roll(z, 2, 0)))
    w = w_ref[...]
    y = w[0:1, :] * z + w[1:2, :] * z1 + w[2:3, :] * z2
    o_ref[...] = (gb_ref[...] * y).astype(o_ref.dtype)
    prev_ref[0:1, :] = z[t - 2:t - 1, :]
    prev_ref[1:2, :] = z[t - 1:t, :]


def _short_conv(ha, conv_w, batch, seq):
    n = ha.shape[0]
    t = min(512, seq)
    nj = seq // t
    spec = lambda c: pl.BlockSpec((t, CONV_W), lambda b, j, c=c: (b * nj + j, c))
    return pl.pallas_call(
        _conv_kernel,
        grid=(batch, nj),
        in_specs=[spec(0), spec(1), spec(2), pl.BlockSpec((CONV_TAPS, CONV_W), lambda b, j: (0, 0))],
        out_specs=pl.BlockSpec((t, CONV_W), lambda b, j: (b * nj + j, 0)),
        out_shape=jax.ShapeDtypeStruct((n, CONV_W), BF16),
        scratch_shapes=[pltpu.VMEM((8, CONV_W), F32)],
        compiler_params=_cparams("arbitrary", "arbitrary"),
        name="short_conv",
    )(ha, ha, ha, conv_w)


def _lane_tiles(x):
    return [x[:, u * LANES:(u + 1) * LANES] for u in range(x.shape[1] // LANES)]


def _flash_chains(chains, k_ref, v_ref, dmat_ref, amat_ref, c_lo, c_hi, scratch):
    qm_ref, s_ref, mx_ref, lp_ref, l_ref, acc_ref = scratch
    mx_ref[...] = jnp.full(mx_ref.shape, -jnp.inf, F32)
    lp_ref[...] = jnp.zeros(lp_ref.shape, F32)
    acc_ref[...] = jnp.zeros(acc_ref.shape, F32)

    for g0 in range(0, len(chains), CHAIN_GROUP):
        group = list(enumerate(chains))[g0:g0 + CHAIN_GROUP]

        def pass_a(c, _, group=group, g0=g0):
            k0 = pl.multiple_of(c * KC, KC)
            for n, (k_tile, _, slope) in group:
                kc = k_ref[pl.ds(k0, KC), k_tile * LANES:(k_tile + 1) * LANES]
                s = lax.dot_general(qm_ref[n], kc, (((1,), (1,)), ((), ())), preferred_element_type=F32)
                s = s * LOG2_E - (slope * LOG2_E) * dmat_ref[:, pl.ds(k0, KC)]
                if amat_ref is not None:
                    s = s + amat_ref[:, pl.ds(k0, KC)]
                s_ref[n - g0, :, pl.ds(k0, KC)] = s
                part = mx_ref[n]
                for t in _lane_tiles(s):
                    part = jnp.maximum(part, t)
                mx_ref[n] = part
            return 0

        lax.fori_loop(c_lo, c_hi, pass_a, 0)

        for n, _ in group:
            row_max = jnp.max(mx_ref[n], axis=1, keepdims=True)
            mx_ref[n] = jnp.broadcast_to(row_max, (TQ, LANES))

        def pass_b(c, _, group=group, g0=g0):
            k0 = pl.multiple_of(c * KC, KC)
            for n, (_, v_tile, _) in group:
                row_max = mx_ref[n]
                p_tiles = [jnp.exp2(t - row_max) for t in _lane_tiles(s_ref[n - g0, :, pl.ds(k0, KC)])]
                part = lp_ref[n]
                for t in p_tiles:
                    part = part + t
                lp_ref[n] = part
                p = jnp.concatenate(p_tiles, axis=1).astype(BF16)
                vc = v_ref[pl.ds(k0, KC), v_tile * LANES:(v_tile + 1) * LANES]
                acc_ref[n] = acc_ref[n] + jnp.dot(p, vc, preferred_element_type=F32)
            return 0

        lax.fori_loop(c_lo, c_hi, pass_b, 0)

    for n in range(len(chains)):
        l_ref[n] = jnp.sum(lp_ref[n], axis=1, keepdims=True)


def _split_head_pairs(q_ref, qm_ref, n_tiles, scale):
    low = lax.broadcasted_iota(I32, (TQ, LANES), 1) < D_HEAD
    for j in range(n_tiles):
        qt = q_ref[:, j * LANES:(j + 1) * LANES] * scale
        qm_ref[2 * j] = jnp.where(low, qt, 0).astype(BF16)
        qm_ref[2 * j + 1] = jnp.where(low, 0, qt).astype(BF16)


def _merge_head_pairs(o_ref, l_ref, acc_ref, n_tiles):
    low = lax.broadcasted_iota(I32, (TQ, LANES), 1) < D_HEAD
    for j in range(n_tiles):
        out = jnp.where(low, acc_ref[2 * j] / l_ref[2 * j], acc_ref[2 * j + 1] / l_ref[2 * j + 1])
        o_ref[:, j * LANES:(j + 1) * LANES] = out.astype(o_ref.dtype)


def _flash_scratch(n_chains, seq):
    return [pltpu.VMEM((n_chains, TQ, LANES), BF16), pltpu.VMEM((CHAIN_GROUP, TQ, seq), F32),
            pltpu.VMEM((n_chains, TQ, LANES), F32), pltpu.VMEM((n_chains, TQ, LANES), F32),
            pltpu.VMEM((n_chains, TQ, 1), F32), pltpu.VMEM((n_chains, TQ, LANES), F32)]


def _dsa_kernel(q_ref, k_ref, v_ref, iq_ref, ik_ref, iw_ref, o_ref, sc_ref, sct_ref, dmat_ref, *scratch, k_sel):
    i = pl.program_id(1)
    seq_len = sc_ref.shape[1]
    t0 = i * TQ
    n_chunks = (t0 + TQ + KC - 1) // KC
    idx_scale = (D_IDX ** -0.5) * (N_IDX_HEADS ** -0.5)
    row = t0 + lax.broadcasted_iota(I32, (TQ, KC), 0)
    col_in_chunk = lax.broadcasted_iota(I32, (TQ, KC), 1)

    iq = iq_ref[...]
    iw = iw_ref[:, 0:N_IDX_HEADS]

    def score_body(c, _):
        k0 = pl.multiple_of(c * KC, KC)
        ik = ik_ref[pl.ds(k0, KC), 0:D_IDX]
        sc = jnp.zeros((TQ, KC), F32)
        for h in range(N_IDX_HEADS):
            rel = lax.dot_general(iq[:, h * D_IDX:(h + 1) * D_IDX], ik,
                                  (((1,), (1,)), ((), ())), preferred_element_type=F32)
            sc = sc + iw[:, h:h + 1] * jnp.maximum(rel, 0.0)
        causal = (k0 + col_in_chunk) <= row
        sc = jnp.where(causal, sc * idx_scale, -jnp.inf)
        sc_ref[:, pl.ds(k0, KC)] = sc
        sct_ref[pl.ds(k0, KC), :] = sc.T
        return 0

    lax.fori_loop(0, n_chunks, score_body, 0)

    def code_to_float(code):
        return lax.bitcast_convert_type(jnp.where(code < 0, code ^ 0x7FFFFFFF, code), F32)

    def count(pred):
        def body(c, acc):
            k0 = pl.multiple_of(c * KC, KC)
            hit = jnp.where(pred(sct_ref[pl.ds(k0, KC), :]), 1.0, 0.0)
            return acc + jnp.sum(hit.reshape(KC // COUNT_ROWS, COUNT_ROWS, TQ), axis=0)
        acc = lax.fori_loop(0, n_chunks, body, jnp.zeros((COUNT_ROWS, TQ), F32))
        return jnp.sum(acc, axis=0, keepdims=True)

    kf = float(k_sel)
    has_k = count(lambda t: t > -jnp.inf) >= kf
    code = jnp.where(count(lambda t: t >= 0.0) >= kf, 0, INT_MIN).astype(I32)

    def bit_body(b, code):
        cand = code + lax.shift_left(jnp.int32(1), 30 - b)
        cand_f = code_to_float(cand)
        return jnp.where(count(lambda t: t >= cand_f) >= kf, cand, code)

    code = lax.fori_loop(0, 31, bit_body, code)
    thr_q = jnp.where(has_k, code_to_float(code), -jnp.inf)
    need_q = jnp.where(has_k, kf - count(lambda t: t > thr_q), float(seq_len))
    thr = jnp.broadcast_to(thr_q, (LANES, TQ)).T
    need = jnp.broadcast_to(need_q, (LANES, TQ)).T

    tri = (lax.broadcasted_iota(I32, (KC, KC), 0) <= lax.broadcasted_iota(I32, (KC, KC), 1)).astype(BF16)
    thr_w = jnp.concatenate([thr] * (KC // LANES), axis=1)
    need_w = jnp.concatenate([need] * (KC // LANES), axis=1)

    def mask_body(c, ties_before):
        k0 = pl.multiple_of(c * KC, KC)
        sc = sc_ref[:, pl.ds(k0, KC)]
        eq = sc == thr_w
        eqf = jnp.where(eq, 1.0, 0.0)
        rank = ties_before + jnp.dot(eqf.astype(BF16), tri, preferred_element_type=F32)
        dist = row - (k0 + col_in_chunk)
        sel = ((sc > thr_w) | (eq & (rank <= need_w))) & (dist >= 0)
        dmat_ref[:, pl.ds(k0, KC)] = jnp.where(sel, dist.astype(F32), jnp.inf)
        return ties_before + jnp.sum(eqf, axis=1, keepdims=True)

    lax.fori_loop(0, n_chunks, mask_body, jnp.zeros((TQ, 1), F32))

    slopes = _alibi_slopes(N_HEADS_SPARSE)
    n_tiles = N_HEADS_SPARSE * D_HEAD // LANES
    _split_head_pairs(q_ref, scratch[0], n_tiles, D_HEAD ** -0.5)
    chains = [(h // 2, h // 2, slopes[h]) for h in range(N_HEADS_SPARSE)]
    _flash_chains(chains, k_ref, v_ref, dmat_ref, None, 0, n_chunks, scratch)
    _merge_head_pairs(o_ref, scratch[-2], scratch[-1], n_tiles)


def _dsa_attention(ha, hb, batch, seq):
    n = hb.shape[0]
    nq = seq // TQ
    k_sel = min(TOPK_LIMIT, seq // 4)
    width = N_HEADS_SPARSE * D_HEAD
    iq_w = N_IDX_HEADS * D_IDX
    return pl.pallas_call(
        functools.partial(_dsa_kernel, k_sel=k_sel),
        grid=(batch, nq),
        in_specs=[
            pl.BlockSpec((TQ, width), lambda b, i: (b * nq + i, 0)),
            pl.BlockSpec((seq, width), lambda b, i: (b, 1)),
            pl.BlockSpec((seq, width), lambda b, i: (b, 2)),
            pl.BlockSpec((TQ, iq_w), lambda b, i: (b * nq + i, 3 * width // iq_w)),
            pl.BlockSpec((seq, LANES), lambda b, i: (b, (3 * width + iq_w) // LANES)),
            pl.BlockSpec((TQ, LANES), lambda b, i: (b * nq + i, 3 * CONV_W // LANES)),
        ],
        out_specs=pl.BlockSpec((TQ, width), lambda b, i: (b * nq + i, 0)),
        out_shape=jax.ShapeDtypeStruct((n, width), BF16),
        scratch_shapes=[pltpu.VMEM((TQ, seq), F32), pltpu.VMEM((seq, TQ), F32), pltpu.VMEM((TQ, seq), F32)]
        + _flash_scratch(N_HEADS_SPARSE, seq),
        compiler_params=_cparams("arbitrary", "arbitrary"),
        name="dsa_attention",
    )(hb, hb, hb, hb, hb, ha)


def _diff_kernel(q_ref, k_ref, v_ref, lam_ref, g_ref, o_ref, dmat_ref, *scratch, lam_init):
    i = pl.program_id(1)
    t0 = i * TQ
    n_chunks = (t0 + TQ + KC - 1) // KC
    row = t0 + lax.broadcasted_iota(I32, (TQ, KC), 0)
    col_in_chunk = lax.broadcasted_iota(I32, (TQ, KC), 1)

    def mask_body(c, _):
        k0 = pl.multiple_of(c * KC, KC)
        dist = row - (k0 + col_in_chunk)
        dmat_ref[:, pl.ds(k0, KC)] = jnp.where(dist >= 0, dist.astype(F32), jnp.inf)
        return 0

    lax.fori_loop(0, n_chunks, mask_body, 0)

    lv = lam_ref[...]
    lam = (jnp.exp(jnp.sum(lv[0:1, :] * lv[1:2, :], axis=1, keepdims=True))
           - jnp.exp(jnp.sum(lv[2:3, :] * lv[3:4, :], axis=1, keepdims=True)) + lam_init)

    slopes = _alibi_slopes(N_HEADS_DIFF)
    _split_head_pairs(q_ref, scratch[0], N_HEADS_DIFF, D_HEAD ** -0.5)
    chains = [(n // 2, n // 2, slopes[n // 2]) for n in range(2 * N_HEADS_DIFF)]
    _flash_chains(chains, k_ref, v_ref, dmat_ref, None, 0, n_chunks, scratch)
    l_ref, acc_ref = scratch[-2], scratch[-1]
    g = g_ref[...]
    for h in range(N_HEADS_DIFF):
        of = acc_ref[2 * h] / l_ref[2 * h] - lam * (acc_ref[2 * h + 1] / l_ref[2 * h + 1])
        of = of * lax.rsqrt(jnp.mean(of * of, axis=1, keepdims=True) + RMS_EPS) * g
        o_ref[:, h * LANES:(h + 1) * LANES] = (of * (1.0 - lam_init)).astype(o_ref.dtype)


def _diff_attention(hb, lam_vecs, subln_g, batch, seq, layer):
    n = hb.shape[0]
    nq = seq // TQ
    width = N_HEADS_DIFF * 2 * D_HEAD
    lam_init = 0.8 - 0.6 * math.exp(-0.3 * layer)
    return pl.pallas_call(
        functools.partial(_diff_kernel, lam_init=lam_init),
        grid=(batch, nq),
        in_specs=[
            pl.BlockSpec((TQ, width), lambda b, i: (b * nq + i, 0)),
            pl.BlockSpec((seq, width), lambda b, i: (b, 1)),
            pl.BlockSpec((seq, width), lambda b, i: (b, 2)),
            pl.BlockSpec(lam_vecs.shape, lambda b, i: (0, 0)),
            pl.BlockSpec(subln_g.shape, lambda b, i: (0, 0)),
        ],
        out_specs=pl.BlockSpec((TQ, width), lambda b, i: (b * nq + i, 0)),
        out_shape=jax.ShapeDtypeStruct((n, width), BF16),
        scratch_shapes=[pltpu.VMEM((TQ, seq), F32)] + _flash_scratch(2 * N_HEADS_DIFF, seq),
        compiler_params=_cparams("arbitrary", "arbitrary"),
        name="diff_attention",
    )(hb, hb, hb, lam_vecs, subln_g)


def _dilated_kernel(q_ref, k_ref, v_ref, o_ref, dmat_ref, amat_ref, *scratch):
    i = pl.program_id(1)
    t0 = i * TQ
    w_max = max(w for w, _ in DIL_GROUPS)
    c_lo = jnp.maximum(t0 - w_max, 0) // KC
    c_hi = (t0 + TQ + KC - 1) // KC
    row = t0 + lax.broadcasted_iota(I32, (TQ, KC), 0)
    col_in_chunk = lax.broadcasted_iota(I32, (TQ, KC), 1)

    def mask_body(c, _):
        k0 = pl.multiple_of(c * KC, KC)
        dist = row - (k0 + col_in_chunk)
        mult = jnp.zeros((TQ, KC), F32)
        for w, d in DIL_GROUPS:
            member = (dist >= 0) & (dist <= w) & ((dist & (d - 1)) == 0)
            mult = mult + jnp.where(member, 1.0, 0.0)
        on = mult > 0.0
        dmat_ref[:, pl.ds(k0, KC)] = jnp.where(on, dist.astype(F32), jnp.inf)
        amat_ref[:, pl.ds(k0, KC)] = jnp.log2(jnp.where(on, mult, 1.0))
        return 0

    lax.fori_loop(c_lo, c_hi, mask_body, 0)

    slopes = _alibi_slopes(N_HEADS_DIL)
    n_tiles = N_HEADS_DIL * D_HEAD // LANES
    _split_head_pairs(q_ref, scratch[0], n_tiles, D_HEAD ** -0.5)
    chains = [(h // 2, h // 2, slopes[h]) for h in range(N_HEADS_DIL)]
    _flash_chains(chains, k_ref, v_ref, dmat_ref, amat_ref, c_lo, c_hi, scratch)
    _merge_head_pairs(o_ref, scratch[-2], scratch[-1], n_tiles)


def _dilated_attention(hb, batch, seq):
    n = hb.shape[0]
    nq = seq // TQ
    width = N_HEADS_DIL * D_HEAD
    return pl.pallas_call(
        _dilated_kernel,
        grid=(batch, nq),
        in_specs=[
            pl.BlockSpec((TQ, width), lambda b, i: (b * nq + i, 3)),
            pl.BlockSpec((seq, width), lambda b, i: (b, 4)),
            pl.BlockSpec((seq, width), lambda b, i: (b, 5)),
        ],
        out_specs=pl.BlockSpec((TQ, width), lambda b, i: (b * nq + i, 0)),
        out_shape=jax.ShapeDtypeStruct((n, width), BF16),
        scratch_shapes=[pltpu.VMEM((TQ, seq), F32), pltpu.VMEM((TQ, seq), F32)] + _flash_scratch(N_HEADS_DIL, seq),
        compiler_params=_cparams("arbitrary", "arbitrary"),
        name="dilated_attention",
    )(hb, hb, hb)


def _mix_out_kernel(ya_ref, yb_ref, wa_ref, wb_ref, x_ref, g_ref, b_ref, rw_ref, rb_ref,
                    x1_ref, x1t_ref, eidx_ref, gate_ref, rank_ref, cnt_ref, carry_ref):
    step = pl.program_id(0)

    @pl.when(step == 0)
    def _():
        carry_ref[...] = jnp.zeros_like(carry_ref)

    m = (jnp.dot(ya_ref[...], wa_ref[...], preferred_element_type=F32)
         + jnp.dot(yb_ref[...], wb_ref[...], preferred_element_type=F32))
    x1 = _layer_norm(DEEPNORM_ALPHA * x_ref[...] + m, g_ref[...], b_ref[...])
    x1_ref[...] = x1
    _to_token_tiles(x1t_ref, x1)

    t = x1.shape[0]
    logits = jnp.dot(x1.astype(BF16), rw_ref[...], preferred_element_type=F32) + rb_ref[...]
    lane = lax.broadcasted_iota(I32, (t, LANES), 1)
    vals, idxs = [], []
    lg = logits
    for _ in range(TOP_K):
        mx = jnp.max(lg, axis=1, keepdims=True)
        ix = jnp.min(jnp.where(lg == mx, lane, LANES), axis=1, keepdims=True)
        vals.append(mx)
        idxs.append(ix)
        lg = jnp.where(lane == ix, -jnp.inf, lg)
    exps = [jnp.exp(v - vals[0]) for v in vals]
    denom = exps[0]
    for e in exps[1:]:
        denom = denom + e

    onehot = jnp.zeros((t, LANES), F32)
    for ix in idxs:
        onehot = onehot + jnp.where(lane == ix, 1.0, 0.0)
    strict = (lax.broadcasted_iota(I32, (t, t), 1) < lax.broadcasted_iota(I32, (t, t), 0)).astype(BF16)
    before = jnp.dot(strict, onehot.astype(BF16), preferred_element_type=F32) + carry_ref[...]

    eidx = jnp.zeros((t, LANES), I32)
    gate = jnp.zeros((t, LANES), F32)
    rank = jnp.zeros((t, LANES), I32)
    for k in range(TOP_K):
        rk = jnp.sum(jnp.where(lane == idxs[k], before, 0.0), axis=1, keepdims=True)
        eidx = jnp.where(lane == k, idxs[k], eidx)
        gate = jnp.where(lane == k, exps[k] / denom, gate)
        rank = jnp.where(lane == k, rk.astype(I32), rank)
    eidx_ref[...] = eidx
    gate_ref[...] = gate
    rank_ref[...] = rank
    carry_ref[...] = carry_ref[...] + jnp.sum(onehot, axis=0, keepdims=True)
    cnt_ref[...] = carry_ref[...]


def _mix_out(ya, yb, w_out, x2d, ln_g, ln_b, router_w, router_b):
    n, d = x2d.shape
    half = ya.shape[1]
    wa = w_out[:half].astype(BF16)
    wb = w_out[half:].astype(BF16)
    rw = jnp.zeros((d, LANES), BF16).at[:, :N_EXPERTS].set(router_w.astype(BF16))
    rb = jnp.full((1, LANES), -jnp.inf, F32).at[0, :N_EXPERTS].set(router_b)
    t = ROW_TILE
    full = lambda a: pl.BlockSpec(a.shape, lambda i: (0, 0))
    rows = lambda w: pl.BlockSpec((t, w), lambda i: (i, 0))
    g2, b2 = ln_g.reshape(1, d), ln_b.reshape(1, d)
    return pl.pallas_call(
        _mix_out_kernel,
        grid=(n // t,),
        in_specs=[rows(half), rows(half), full(wa), full(wb), rows(d), full(g2), full(b2), full(rw), full(rb)],
        out_specs=[rows(d), pl.BlockSpec((t * ROW_SUBLANES, LANES), lambda i: (i, 0)),
                   rows(LANES), rows(LANES), rows(LANES), pl.BlockSpec((1, LANES), lambda i: (0, 0))],
        out_shape=[jax.ShapeDtypeStruct((n, d), F32), jax.ShapeDtypeStruct((n * ROW_SUBLANES, LANES), F32),
                   jax.ShapeDtypeStruct((n, LANES), I32),
                   jax.ShapeDtypeStruct((n, LANES), F32), jax.ShapeDtypeStruct((n, LANES), I32),
                   jax.ShapeDtypeStruct((1, LANES), F32)],
        scratch_shapes=[pltpu.VMEM((1, LANES), F32)],
        compiler_params=_cparams("arbitrary"),
        name="mix_out_router",
    )(ya, yb, wa, wb, x2d, g2, b2, rw, rb)


ROW_SUBLANES = 8


def _to_token_tiles(o_ref, x):
    for j in range(ROW_SUBLANES):
        o_ref[pl.ds(j, x.shape[0], stride=ROW_SUBLANES), :] = x[:, j * LANES:(j + 1) * LANES]


def _from_token_tiles(buf_ref, n_rows):
    return [buf_ref[pl.ds(j, n_rows, stride=ROW_SUBLANES), :] for j in range(ROW_SUBLANES)]


def _row_copy(src_hbm, dst_ref, src_row, dst_row, sem):
    src = src_hbm.at[pl.ds(pl.multiple_of(src_row * ROW_SUBLANES, ROW_SUBLANES), ROW_SUBLANES)]
    dst = dst_ref.at[pl.ds(pl.multiple_of(dst_row * ROW_SUBLANES, ROW_SUBLANES), ROW_SUBLANES)]
    return pltpu.make_async_copy(src, dst, sem)


def _rows_wait(src_hbm, dst_ref, sem):
    pltpu.make_async_copy(src_hbm.at[pl.ds(0, dst_ref.shape[0])], dst_ref, sem).wait()


def _expert_kernel(blk_e_ref, n_used_ref, run_start_ref, run_parity_ref, next_e_ref,
                   tok_ref, tok_next_ref, x_hbm, wgu_hbm, bgu_ref, wdn_hbm, bdn_ref,
                   o_ref, xbuf, wgu_raw, wdn_raw, wgu_bf, wdn_bf, sem, wsem):
    i = pl.program_id(0)
    n_used = n_used_ref[0]
    slot = i % 2
    e = blk_e_ref[i]
    parity = run_parity_ref[i]

    def weight_copies(expert, p):
        return (pltpu.make_async_copy(wgu_hbm.at[expert], wgu_raw.at[p], wsem.at[p, 0]),
                pltpu.make_async_copy(wdn_hbm.at[expert], wdn_raw.at[p], wsem.at[p, 1]))

    def gather(rows_ref, s):
        def body(g, _):
            for u in range(ISSUE_UNROLL):
                r = g * ISSUE_UNROLL + u
                _row_copy(x_hbm, xbuf.at[s], rows_ref[r], r, sem.at[s]).start(priority=u % 2)
            return 0
        lax.fori_loop(0, MOE_BM // ISSUE_UNROLL, body, 0)

    @pl.when(i == 0)
    def _():
        gather(tok_ref, 0)

        @pl.when(run_start_ref[0] == 1)
        def _():
            for c in weight_copies(e, parity):
                c.start()

    @pl.when(run_start_ref[i] == 1)
    def _():
        for c in weight_copies(e, parity):
            c.wait()
        wgu_bf[...] = wgu_raw[parity].astype(BF16)
        wdn_bf[...] = wdn_raw[parity].astype(BF16)
        e_next = next_e_ref[i]

        @pl.when(e_next >= 0)
        def _():
            for c in weight_copies(e_next, 1 - parity):
                c.start()

    gather(tok_next_ref, 1 - slot)
    _rows_wait(x_hbm, xbuf.at[slot], sem.at[slot])

    @pl.when(i < n_used)
    def _():
        x = jnp.concatenate(_from_token_tiles(xbuf.at[slot], MOE_BM), axis=1).astype(BF16)
        h = jnp.dot(x, wgu_bf[...], preferred_element_type=F32) + bgu_ref[...]
        gate = jnp.minimum(h[:, :D_FF], SWIGLU_LIMIT)
        up = jnp.clip(h[:, D_FF:], -SWIGLU_LIMIT, SWIGLU_LIMIT)
        glu = gate * (1.0 / (1.0 + jnp.exp(-SWIGLU_ALPHA * gate)))
        act = ((up + 1.0) * glu).astype(BF16)
        _to_token_tiles(o_ref, jnp.dot(act, wdn_bf[...], preferred_element_type=F32) + bdn_ref[...])

    @pl.when(i >= n_used)
    def _():
        o_ref[...] = jnp.zeros_like(o_ref)

    @pl.when(i == pl.num_programs(0) - 1)
    def _():
        _rows_wait(x_hbm, xbuf.at[1 - slot], sem.at[1 - slot])


def _expert_ffn(x_tiles, row_tok, blk_e, n_used, w_gu, b_gu, w_dn, b_dn):
    n_rows = row_tok.shape[0]
    d = w_gu.shape[1]
    n_blocks = n_rows // MOE_BM
    ne = w_gu.shape[0]
    blk = jnp.arange(n_blocks, dtype=I32)
    prev_e = jnp.concatenate([blk_e[:1], blk_e[:-1]])
    run_start = (blk < n_used[0]) & ((blk == 0) | (blk_e != prev_e))
    run_parity = ((jnp.cumsum(run_start.astype(I32)) - 1) % 2).astype(I32)
    first_start_from = lax.cummin(jnp.where(run_start, blk, n_blocks)[::-1])[::-1]
    next_start = jnp.concatenate([first_start_from[1:], jnp.full((1,), n_blocks, I32)])
    next_e = jnp.where(next_start < n_blocks, blk_e[jnp.minimum(next_start, n_blocks - 1)], -1).astype(I32)
    idx = lambda f: (lambda i, *prefetch: f(i, prefetch[0]))
    grid_spec = pltpu.PrefetchScalarGridSpec(
        num_scalar_prefetch=5,
        grid=(n_blocks,),
        in_specs=[
            pl.BlockSpec((MOE_BM,), idx(lambda i, be: (i,)), memory_space=pltpu.SMEM),
            pl.BlockSpec((MOE_BM,), idx(lambda i, be: (jnp.minimum(i + 1, n_blocks - 1),)), memory_space=pltpu.SMEM),
            pl.BlockSpec(memory_space=pl.ANY),
            pl.BlockSpec(memory_space=pl.ANY),
            pl.BlockSpec((None, 1, 2 * D_FF), idx(lambda i, be: (be[i], 0, 0))),
            pl.BlockSpec(memory_space=pl.ANY),
            pl.BlockSpec((None, 1, d), idx(lambda i, be: (be[i], 0, 0))),
        ],
        out_specs=pl.BlockSpec((MOE_BM * ROW_SUBLANES, LANES), idx(lambda i, be: (i, 0))),
        scratch_shapes=[pltpu.VMEM((2, MOE_BM * ROW_SUBLANES, LANES), F32),
                        pltpu.VMEM((2, d, 2 * D_FF), F32), pltpu.VMEM((2, D_FF, d), F32),
                        pltpu.VMEM((d, 2 * D_FF), BF16), pltpu.VMEM((D_FF, d), BF16),
                        pltpu.SemaphoreType.DMA((2,)), pltpu.SemaphoreType.DMA((2, 2))],
    )
    return pl.pallas_call(
        _expert_kernel,
        grid_spec=grid_spec,
        out_shape=jax.ShapeDtypeStruct((n_rows * ROW_SUBLANES, LANES), F32),
        compiler_params=_cparams("arbitrary"),
        name="moe_experts",
    )(blk_e, n_used, run_start.astype(I32), run_parity, next_e,
      row_tok, row_tok, x_tiles, w_gu, b_gu.reshape(ne, 1, 2 * D_FF), w_dn, b_dn.reshape(ne, 1, d))


def _combine_kernel(dest_ref, dest_next_ref, y_hbm, gate_ref, x_ref, g_ref, b_ref, o_ref, buf_ref, sem):
    i = pl.program_id(0)
    t = x_ref.shape[0]
    slot = i % 2

    def gather(rows_ref, s):
        def body(g, _):
            for u in range(ISSUE_UNROLL // TOP_K):
                r = g * (ISSUE_UNROLL // TOP_K) + u
                for k in range(TOP_K):
                    _row_copy(y_hbm, buf_ref.at[s, k], rows_ref[r * TOP_K + k], r, sem.at[s]).start(priority=k % 2)
            return 0
        lax.fori_loop(0, t * TOP_K // ISSUE_UNROLL, body, 0)

    @pl.when(i == 0)
    def _():
        gather(dest_ref, 0)

    @pl.when(i + 1 < pl.num_programs(0))
    def _():
        gather(dest_next_ref, 1 - slot)

    for k in range(TOP_K):
        _rows_wait(y_hbm, buf_ref.at[slot, k], sem.at[slot])
    gate = gate_ref[...]
    f_tiles = None
    for k in range(TOP_K):
        y_tiles = [gate[:, k:k + 1] * yt for yt in _from_token_tiles(buf_ref.at[slot, k], t)]
        f_tiles = y_tiles if f_tiles is None else [a + b for a, b in zip(f_tiles, y_tiles)]
    f = jnp.concatenate(f_tiles, axis=1)
    o_ref[...] = _layer_norm(DEEPNORM_ALPHA * x_ref[...] + f, g_ref[...], b_ref[...])


def _combine(ys, dest_flat, gates, x2d, ln_g, ln_b):
    n, d = x2d.shape
    t = GATHER_T
    g2, b2 = ln_g.reshape(1, d), ln_b.reshape(1, d)
    full = lambda a: pl.BlockSpec(a.shape, lambda i: (0, 0))
    n_steps = n // t
    return pl.pallas_call(
        _combine_kernel,
        grid=(n_steps,),
        in_specs=[pl.BlockSpec((t * TOP_K,), lambda i: (i,), memory_space=pltpu.SMEM),
                  pl.BlockSpec((t * TOP_K,), lambda i: (jnp.minimum(i + 1, n_steps - 1),),
                               memory_space=pltpu.SMEM),
                  pl.BlockSpec(memory_space=pl.ANY),
                  pl.BlockSpec((t, LANES), lambda i: (i, 0)),
                  pl.BlockSpec((t, d), lambda i: (i, 0)),
                  full(g2), full(b2)],
        out_specs=pl.BlockSpec((t, d), lambda i: (i, 0)),
        out_shape=jax.ShapeDtypeStruct((n, d), F32),
        scratch_shapes=[pltpu.VMEM((2, TOP_K, t * ROW_SUBLANES, LANES), F32), pltpu.SemaphoreType.DMA((2,))],
        compiler_params=_cparams("arbitrary"),
        name="moe_combine",
    )(dest_flat, dest_flat, ys, gates, x2d, g2, b2)


def _row_token_kernel(dest_ref, pad_lo_ref, pad_hi_ref, o_ref):
    def clear_segment(e, _):
        def clear(r, _):
            o_ref[r] = 0
            return 0
        lax.fori_loop(pad_lo_ref[e], pad_hi_ref[e], clear, 0)
        return 0

    lax.fori_loop(0, pad_lo_ref.shape[0], clear_segment, 0)

    def put(g, _):
        base = g * ISSUE_UNROLL
        rows = [dest_ref[base + u] for u in range(ISSUE_UNROLL)]
        tok0 = g * (ISSUE_UNROLL // TOP_K)
        for u in range(ISSUE_UNROLL):
            o_ref[rows[u]] = tok0 + u // TOP_K
        return 0

    lax.fori_loop(0, dest_ref.shape[0] // ISSUE_UNROLL, put, 0)


def _row_tokens(dest_flat, pad_lo, pad_hi, n_rows):
    smem = pl.BlockSpec(memory_space=pltpu.SMEM)
    return pl.pallas_call(
        _row_token_kernel,
        in_specs=[smem, smem, smem],
        out_specs=smem,
        out_shape=jax.ShapeDtypeStruct((n_rows,), I32),
        name="moe_row_tokens",
    )(dest_flat, pad_lo, pad_hi)


def _moe_block(x1, x1_tiles, eidx, gates, rank, counts, w_gu, b_gu, w_dn, b_dn, ln_g, ln_b):
    n = x1.shape[0]
    e_sel = eidx[:, :TOP_K]
    cnt = counts[0, :N_EXPERTS].astype(I32)
    padded = ((cnt + MOE_BM - 1) // MOE_BM) * MOE_BM
    pad_end = jnp.cumsum(padded)
    pad_start = pad_end - padded
    dest = pad_start[e_sel] + rank[:, :TOP_K]
    n_rows = ((n * TOP_K + N_EXPERTS * (MOE_BM - 1) + MOE_BM - 1) // MOE_BM) * MOE_BM
    n_blocks = n_rows // MOE_BM
    dest_flat = dest.reshape(-1)
    pad_lo = jnp.concatenate([pad_start + cnt, pad_end[-1:]]).astype(I32)
    pad_hi = jnp.concatenate([pad_end, jnp.full((1,), n_rows)]).astype(I32)
    row_tok = _row_tokens(dest_flat, pad_lo, pad_hi, n_rows)
    blk_start = jnp.arange(n_blocks, dtype=I32) * MOE_BM
    blk_e = jnp.minimum(jnp.sum((pad_end[None, :] <= blk_start[:, None]).astype(I32), axis=1), N_EXPERTS - 1)
    n_used = (pad_end[-1:] // MOE_BM).astype(I32)
    ys = _expert_ffn(x1_tiles, row_tok, blk_e, n_used, w_gu, b_gu, w_dn, b_dn)
    return _combine(ys, dest_flat, gates, x1, ln_g, ln_b)


def _pad_cols(w, width):
    return jnp.pad(w, ((0, 0), (0, width - w.shape[1])))


def kernel(x, w_in_0, conv_w_0, w_out_0, ln_mix_g_0, ln_mix_b_0, router_w_0, router_b_0, w_gu_0, b_gu_0, w_dn_0, b_dn_0, ln_ffn_g_0, ln_ffn_b_0, w_in_1, lam_q1_1, lam_k1_1, lam_q2_1, lam_k2_1, subln_g_1, w_out_1, ln_mix_g_1, ln_mix_b_1, router_w_1, router_b_1, w_gu_1, b_gu_1, w_dn_1, b_dn_1, ln_ffn_g_1, ln_ffn_b_1):
    batch, seq, d = x.shape
    x0 = x.reshape(batch * seq, d)

    n_f32 = 3 * CONV_W
    n_attn = 3 * N_HEADS_SPARSE * D_HEAD + N_IDX_HEADS * D_IDX + D_IDX
    w_a = _pad_cols(jnp.concatenate([w_in_0[:, :n_f32], w_in_0[:, n_f32 + n_attn:]], axis=1),
                    n_f32 + LANES)
    w_b = _pad_cols(w_in_0[:, n_f32:n_f32 + n_attn], 3 * N_HEADS_SPARSE * D_HEAD + N_IDX_HEADS * D_IDX + LANES)
    ha, hb = _project(x0, [w_a.astype(BF16), w_b.astype(BF16)], [F32, BF16])
    ya = _short_conv(ha, conv_w_0, batch, seq)
    yb = _dsa_attention(ha, hb, batch, seq)
    x1, x1t, eidx, gates, rank, counts = _mix_out(ya, yb, w_out_0, x0, ln_mix_g_0, ln_mix_b_0, router_w_0, router_b_0)
    x2 = _moe_block(x1, x1t, eidx, gates, rank, counts, w_gu_0, b_gu_0, w_dn_0, b_dn_0, ln_ffn_g_0, ln_ffn_b_0)

    (hc,) = _project(x2, [w_in_1.astype(BF16)], [BF16])
    lam_vecs = jnp.stack([lam_q1_1, lam_k1_1, lam_q2_1, lam_k2_1]).astype(F32)
    yc = _diff_attention(hc, lam_vecs, subln_g_1.reshape(1, -1).astype(F32), batch, seq, 1)
    yd = _dilated_attention(hc, batch, seq)
    x3, x3t, eidx, gates, rank, counts = _mix_out(yc, yd, w_out_1, x2, ln_mix_g_1, ln_mix_b_1, router_w_1, router_b_1)
    x4 = _moe_block(x3, x3t, eidx, gates, rank, counts, w_gu_1, b_gu_1, w_dn_1, b_dn_1, ln_ffn_g_1, ln_ffn_b_1)
    return x4.reshape(batch, seq, d)
```

```python
import functools
import math

import jax
import jax.numpy as jnp
from jax import lax
from jax.experimental import pallas as pl
from jax.experimental.pallas import tpu as pltpu

F32 = jnp.float32
BF16 = jnp.bfloat16
I32 = jnp.int32

CONV_W = 512
CONV_TAPS = 3
N_HEADS_SPARSE = 8
D_HEAD = 64
N_IDX_HEADS = 8
D_IDX = 32
TOPK_LIMIT = 256
N_HEADS_DIFF = 4
N_HEADS_DIL = 8
DIL_GROUPS = ((128, 1), (512, 4), (2048, 16))
N_EXPERTS = 32
TOP_K = 4
D_FF = 1024
SWIGLU_LIMIT = 7.0
SWIGLU_ALPHA = 1.702
DEPTH = 2
DEEPNORM_ALPHA = (2 * DEPTH) ** 0.25
LN_EPS = 1e-5
RMS_EPS = 1e-5

LANES = 128
VMEM_LIMIT = 56 * 1024 * 1024
TQ = 256
CHAIN_GROUP = 4
KC = 512
ROW_TILE = 256
MOE_BM = 256
GATHER_T = 128
ISSUE_UNROLL = 8
COUNT_ROWS = 64
LOG2_E = math.log2(math.e)
INT_MIN = -2 ** 31


def _alibi_slopes(n):
    return [2.0 ** (-8.0 * (h + 1) / n) for h in range(n)]


def _cparams(*sem):
    return pltpu.CompilerParams(dimension_semantics=sem, vmem_limit_bytes=VMEM_LIMIT)


def _layer_norm(z, g, b):
    mu = jnp.mean(z, axis=-1, keepdims=True)
    zc = z - mu
    var = jnp.mean(zc * zc, axis=-1, keepdims=True)
    return zc * lax.rsqrt(var + LN_EPS) * g + b


def _proj_kernel(*refs, n_out):
    x_ref = refs[0]
    w_refs = refs[1:1 + n_out]
    o_refs = refs[1 + n_out:]
    xb = x_ref[...].astype(BF16)
    for w_ref, o_ref in zip(w_refs, o_refs):
        o_ref[...] = jnp.dot(xb, w_ref[...], preferred_element_type=F32).astype(o_ref.dtype)


def _project(x2d, weights, out_dtypes):
    n, d = x2d.shape
    n_out = len(weights)
    in_specs = [pl.BlockSpec((ROW_TILE, d), lambda i: (i, 0))]
    in_specs += [pl.BlockSpec(w.shape, lambda i: (0, 0)) for w in weights]
    out_specs = [pl.BlockSpec((ROW_TILE, w.shape[1]), lambda i: (i, 0)) for w in weights]
    out_shape = [jax.ShapeDtypeStruct((n, w.shape[1]), dt) for w, dt in zip(weights, out_dtypes)]
    return pl.pallas_call(
        functools.partial(_proj_kernel, n_out=n_out),
        grid=(n // ROW_TILE,),
        in_specs=in_specs, out_specs=out_specs, out_shape=out_shape,
        compiler_params=_cparams("parallel"),
        name="in_proj",
    )(x2d, *weights)


def _conv_kernel(gb_ref, gc_ref, xa_ref, w_ref, o_ref, prev_ref):
    j = pl.program_id(1)
    t = gb_ref.shape[0]

    @pl.when(j == 0)
    def _():
        prev_ref[...] = jnp.zeros_like(prev_ref)

    z = gc_ref[...] * xa_ref[...]
    row = lax.broadcasted_iota(I32, z.shape, 0)
    prev2 = prev_ref[0:1, :]
    prev1 = prev_ref[1:2, :]
    z1 = jnp.where(row == 0, prev1, pltpu.roll(z, 1, 0))
    z2 = jnp.where(row == 0, prev2, jnp.where(row == 1, prev1, pltpu.roll(z, 2, 0)))
    w = w_ref[...]
    y = w[0:1, :] * z + w[1:2, :] * z1 + w[2:3, :] * z2
    o_ref[...] = (gb_ref[...] * y).astype(o_ref.dtype)
    prev_ref[0:1, :] = z[t - 2:t - 1, :]
    prev_ref[1:2, :] = z[t - 1:t, :]


def _short_conv(ha, conv_w, batch, seq):
    n = ha.shape[0]
    t = min(512, seq)
    nj = seq // t
    spec = lambda c: pl.BlockSpec((t, CONV_W), lambda b, j, c=c: (b * nj + j, c))
    return pl.pallas_call(
        _conv_kernel,
        grid=(batch, nj),
        in_specs=[spec(0), spec(1), spec(2), pl.BlockSpec((CONV_TAPS, CONV_W), lambda b, j: (0, 0))],
        out_specs=pl.BlockSpec((t, CONV_W), lambda b, j: (b * nj + j, 0)),
        out_shape=jax.ShapeDtypeStruct((n, CONV_W), BF16),
        scratch_shapes=[pltpu.VMEM((8, CONV_W), F32)],
        compiler_params=_cparams("arbitrary", "arbitrary"),
        name="short_conv",
    )(ha, ha, ha, conv_w)


def _lane_tiles(x):
    return [x[:, u * LANES:(u + 1) * LANES] for u in range(x.shape[1] // LANES)]


def _flash_chains(chains, k_ref, v_ref, dmat_ref, amat_ref, c_lo, c_hi, scratch):
    qm_ref, s_ref, mx_ref, lp_ref, l_ref, acc_ref = scratch
    mx_ref[...] = jnp.full(mx_ref.shape, -jnp.inf, F32)
    lp_ref[...] = jnp.zeros(lp_ref.shape, F32)
    acc_ref[...] = jnp.zeros(acc_ref.shape, F32)

    for g0 in range(0, len(chains), CHAIN_GROUP):
        group = list(enumerate(chains))[g0:g0 + CHAIN_GROUP]

        def pass_a(c, _, group=group, g0=g0):
            k0 = pl.multiple_of(c * KC, KC)
            for n, (k_tile, _, slope) in group:
                kc = k_ref[pl.ds(k0, KC), k_tile * LANES:(k_tile + 1) * LANES]
                s = lax.dot_general(qm_ref[n], kc, (((1,), (1,)), ((), ())), preferred_element_type=F32)
                s = s * LOG2_E - (slope * LOG2_E) * dmat_ref[:, pl.ds(k0, KC)]
                if amat_ref is not None:
                    s = s + amat_ref[:, pl.ds(k0, KC)]
                s_ref[n - g0, :, pl.ds(k0, KC)] = s
                part = mx_ref[n]
                for t in _lane_tiles(s):
                    part = jnp.maximum(part, t)
                mx_ref[n] = part
            return 0

        lax.fori_loop(c_lo, c_hi, pass_a, 0)

        for n, _ in group:
            row_max = jnp.max(mx_ref[n], axis=1, keepdims=True)
            mx_ref[n] = jnp.broadcast_to(row_max, (TQ, LANES))

        def pass_b(c, _, group=group, g0=g0):
            k0 = pl.multiple_of(c * KC, KC)
            for n, (_, v_tile, _) in group:
                row_max = mx_ref[n]
                p_tiles = [jnp.exp2(t - row_max) for t in _lane_tiles(s_ref[n - g0, :, pl.ds(k0, KC)])]
                part = lp_ref[n]
                for t in p_tiles:
                    part = part + t
                lp_ref[n] = part
                p = jnp.concatenate(p_tiles, axis=1).astype(BF16)
                vc = v_ref[pl.ds(k0, KC), v_tile * LANES:(v_tile + 1) * LANES]
                acc_ref[n] = acc_ref[n] + jnp.dot(p, vc, preferred_element_type=F32)
            return 0

        lax.fori_loop(c_lo, c_hi, pass_b, 0)

    for n in range(len(chains)):
        l_ref[n] = jnp.sum(lp_ref[n], axis=1, keepdims=True)


def _split_head_pairs(q_ref, qm_ref, n_tiles, scale):
    low = lax.broadcasted_iota(I32, (TQ, LANES), 1) < D_HEAD
    for j in range(n_tiles):
        qt = q_ref[:, j * LANES:(j + 1) * LANES] * scale
        qm_ref[2 * j] = jnp.where(low, qt, 0).astype(BF16)
        qm_ref[2 * j + 1] = jnp.where(low, 0, qt).astype(BF16)


def _merge_head_pairs(o_ref, l_ref, acc_ref, n_tiles):
    low = lax.broadcasted_iota(I32, (TQ, LANES), 1) < D_HEAD
    for j in range(n_tiles):
        out = jnp.where(low, acc_ref[2 * j] / l_ref[2 * j], acc_ref[2 * j + 1] / l_ref[2 * j + 1])
        o_ref[:, j * LANES:(j + 1) * LANES] = out.astype(o_ref.dtype)


def _flash_scratch(n_chains, seq):
    return [pltpu.VMEM((n_chains, TQ, LANES), BF16), pltpu.VMEM((CHAIN_GROUP, TQ, seq), F32),
            pltpu.VMEM((n_chains, TQ, LANES), F32), pltpu.VMEM((n_chains, TQ, LANES), F32),
            pltpu.VMEM((n_chains, TQ, 1), F32), pltpu.VMEM((n_chains, TQ, LANES), F32)]


def _dsa_kernel(q_ref, k_ref, v_ref, iq_ref, ik_ref, iw_ref, o_ref, sc_ref, sct_ref, dmat_ref, *scratch, k_sel):
    i = pl.program_id(1)
    seq_len = sc_ref.shape[1]
    t0 = i * TQ
    n_chunks = (t0 + TQ + KC - 1) // KC
    idx_scale = (D_IDX ** -0.5) * (N_IDX_HEADS ** -0.5)
    row = t0 + lax.broadcasted_iota(I32, (TQ, KC), 0)
    col_in_chunk = lax.broadcasted_iota(I32, (TQ, KC), 1)

    iq = iq_ref[...]
    iw = iw_ref[:, 0:N_IDX_HEADS]

    def score_body(c, _):
        k0 = pl.multiple_of(c * KC, KC)
        ik = ik_ref[pl.ds(k0, KC), 0:D_IDX]
        sc = jnp.zeros((TQ, KC), F32)
        for h in range(N_IDX_HEADS):
            rel = lax.dot_general(iq[:, h * D_IDX:(h + 1) * D_IDX], ik,
                                  (((1,), (1,)), ((), ())), preferred_element_type=F32)
            sc = sc + iw[:, h:h + 1] * jnp.maximum(rel, 0.0)
        causal = (k0 + col_in_chunk) <= row
        sc = jnp.where(causal, sc * idx_scale, -jnp.inf)
        sc_ref[:, pl.ds(k0, KC)] = sc
        sct_ref[pl.ds(k0, KC), :] = sc.T
        return 0

    lax.fori_loop(0, n_chunks, score_body, 0)

    def code_to_float(code):
        return lax.bitcast_convert_type(jnp.where(code < 0, code ^ 0x7FFFFFFF, code), F32)

    def count(pred):
        def body(c, acc):
            k0 = pl.multiple_of(c * KC, KC)
            hit = jnp.where(pred(sct_ref[pl.ds(k0, KC), :]), 1.0, 0.0)
            return acc + jnp.sum(hit.reshape(KC // COUNT_ROWS, COUNT_ROWS, TQ), axis=0)
        acc = lax.fori_loop(0, n_chunks, body, jnp.zeros((COUNT_ROWS, TQ), F32))
        return jnp.sum(acc, axis=0, keepdims=True)

    kf = float(k_sel)
    has_k = count(lambda t: t > -jnp.inf) >= kf
    code = jnp.where(count(lambda t: t >= 0.0) >= kf, 0, INT_MIN).astype(I32)

    def bit_body(b, code):
        cand = code + lax.shift_left(jnp.int32(1), 30 - b)
        cand_f = code_to_float(cand)
        return jnp.where(count(lambda t: t >= cand_f) >= kf, cand, code)

    code = lax.fori_loop(0, 31, bit_body, code)
    thr_q = jnp.where(has_k, code_to_float(code), -jnp.inf)
    need_q = jnp.where(has_k, kf - count(lambda t: t > thr_q), float(seq_len))
    thr = jnp.broadcast_to(thr_q, (LANES, TQ)).T
    need = jnp.broadcast_to(need_q, (LANES, TQ)).T

    tri = (lax.broadcasted_iota(I32, (KC, KC), 0) <= lax.broadcasted_iota(I32, (KC, KC), 1)).astype(BF16)
    thr_w = jnp.concatenate([thr] * (KC // LANES), axis=1)
    need_w = jnp.concatenate([need] * (KC // LANES), axis=1)

    def mask_body(c, ties_before):
        k0 = pl.multiple_of(c * KC, KC)
        sc = sc_ref[:, pl.ds(k0, KC)]
        eq = sc == thr_w
        eqf = jnp.where(eq, 1.0, 0.0)
        rank = ties_before + jnp.dot(eqf.astype(BF16), tri, preferred_element_type=F32)
        dist = row - (k0 + col_in_chunk)
        sel = ((sc > thr_w) | (eq & (rank <= need_w))) & (dist >= 0)
        dmat_ref[:, pl.ds(k0, KC)] = jnp.where(sel, dist.astype(F32), jnp.inf)
        return ties_before + jnp.sum(eqf, axis=1, keepdims=True)

    lax.fori_loop(0, n_chunks, mask_body, jnp.zeros((TQ, 1), F32))

    slopes = _alibi_slopes(N_HEADS_SPARSE)
    n_tiles = N_HEADS_SPARSE * D_HEAD // LANES
    _split_head_pairs(q_ref, scratch[0], n_tiles, D_HEAD ** -0.5)
    chains = [(h // 2, h // 2, slopes[h]) for h in range(N_HEADS_SPARSE)]
    _flash_chains(chains, k_ref, v_ref, dmat_ref, None, 0, n_chunks, scratch)
    _merge_head_pairs(o_ref, scratch[-2], scratch[-1], n_tiles)


def _dsa_attention(ha, hb, batch, seq):
    n = hb.shape[0]
    nq = seq // TQ
    k_sel = min(TOPK_LIMIT, seq // 4)
    width = N_HEADS_SPARSE * D_HEAD
    iq_w = N_IDX_HEADS * D_IDX
    return pl.pallas_call(
        functools.partial(_dsa_kernel, k_sel=k_sel),
        grid=(batch, nq),
        in_specs=[
            pl.BlockSpec((TQ, width), lambda b, i: (b * nq + i, 0)),
            pl.BlockSpec((seq, width), lambda b, i: (b, 1)),
            pl.BlockSpec((seq, width), lambda b, i: (b, 2)),
            pl.BlockSpec((TQ, iq_w), lambda b, i: (b * nq + i, 3 * width // iq_w)),
            pl.BlockSpec((seq, LANES), lambda b, i: (b, (3 * width + iq_w) // LANES)),
            pl.BlockSpec((TQ, LANES), lambda b, i: (b * nq + i, 3 * CONV_W // LANES)),
        ],
        out_specs=pl.BlockSpec((TQ, width), lambda b, i: (b * nq + i, 0)),
        out_shape=jax.ShapeDtypeStruct((n, width), BF16),
        scratch_shapes=[pltpu.VMEM((TQ, seq), F32), pltpu.VMEM((seq, TQ), F32), pltpu.VMEM((TQ, seq), F32)]
        + _flash_scratch(N_HEADS_SPARSE, seq),
        compiler_params=_cparams("arbitrary", "arbitrary"),
        name="dsa_attention",
    )(hb, hb, hb, hb, hb, ha)


def _diff_kernel(q_ref, k_ref, v_ref, lam_ref, g_ref, o_ref, dmat_ref, *scratch, lam_init):
    i = pl.program_id(1)
    t0 = i * TQ
    n_chunks = (t0 + TQ + KC - 1) // KC
    row = t0 + lax.broadcasted_iota(I32, (TQ, KC), 0)
    col_in_chunk = lax.broadcasted_iota(I32, (TQ, KC), 1)

    def mask_body(c, _):
        k0 = pl.multiple_of(c * KC, KC)
        dist = row - (k0 + col_in_chunk)
        dmat_ref[:, pl.ds(k0, KC)] = jnp.where(dist >= 0, dist.astype(F32), jnp.inf)
        return 0

    lax.fori_loop(0, n_chunks, mask_body, 0)

    lv = lam_ref[...]
    lam = (jnp.exp(jnp.sum(lv[0:1, :] * lv[1:2, :], axis=1, keepdims=True))
           - jnp.exp(jnp.sum(lv[2:3, :] * lv[3:4, :], axis=1, keepdims=True)) + lam_init)

    slopes = _alibi_slopes(N_HEADS_DIFF)
    _split_head_pairs(q_ref, scratch[0], N_HEADS_DIFF, D_HEAD ** -0.5)
    chains = [(n // 2, n // 2, slopes[n // 2]) for n in range(2 * N_HEADS_DIFF)]
    _flash_chains(chains, k_ref, v_ref, dmat_ref, None, 0, n_chunks, scratch)
    l_ref, acc_ref = scratch[-2], scratch[-1]
    g = g_ref[...]
    for h in range(N_HEADS_DIFF):
        of = acc_ref[2 * h] / l_ref[2 * h] - lam * (acc_ref[2 * h + 1] / l_ref[2 * h + 1])
        of = of * lax.rsqrt(jnp.mean(of * of, axis=1, keepdims=True) + RMS_EPS) * g
        o_ref[:, h * LANES:(h + 1) * LANES] = (of * (1.0 - lam_init)).astype(o_ref.dtype)


def _diff_attention(hb, lam_vecs, subln_g, batch, seq, layer):
    n = hb.shape[0]
    nq = seq // TQ
    width = N_HEADS_DIFF * 2 * D_HEAD
    lam_init = 0.8 - 0.6 * math.exp(-0.3 * layer)
    return pl.pallas_call(
        functools.partial(_diff_kernel, lam_init=lam_init),
        grid=(batch, nq),
        in_specs=[
            pl.BlockSpec((TQ, width), lambda b, i: (b * nq + i, 0)),
            pl.BlockSpec((seq, width), lambda b, i: (b, 1)),
            pl.BlockSpec((seq, width), lambda b, i: (b, 2)),
            pl.BlockSpec(lam_vecs.shape, lambda b, i: (0, 0)),
            pl.BlockSpec(subln_g.shape, lambda b, i: (0, 0)),
        ],
        out_specs=pl.BlockSpec((TQ, width), lambda b, i: (b * nq + i, 0)),
        out_shape=jax.ShapeDtypeStruct((n, width), BF16),
        scratch_shapes=[pltpu.VMEM((TQ, seq), F32)] + _flash_scratch(2 * N_HEADS_DIFF, seq),
        compiler_params=_cparams("arbitrary", "arbitrary"),
        name="diff_attention",
    )(hb, hb, hb, lam_vecs, subln_g)


def _dilated_kernel(q_ref, k_ref, v_ref, o_ref, dmat_ref, amat_ref, *scratch):
    i = pl.program_id(1)
    t0 = i * TQ
    w_max = max(w for w, _ in DIL_GROUPS)
    c_lo = jnp.maximum(t0 - w_max, 0) // KC
    c_hi = (t0 + TQ + KC - 1) // KC
    row = t0 + lax.broadcasted_iota(I32, (TQ, KC), 0)
    col_in_chunk = lax.broadcasted_iota(I32, (TQ, KC), 1)

    def mask_body(c, _):
        k0 = pl.multiple_of(c * KC, KC)
        dist = row - (k0 + col_in_chunk)
        mult = jnp.zeros((TQ, KC), F32)
        for w, d in DIL_GROUPS:
            member = (dist >= 0) & (dist <= w) & ((dist & (d - 1)) == 0)
            mult = mult + jnp.where(member, 1.0, 0.0)
        on = mult > 0.0
        dmat_ref[:, pl.ds(k0, KC)] = jnp.where(on, dist.astype(F32), jnp.inf)
        amat_ref[:, pl.ds(k0, KC)] = jnp.log2(jnp.where(on, mult, 1.0))
        return 0

    lax.fori_loop(c_lo, c_hi, mask_body, 0)

    slopes = _alibi_slopes(N_HEADS_DIL)
    n_tiles = N_HEADS_DIL * D_HEAD // LANES
    _split_head_pairs(q_ref, scratch[0], n_tiles, D_HEAD ** -0.5)
    chains = [(h // 2, h // 2, slopes[h]) for h in range(N_HEADS_DIL)]
    _flash_chains(chains, k_ref, v_ref, dmat_ref, amat_ref, c_lo, c_hi, scratch)
    _merge_head_pairs(o_ref, scratch[-2], scratch[-1], n_tiles)


def _dilated_attention(hb, batch, seq):
    n = hb.shape[0]
    nq = seq // TQ
    width = N_HEADS_DIL * D_HEAD
    return pl.pallas_call(
        _dilated_kernel,
        grid=(batch, nq),
        in_specs=[
            pl.BlockSpec((TQ, width), lambda b, i: (b * nq + i, 3)),
            pl.BlockSpec((seq, width), lambda b, i: (b, 4)),
            pl.BlockSpec((seq, width), lambda b, i: (b, 5)),
        ],
        out_specs=pl.BlockSpec((TQ, width), lambda b, i: (b * nq + i, 0)),
        out_shape=jax.ShapeDtypeStruct((n, width), BF16),
        scratch_shapes=[pltpu.VMEM((TQ, seq), F32), pltpu.VMEM((TQ, seq), F32)] + _flash_scratch(N_HEADS_DIL, seq),
        compiler_params=_cparams("arbitrary", "arbitrary"),
        name="dilated_attention",
    )(hb, hb, hb)


def _mix_out_kernel(ya_ref, yb_ref, wa_ref, wb_ref, x_ref, g_ref, b_ref, rw_ref, rb_ref,
                    x1_ref, x1t_ref, eidx_ref, gate_ref, rank_ref, cnt_ref, carry_ref):
    step = pl.program_id(0)

    @pl.when(step == 0)
    def _():
        carry_ref[...] = jnp.zeros_like(carry_ref)

    m = (jnp.dot(ya_ref[...], wa_ref[...], preferred_element_type=F32)
         + jnp.dot(yb_ref[...], wb_ref[...], preferred_element_type=F32))
    x1 = _layer_norm(DEEPNORM_ALPHA * x_ref[...] + m, g_ref[...], b_ref[...])
    x1_ref[...] = x1
    _to_token_tiles(x1t_ref, x1)

    t = x1.shape[0]
    logits = jnp.dot(x1.astype(BF16), rw_ref[...], preferred_element_type=F32) + rb_ref[...]
    lane = lax.broadcasted_iota(I32, (t, LANES), 1)
    vals, idxs = [], []
    lg = logits
    for _ in range(TOP_K):
        mx = jnp.max(lg, axis=1, keepdims=True)
        ix = jnp.min(jnp.where(lg == mx, lane, LANES), axis=1, keepdims=True)
        vals.append(mx)
        idxs.append(ix)
        lg = jnp.where(lane == ix, -jnp.inf, lg)
    exps = [jnp.exp(v - vals[0]) for v in vals]
    denom = exps[0]
    for e in exps[1:]:
        denom = denom + e

    onehot = jnp.zeros((t, LANES), F32)
    for ix in idxs:
        onehot = onehot + jnp.where(lane == ix, 1.0, 0.0)
    strict = (lax.broadcasted_iota(I32, (t, t), 1) < lax.broadcasted_iota(I32, (t, t), 0)).astype(BF16)
    before = jnp.dot(strict, onehot.astype(BF16), preferred_element_type=F32) + carry_ref[...]

    eidx = jnp.zeros((t, LANES), I32)
    gate = jnp.zeros((t, LANES), F32)
    rank = jnp.zeros((t, LANES), I32)
    for k in range(TOP_K):
        rk = jnp.sum(jnp.where(lane == idxs[k], before, 0.0), axis=1, keepdims=True)
        eidx = jnp.where(lane == k, idxs[k], eidx)
        gate = jnp.where(lane == k, exps[k] / denom, gate)
        rank = jnp.where(lane == k, rk.astype(I32), rank)
    eidx_ref[...] = eidx
    gate_ref[...] = gate
    rank_ref[...] = rank
    carry_ref[...] = carry_ref[...] + jnp.sum(onehot, axis=0, keepdims=True)
    cnt_ref[...] = carry_ref[...]


def _mix_out(ya, yb, w_out, x2d, ln_g, ln_b, router_w, router_b):
    n, d = x2d.shape
    half = ya.shape[1]
    wa = w_out[:half].astype(BF16)
    wb = w_out[half:].astype(BF16)
    rw = jnp.zeros((d, LANES), BF16).at[:, :N_EXPERTS].set(router_w.astype(BF16))
    rb = jnp.full((1, LANES), -jnp.inf, F32).at[0, :N_EXPERTS].set(router_b)
    t = ROW_TILE
    full = lambda a: pl.BlockSpec(a.shape, lambda i: (0, 0))
    rows = lambda w: pl.BlockSpec((t, w), lambda i: (i, 0))
    g2, b2 = ln_g.reshape(1, d), ln_b.reshape(1, d)
    return pl.pallas_call(
        _mix_out_kernel,
        grid=(n // t,),
        in_specs=[rows(half), rows(half), full(wa), full(wb), rows(d), full(g2), full(b2), full(rw), full(rb)],
        out_specs=[rows(d), pl.BlockSpec((t * ROW_SUBLANES, LANES), lambda i: (i, 0)),
                   rows(LANES), rows(LANES), rows(LANES), pl.BlockSpec((1, LANES), lambda i: (0, 0))],
        out_shape=[jax.ShapeDtypeStruct((n, d), F32), jax.ShapeDtypeStruct((n * ROW_SUBLANES, LANES), F32),
                   jax.ShapeDtypeStruct((n, LANES), I32),
                   jax.ShapeDtypeStruct((n, LANES), F32), jax.ShapeDtypeStruct((n, LANES), I32),
                   jax.ShapeDtypeStruct((1, LANES), F32)],
        scratch_shapes=[pltpu.VMEM((1, LANES), F32)],
        compiler_params=_cparams("arbitrary"),
        name="mix_out_router",
    )(ya, yb, wa, wb, x2d, g2, b2, rw, rb)


ROW_SUBLANES = 8


def _to_token_tiles(o_ref, x):
    for j in range(ROW_SUBLANES):
        o_ref[pl.ds(j, x.shape[0], stride=ROW_SUBLANES), :] = x[:, j * LANES:(j + 1) * LANES]


def _from_token_tiles(buf_ref, n_rows):
    return [buf_ref[pl.ds(j, n_rows, stride=ROW_SUBLANES), :] for j in range(ROW_SUBLANES)]


def _row_copy(src_hbm, dst_ref, src_row, dst_row, sem):
    src = src_hbm.at[pl.ds(pl.multiple_of(src_row * ROW_SUBLANES, ROW_SUBLANES), ROW_SUBLANES)]
    dst = dst_ref.at[pl.ds(pl.multiple_of(dst_row * ROW_SUBLANES, ROW_SUBLANES), ROW_SUBLANES)]
    return pltpu.make_async_copy(src, dst, sem)


def _rows_wait(src_hbm, dst_ref, sem):
    pltpu.make_async_copy(src_hbm.at[pl.ds(0, dst_ref.shape[0])], dst_ref, sem).wait()


def _expert_kernel(blk_e_ref, n_used_ref, run_start_ref, run_parity_ref, next_e_ref, row_tok_ref,
                   x_hbm, wgu_hbm, bgu_ref, wdn_hbm, bdn_ref,
                   o_ref, xbuf, wgu_raw, wdn_raw, wgu_bf, wdn_bf, sem, wsem):
    i = pl.program_id(0)
    n_used = n_used_ref[0]
    slot = i % 2
    e = blk_e_ref[i]
    parity = run_parity_ref[i]

    def weight_copies(expert, p):
        return (pltpu.make_async_copy(wgu_hbm.at[expert], wgu_raw.at[p], wsem.at[p, 0]),
                pltpu.make_async_copy(wdn_hbm.at[expert], wdn_raw.at[p], wsem.at[p, 1]))

    def gather(block, s):
        base = block * MOE_BM

        def body(g, _):
            for u in range(ISSUE_UNROLL):
                r = g * ISSUE_UNROLL + u
                _row_copy(x_hbm, xbuf.at[s], row_tok_ref[base + r], r, sem.at[s]).start(priority=u % 2)
            return 0
        lax.fori_loop(0, MOE_BM // ISSUE_UNROLL, body, 0)

    @pl.when(i == 0)
    def _():
        gather(0, 0)

        @pl.when(run_start_ref[0] == 1)
        def _():
            for c in weight_copies(e, parity):
                c.start()

    @pl.when(run_start_ref[i] == 1)
    def _():
        for c in weight_copies(e, parity):
            c.wait()
        wgu_bf[...] = wgu_raw[parity].astype(BF16)
        wdn_bf[...] = wdn_raw[parity].astype(BF16)
        e_next = next_e_ref[i]

        @pl.when(e_next >= 0)
        def _():
            for c in weight_copies(e_next, 1 - parity):
                c.start()

    gather(jnp.minimum(i + 1, pl.num_programs(0) - 1), 1 - slot)
    _rows_wait(x_hbm, xbuf.at[slot], sem.at[slot])

    @pl.when(i < n_used)
    def _():
        x = jnp.concatenate(_from_token_tiles(xbuf.at[slot], MOE_BM), axis=1).astype(BF16)
        h = jnp.dot(x, wgu_bf[...], preferred_element_type=F32) + bgu_ref[...]
        gate = jnp.minimum(h[:, :D_FF], SWIGLU_LIMIT)
        up = jnp.clip(h[:, D_FF:], -SWIGLU_LIMIT, SWIGLU_LIMIT)
        glu = gate * (1.0 / (1.0 + jnp.exp(-SWIGLU_ALPHA * gate)))
        act = ((up + 1.0) * glu).astype(BF16)
        _to_token_tiles(o_ref, jnp.dot(act, wdn_bf[...], preferred_element_type=F32) + bdn_ref[...])

    @pl.when(i >= n_used)
    def _():
        o_ref[...] = jnp.zeros_like(o_ref)

    @pl.when(i == pl.num_programs(0) - 1)
    def _():
        _rows_wait(x_hbm, xbuf.at[1 - slot], sem.at[1 - slot])


def _expert_ffn(x_tiles, row_tok, blk_e, n_used, w_gu, b_gu, w_dn, b_dn):
    n_rows = row_tok.shape[0]
    d = w_gu.shape[1]
    n_blocks = n_rows // MOE_BM
    ne = w_gu.shape[0]
    blk = jnp.arange(n_blocks, dtype=I32)
    prev_e = jnp.concatenate([blk_e[:1], blk_e[:-1]])
    run_start = (blk < n_used[0]) & ((blk == 0) | (blk_e != prev_e))
    run_parity = ((jnp.cumsum(run_start.astype(I32)) - 1) % 2).astype(I32)
    first_start_from = lax.cummin(jnp.where(run_start, blk, n_blocks)[::-1])[::-1]
    next_start = jnp.concatenate([first_start_from[1:], jnp.full((1,), n_blocks, I32)])
    next_e = jnp.where(next_start < n_blocks, blk_e[jnp.minimum(next_start, n_blocks - 1)], -1).astype(I32)
    idx = lambda f: (lambda i, *prefetch: f(i, prefetch[0]))
    grid_spec = pltpu.PrefetchScalarGridSpec(
        num_scalar_prefetch=6,
        grid=(n_blocks,),
        in_specs=[
            pl.BlockSpec(memory_space=pl.ANY),
            pl.BlockSpec(memory_space=pl.ANY),
            pl.BlockSpec((None, 1, 2 * D_FF), idx(lambda i, be: (be[i], 0, 0))),
            pl.BlockSpec(memory_space=pl.ANY),
            pl.BlockSpec((None, 1, d), idx(lambda i, be: (be[i], 0, 0))),
        ],
        out_specs=pl.BlockSpec((MOE_BM * ROW_SUBLANES, LANES), idx(lambda i, be: (i, 0))),
        scratch_shapes=[pltpu.VMEM((2, MOE_BM * ROW_SUBLANES, LANES), F32),
                        pltpu.VMEM((2, d, 2 * D_FF), F32), pltpu.VMEM((2, D_FF, d), F32),
                        pltpu.VMEM((d, 2 * D_FF), BF16), pltpu.VMEM((D_FF, d), BF16),
                        pltpu.SemaphoreType.DMA((2,)), pltpu.SemaphoreType.DMA((2, 2))],
    )
    return pl.pallas_call(
        _expert_kernel,
        grid_spec=grid_spec,
        out_shape=jax.ShapeDtypeStruct((n_rows * ROW_SUBLANES, LANES), F32),
        compiler_params=_cparams("arbitrary"),
        name="moe_experts",
    )(blk_e, n_used, run_start.astype(I32), run_parity, next_e, row_tok,
      x_tiles, w_gu, b_gu.reshape(ne, 1, 2 * D_FF), w_dn, b_dn.reshape(ne, 1, d))


def _combine_kernel(dest_ref, y_hbm, gate_ref, x_ref, g_ref, b_ref, o_ref, buf_ref, sem):
    i = pl.program_id(0)
    t = x_ref.shape[0]
    slot = i % 2

    def gather(step, s):
        base = step * (t * TOP_K)

        def body(g, _):
            for u in range(ISSUE_UNROLL // TOP_K):
                r = g * (ISSUE_UNROLL // TOP_K) + u
                for k in range(TOP_K):
                    row = dest_ref[base + r * TOP_K + k]
                    _row_copy(y_hbm, buf_ref.at[s, k], row, r, sem.at[s]).start(priority=k % 2)
            return 0
        lax.fori_loop(0, t * TOP_K // ISSUE_UNROLL, body, 0)

    @pl.when(i == 0)
    def _():
        gather(0, 0)

    @pl.when(i + 1 < pl.num_programs(0))
    def _():
        gather(i + 1, 1 - slot)

    for k in range(TOP_K):
        _rows_wait(y_hbm, buf_ref.at[slot, k], sem.at[slot])
    gate = gate_ref[...]
    f_tiles = None
    for k in range(TOP_K):
        y_tiles = [gate[:, k:k + 1] * yt for yt in _from_token_tiles(buf_ref.at[slot, k], t)]
        f_tiles = y_tiles if f_tiles is None else [a + b for a, b in zip(f_tiles, y_tiles)]
    f = jnp.concatenate(f_tiles, axis=1)
    o_ref[...] = _layer_norm(DEEPNORM_ALPHA * x_ref[...] + f, g_ref[...], b_ref[...])


def _combine(ys, dest_flat, gates, x2d, ln_g, ln_b):
    n, d = x2d.shape
    t = GATHER_T
    g2, b2 = ln_g.reshape(1, d), ln_b.reshape(1, d)
    full = lambda a: pl.BlockSpec(a.shape, lambda i, dest: (0, 0))
    grid_spec = pltpu.PrefetchScalarGridSpec(
        num_scalar_prefetch=1,
        grid=(n // t,),
        in_specs=[pl.BlockSpec(memory_space=pl.ANY),
                  pl.BlockSpec((t, LANES), lambda i, dest: (i, 0)),
                  pl.BlockSpec((t, d), lambda i, dest: (i, 0)),
                  full(g2), full(b2)],
        out_specs=pl.BlockSpec((t, d), lambda i, dest: (i, 0)),
        scratch_shapes=[pltpu.VMEM((2, TOP_K, t * ROW_SUBLANES, LANES), F32), pltpu.SemaphoreType.DMA((2,))],
    )
    return pl.pallas_call(
        _combine_kernel,
        grid_spec=grid_spec,
        out_shape=jax.ShapeDtypeStruct((n, d), F32),
        compiler_params=_cparams("arbitrary"),
        name="moe_combine",
    )(dest_flat, ys, gates, x2d, g2, b2)


def _row_token_kernel(dest_ref, pad_lo_ref, pad_hi_ref, o_ref):
    def clear_segment(e, _):
        def clear(r, _):
            o_ref[r] = 0
            return 0
        lax.fori_loop(pad_lo_ref[e], pad_hi_ref[e], clear, 0)
        return 0

    lax.fori_loop(0, pad_lo_ref.shape[0], clear_segment, 0)

    def put(g, _):
        base = g * ISSUE_UNROLL
        rows = [dest_ref[base + u] for u in range(ISSUE_UNROLL)]
        tok0 = g * (ISSUE_UNROLL // TOP_K)
        for u in range(ISSUE_UNROLL):
            o_ref[rows[u]] = tok0 + u // TOP_K
        return 0

    lax.fori_loop(0, dest_ref.shape[0] // ISSUE_UNROLL, put, 0)


def _row_tokens(dest_flat, pad_lo, pad_hi, n_rows):
    smem = pl.BlockSpec(memory_space=pltpu.SMEM)
    return pl.pallas_call(
        _row_token_kernel,
        in_specs=[smem, smem, smem],
        out_specs=smem,
        out_shape=jax.ShapeDtypeStruct((n_rows,), I32),
        name="moe_row_tokens",
    )(dest_flat, pad_lo, pad_hi)


def _moe_block(x1, x1_tiles, eidx, gates, rank, counts, w_gu, b_gu, w_dn, b_dn, ln_g, ln_b):
    n = x1.shape[0]
    e_sel = eidx[:, :TOP_K]
    cnt = counts[0, :N_EXPERTS].astype(I32)
    padded = ((cnt + MOE_BM - 1) // MOE_BM) * MOE_BM
    pad_end = jnp.cumsum(padded)
    pad_start = pad_end - padded
    expert_ids = jnp.arange(N_EXPERTS, dtype=I32)
    dest = jnp.sum(jnp.where(e_sel[..., None] == expert_ids, pad_start, 0), axis=-1) + rank[:, :TOP_K]
    n_rows = ((n * TOP_K + N_EXPERTS * (MOE_BM - 1) + MOE_BM - 1) // MOE_BM) * MOE_BM
    n_blocks = n_rows // MOE_BM
    dest_flat = dest.reshape(-1)
    pad_lo = jnp.concatenate([pad_start + cnt, pad_end[-1:]]).astype(I32)
    pad_hi = jnp.concatenate([pad_end, jnp.full((1,), n_rows)]).astype(I32)
    row_tok = _row_tokens(dest_flat, pad_lo, pad_hi, n_rows)
    blk_start = jnp.arange(n_blocks, dtype=I32) * MOE_BM
    blk_e = jnp.minimum(jnp.sum((pad_end[None, :] <= blk_start[:, None]).astype(I32), axis=1), N_EXPERTS - 1)
    n_used = (pad_end[-1:] // MOE_BM).astype(I32)
    ys = _expert_ffn(x1_tiles, row_tok, blk_e, n_used, w_gu, b_gu, w_dn, b_dn)
    return _combine(ys, dest_flat, gates, x1, ln_g, ln_b)


def _pad_cols(w, width):
    return jnp.pad(w, ((0, 0), (0, width - w.shape[1])))


def kernel(x, w_in_0, conv_w_0, w_out_0, ln_mix_g_0, ln_mix_b_0, router_w_0, router_b_0, w_gu_0, b_gu_0, w_dn_0, b_dn_0, ln_ffn_g_0, ln_ffn_b_0, w_in_1, lam_q1_1, lam_k1_1, lam_q2_1, lam_k2_1, subln_g_1, w_out_1, ln_mix_g_1, ln_mix_b_1, router_w_1, router_b_1, w_gu_1, b_gu_1, w_dn_1, b_dn_1, ln_ffn_g_1, ln_ffn_b_1):
    batch, seq, d = x.shape
    x0 = x.reshape(batch * seq, d)

    n_f32 = 3 * CONV_W
    n_attn = 3 * N_HEADS_SPARSE * D_HEAD + N_IDX_HEADS * D_IDX + D_IDX
    w_a = _pad_cols(jnp.concatenate([w_in_0[:, :n_f32], w_in_0[:, n_f32 + n_attn:]], axis=1),
                    n_f32 + LANES)
    w_b = _pad_cols(w_in_0[:, n_f32:n_f32 + n_attn], 3 * N_HEADS_SPARSE * D_HEAD + N_IDX_HEADS * D_IDX + LANES)
    ha, hb = _project(x0, [w_a.astype(BF16), w_b.astype(BF16)], [F32, BF16])
    ya = _short_conv(ha, conv_w_0, batch, seq)
    yb = _dsa_attention(ha, hb, batch, seq)
    x1, x1t, eidx, gates, rank, counts = _mix_out(ya, yb, w_out_0, x0, ln_mix_g_0, ln_mix_b_0, router_w_0, router_b_0)
    x2 = _moe_block(x1, x1t, eidx, gates, rank, counts, w_gu_0, b_gu_0, w_dn_0, b_dn_0, ln_ffn_g_0, ln_ffn_b_0)

    (hc,) = _project(x2, [w_in_1.astype(BF16)], [BF16])
    lam_vecs = jnp.stack([lam_q1_1, lam_k1_1, lam_q2_1, lam_k2_1]).astype(F32)
    yc = _diff_attention(hc, lam_vecs, subln_g_1.reshape(1, -1).astype(F32), batch, seq, 1)
    yd = _dilated_attention(hc, batch, seq)
    x3, x3t, eidx, gates, rank, counts = _mix_out(yc, yd, w_out_1, x2, ln_mix_g_1, ln_mix_b_1, router_w_1, router_b_1)
    x4 = _moe_block(x3, x3t, eidx, gates, rank, counts, w_gu_1, b_gu_1, w_dn_1, b_dn_1, ln_ffn_g_1, ln_ffn_b_1)
    return x4.reshape(batch, seq, d)
```

```python
import functools
import math

import jax
import jax.numpy as jnp
from jax import lax
from jax.experimental import pallas as pl
from jax.experimental.pallas import tpu as pltpu

F32 = jnp.float32
BF16 = jnp.bfloat16
I32 = jnp.int32

CONV_W = 512
CONV_TAPS = 3
N_HEADS_SPARSE = 8
D_HEAD = 64
N_IDX_HEADS = 8
D_IDX = 32
TOPK_LIMIT = 256
N_HEADS_DIFF = 4
N_HEADS_DIL = 8
DIL_GROUPS = ((128, 1), (512, 4), (2048, 16))
N_EXPERTS = 32
TOP_K = 4
D_FF = 1024
SWIGLU_LIMIT = 7.0
SWIGLU_ALPHA = 1.702
DEPTH = 2
DEEPNORM_ALPHA = (2 * DEPTH) ** 0.25
LN_EPS = 1e-5
RMS_EPS = 1e-5

LANES = 128
VMEM_LIMIT = 56 * 1024 * 1024
TQ = 256
CHAIN_GROUP = 4
KC = 512
ROW_TILE = 256
MOE_BM = 256
GATHER_T = 128
ISSUE_UNROLL = 8
COUNT_ROWS = 64
LOG2_E = math.log2(math.e)
INT_MIN = -2 ** 31


def _alibi_slopes(n):
    return [2.0 ** (-8.0 * (h + 1) / n) for h in range(n)]


def _cparams(*sem):
    return pltpu.CompilerParams(dimension_semantics=sem, vmem_limit_bytes=VMEM_LIMIT)


def _layer_norm(z, g, b):
    mu = jnp.mean(z, axis=-1, keepdims=True)
    zc = z - mu
    var = jnp.mean(zc * zc, axis=-1, keepdims=True)
    return zc * lax.rsqrt(var + LN_EPS) * g + b


def _proj_kernel(*refs, n_out):
    x_ref = refs[0]
    w_refs = refs[1:1 + n_out]
    o_refs = refs[1 + n_out:]
    xb = x_ref[...].astype(BF16)
    for w_ref, o_ref in zip(w_refs, o_refs):
        o_ref[...] = jnp.dot(xb, w_ref[...], preferred_element_type=F32).astype(o_ref.dtype)


def _project(x2d, weights, out_dtypes):
    n, d = x2d.shape
    n_out = len(weights)
    in_specs = [pl.BlockSpec((ROW_TILE, d), lambda i: (i, 0))]
    in_specs += [pl.BlockSpec(w.shape, lambda i: (0, 0)) for w in weights]
    out_specs = [pl.BlockSpec((ROW_TILE, w.shape[1]), lambda i: (i, 0)) for w in weights]
    out_shape = [jax.ShapeDtypeStruct((n, w.shape[1]), dt) for w, dt in zip(weights, out_dtypes)]
    return pl.pallas_call(
        functools.partial(_proj_kernel, n_out=n_out),
        grid=(n // ROW_TILE,),
        in_specs=in_specs, out_specs=out_specs, out_shape=out_shape,
        compiler_params=_cparams("parallel"),
        name="in_proj",
    )(x2d, *weights)


def _conv_kernel(gb_ref, gc_ref, xa_ref, w_ref, o_ref, prev_ref):
    j = pl.program_id(1)
    t = gb_ref.shape[0]

    @pl.when(j == 0)
    def _():
        prev_ref[...] = jnp.zeros_like(prev_ref)

    z = gc_ref[...] * xa_ref[...]
    row = lax.broadcasted_iota(I32, z.shape, 0)
    prev2 = prev_ref[0:1, :]
    prev1 = prev_ref[1:2, :]
    z1 = jnp.where(row == 0, prev1, pltpu.roll(z, 1, 0))
    z2 = jnp.where(row == 0, prev2, jnp.where(row == 1, prev1, pltpu.roll(z, 2, 0)))
    w = w_ref[...]
    y = w[0:1, :] * z + w[1:2, :] * z1 + w[2:3, :] * z2
    o_ref[...] = (gb_ref[...] * y).astype(o_ref.dtype)
    prev_ref[0:1, :] = z[t - 2:t - 1, :]
    prev_ref[1:2, :] = z[t - 1:t, :]


def _short_conv(ha, conv_w, batch, seq):
    n = ha.shape[0]
    t = min(512, seq)
    nj = seq // t
    spec = lambda c: pl.BlockSpec((t, CONV_W), lambda b, j, c=c: (b * nj + j, c))
    return pl.pallas_call(
        _conv_kernel,
        grid=(batch, nj),
        in_specs=[spec(0), spec(1), spec(2), pl.BlockSpec((CONV_TAPS, CONV_W), lambda b, j: (0, 0))],
        out_specs=pl.BlockSpec((t, CONV_W), lambda b, j: (b * nj + j, 0)),
        out_shape=jax.ShapeDtypeStruct((n, CONV_W), BF16),
        scratch_shapes=[pltpu.VMEM((8, CONV_W), F32)],
        compiler_params=_cparams("arbitrary", "arbitrary"),
        name="short_conv",
    )(ha, ha, ha, conv_w)


def _lane_tiles(x):
    return [x[:, u * LANES:(u + 1) * LANES] for u in range(x.shape[1] // LANES)]


def _flash_chains(chains, k_ref, v_ref, dmat_ref, amat_ref, c_lo, c_hi, scratch):
    qm_ref, s_ref, mx_ref, lp_ref, l_ref, acc_ref = scratch
    mx_ref[...] = jnp.full(mx_ref.shape, -jnp.inf, F32)
    lp_ref[...] = jnp.zeros(lp_ref.shape, F32)
    acc_ref[...] = jnp.zeros(acc_ref.shape, F32)

    for g0 in range(0, len(chains), CHAIN_GROUP):
        group = list(enumerate(chains))[g0:g0 + CHAIN_GROUP]

        def pass_a(c, _, group=group, g0=g0):
            k0 = pl.multiple_of(c * KC, KC)
            for n, (k_tile, _, slope) in group:
                kc = k_ref[pl.ds(k0, KC), k_tile * LANES:(k_tile + 1) * LANES]
                s = lax.dot_general(qm_ref[n], kc, (((1,), (1,)), ((), ())), preferred_element_type=F32)
                s = s * LOG2_E - (slope * LOG2_E) * dmat_ref[:, pl.ds(k0, KC)]
                if amat_ref is not None:
                    s = s + amat_ref[:, pl.ds(k0, KC)]
                s_ref[n - g0, :, pl.ds(k0, KC)] = s
                part = mx_ref[n]
                for t in _lane_tiles(s):
                    part = jnp.maximum(part, t)
                mx_ref[n] = part
            return 0

        lax.fori_loop(c_lo, c_hi, pass_a, 0)

        for n, _ in group:
            row_max = jnp.max(mx_ref[n], axis=1, keepdims=True)
            mx_ref[n] = jnp.broadcast_to(row_max, (TQ, LANES))

        def pass_b(c, _, group=group, g0=g0):
            k0 = pl.multiple_of(c * KC, KC)
            for n, (_, v_tile, _) in group:
                row_max = mx_ref[n]
                p_tiles = [jnp.exp2(t - row_max) for t in _lane_tiles(s_ref[n - g0, :, pl.ds(k0, KC)])]
                part = lp_ref[n]
                for t in p_tiles:
                    part = part + t
                lp_ref[n] = part
                p = jnp.concatenate(p_tiles, axis=1).astype(BF16)
                vc = v_ref[pl.ds(k0, KC), v_tile * LANES:(v_tile + 1) * LANES]
                acc_ref[n] = acc_ref[n] + jnp.dot(p, vc, preferred_element_type=F32)
            return 0

        lax.fori_loop(c_lo, c_hi, pass_b, 0)

    for n in range(len(chains)):
        l_ref[n] = jnp.sum(lp_ref[n], axis=1, keepdims=True)


def _split_head_pairs(q_ref, qm_ref, n_tiles, scale):
    low = lax.broadcasted_iota(I32, (TQ, LANES), 1) < D_HEAD
    for j in range(n_tiles):
        qt = q_ref[:, j * LANES:(j + 1) * LANES] * scale
        qm_ref[2 * j] = jnp.where(low, qt, 0).astype(BF16)
        qm_ref[2 * j + 1] = jnp.where(low, 0, qt).astype(BF16)


def _merge_head_pairs(o_ref, l_ref, acc_ref, n_tiles):
    low = lax.broadcasted_iota(I32, (TQ, LANES), 1) < D_HEAD
    for j in range(n_tiles):
        out = jnp.where(low, acc_ref[2 * j] / l_ref[2 * j], acc_ref[2 * j + 1] / l_ref[2 * j + 1])
        o_ref[:, j * LANES:(j + 1) * LANES] = out.astype(o_ref.dtype)


def _flash_scratch(n_chains, seq):
    return [pltpu.VMEM((n_chains, TQ, LANES), BF16), pltpu.VMEM((CHAIN_GROUP, TQ, seq), F32),
            pltpu.VMEM((n_chains, TQ, LANES), F32), pltpu.VMEM((n_chains, TQ, LANES), F32),
            pltpu.VMEM((n_chains, TQ, 1), F32), pltpu.VMEM((n_chains, TQ, LANES), F32)]


def _dsa_kernel(q_ref, k_ref, v_ref, iq_ref, ik_ref, iw_ref, o_ref, sc_ref, sct_ref, dmat_ref, *scratch, k_sel):
    i = pl.program_id(1)
    seq_len = sc_ref.shape[1]
    t0 = i * TQ
    n_chunks = (t0 + TQ + KC - 1) // KC
    idx_scale = (D_IDX ** -0.5) * (N_IDX_HEADS ** -0.5)
    row = t0 + lax.broadcasted_iota(I32, (TQ, KC), 0)
    col_in_chunk = lax.broadcasted_iota(I32, (TQ, KC), 1)

    iq = iq_ref[...]
    iw = iw_ref[:, 0:N_IDX_HEADS]

    def score_body(c, _):
        k0 = pl.multiple_of(c * KC, KC)
        ik = ik_ref[pl.ds(k0, KC), 0:D_IDX]
        sc = jnp.zeros((TQ, KC), F32)
        for h in range(N_IDX_HEADS):
            rel = lax.dot_general(iq[:, h * D_IDX:(h + 1) * D_IDX], ik,
                                  (((1,), (1,)), ((), ())), preferred_element_type=F32)
            sc = sc + iw[:, h:h + 1] * jnp.maximum(rel, 0.0)
        causal = (k0 + col_in_chunk) <= row
        sc = jnp.where(causal, sc * idx_scale, -jnp.inf)
        sc_ref[:, pl.ds(k0, KC)] = sc
        sct_ref[pl.ds(k0, KC), :] = sc.T
        return 0

    lax.fori_loop(0, n_chunks, score_body, 0)

    def code_to_float(code):
        return lax.bitcast_convert_type(jnp.where(code < 0, code ^ 0x7FFFFFFF, code), F32)

    def count(pred):
        def body(c, acc):
            k0 = pl.multiple_of(c * KC, KC)
            hit = jnp.where(pred(sct_ref[pl.ds(k0, KC), :]), 1.0, 0.0)
            return acc + jnp.sum(hit.reshape(KC // COUNT_ROWS, COUNT_ROWS, TQ), axis=0)
        acc = lax.fori_loop(0, n_chunks, body, jnp.zeros((COUNT_ROWS, TQ), F32))
        return jnp.sum(acc, axis=0, keepdims=True)

    kf = float(k_sel)
    has_k = count(lambda t: t > -jnp.inf) >= kf
    code = jnp.where(count(lambda t: t >= 0.0) >= kf, 0, INT_MIN).astype(I32)

    def bit_body(b, code):
        cand = code + lax.shift_left(jnp.int32(1), 30 - b)
        cand_f = code_to_float(cand)
        return jnp.where(count(lambda t: t >= cand_f) >= kf, cand, code)

    code = lax.fori_loop(0, 31, bit_body, code)
    thr_q = jnp.where(has_k, code_to_float(code), -jnp.inf)
    need_q = jnp.where(has_k, kf - count(lambda t: t > thr_q), float(seq_len))
    thr = jnp.broadcast_to(thr_q, (LANES, TQ)).T
    need = jnp.broadcast_to(need_q, (LANES, TQ)).T

    tri = (lax.broadcasted_iota(I32, (KC, KC), 0) <= lax.broadcasted_iota(I32, (KC, KC), 1)).astype(BF16)
    thr_w = jnp.concatenate([thr] * (KC // LANES), axis=1)
    need_w = jnp.concatenate([need] * (KC // LANES), axis=1)

    def mask_body(c, ties_before):
        k0 = pl.multiple_of(c * KC, KC)
        sc = sc_ref[:, pl.ds(k0, KC)]
        eq = sc == thr_w
        eqf = jnp.where(eq, 1.0, 0.0)
        rank = ties_before + jnp.dot(eqf.astype(BF16), tri, preferred_element_type=F32)
        dist = row - (k0 + col_in_chunk)
        sel = ((sc > thr_w) | (eq & (rank <= need_w))) & (dist >= 0)
        dmat_ref[:, pl.ds(k0, KC)] = jnp.where(sel, dist.astype(F32), jnp.inf)
        return ties_before + jnp.sum(eqf, axis=1, keepdims=True)

    lax.fori_loop(0, n_chunks, mask_body, jnp.zeros((TQ, 1), F32))

    slopes = _alibi_slopes(N_HEADS_SPARSE)
    n_tiles = N_HEADS_SPARSE * D_HEAD // LANES
    _split_head_pairs(q_ref, scratch[0], n_tiles, D_HEAD ** -0.5)
    chains = [(h // 2, h // 2, slopes[h]) for h in range(N_HEADS_SPARSE)]
    _flash_chains(chains, k_ref, v_ref, dmat_ref, None, 0, n_chunks, scratch)
    _merge_head_pairs(o_ref, scratch[-2], scratch[-1], n_tiles)


def _dsa_attention(ha, hb, batch, seq):
    n = hb.shape[0]
    nq = seq // TQ
    k_sel = min(TOPK_LIMIT, seq // 4)
    width = N_HEADS_SPARSE * D_HEAD
    iq_w = N_IDX_HEADS * D_IDX
    return pl.pallas_call(
        functools.partial(_dsa_kernel, k_sel=k_sel),
        grid=(batch, nq),
        in_specs=[
            pl.BlockSpec((TQ, width), lambda b, i: (b * nq + i, 0)),
            pl.BlockSpec((seq, width), lambda b, i: (b, 1)),
            pl.BlockSpec((seq, width), lambda b, i: (b, 2)),
            pl.BlockSpec((TQ, iq_w), lambda b, i: (b * nq + i, 3 * width // iq_w)),
            pl.BlockSpec((seq, LANES), lambda b, i: (b, (3 * width + iq_w) // LANES)),
            pl.BlockSpec((TQ, LANES), lambda b, i: (b * nq + i, 3 * CONV_W // LANES)),
        ],
        out_specs=pl.BlockSpec((TQ, width), lambda b, i: (b * nq + i, 0)),
        out_shape=jax.ShapeDtypeStruct((n, width), BF16),
        scratch_shapes=[pltpu.VMEM((TQ, seq), F32), pltpu.VMEM((seq, TQ), F32), pltpu.VMEM((TQ, seq), F32)]
        + _flash_scratch(N_HEADS_SPARSE, seq),
        compiler_params=_cparams("arbitrary", "arbitrary"),
        name="dsa_attention",
    )(hb, hb, hb, hb, hb, ha)


def _diff_kernel(q_ref, k_ref, v_ref, lam_ref, g_ref, o_ref, dmat_ref, *scratch, lam_init):
    i = pl.program_id(1)
    t0 = i * TQ
    n_chunks = (t0 + TQ + KC - 1) // KC
    row = t0 + lax.broadcasted_iota(I32, (TQ, KC), 0)
    col_in_chunk = lax.broadcasted_iota(I32, (TQ, KC), 1)

    def mask_body(c, _):
        k0 = pl.multiple_of(c * KC, KC)
        dist = row - (k0 + col_in_chunk)
        dmat_ref[:, pl.ds(k0, KC)] = jnp.where(dist >= 0, dist.astype(F32), jnp.inf)
        return 0

    lax.fori_loop(0, n_chunks, mask_body, 0)

    lv = lam_ref[...]
    lam = (jnp.exp(jnp.sum(lv[0:1, :] * lv[1:2, :], axis=1, keepdims=True))
           - jnp.exp(jnp.sum(lv[2:3, :] * lv[3:4, :], axis=1, keepdims=True)) + lam_init)

    slopes = _alibi_slopes(N_HEADS_DIFF)
    _split_head_pairs(q_ref, scratch[0], N_HEADS_DIFF, D_HEAD ** -0.5)
    chains = [(n // 2, n // 2, slopes[n // 2]) for n in range(2 * N_HEADS_DIFF)]
    _flash_chains(chains, k_ref, v_ref, dmat_ref, None, 0, n_chunks, scratch)
    l_ref, acc_ref = scratch[-2], scratch[-1]
    g = g_ref[...]
    for h in range(N_HEADS_DIFF):
        of = acc_ref[2 * h] / l_ref[2 * h] - lam * (acc_ref[2 * h + 1] / l_ref[2 * h + 1])
        of = of * lax.rsqrt(jnp.mean(of * of, axis=1, keepdims=True) + RMS_EPS) * g
        o_ref[:, h * LANES:(h + 1) * LANES] = (of * (1.0 - lam_init)).astype(o_ref.dtype)


def _diff_attention(hb, lam_vecs, subln_g, batch, seq, layer):
    n = hb.shape[0]
    nq = seq // TQ
    width = N_HEADS_DIFF * 2 * D_HEAD
    lam_init = 0.8 - 0.6 * math.exp(-0.3 * layer)
    return pl.pallas_call(
        functools.partial(_diff_kernel, lam_init=lam_init),
        grid=(batch, nq),
        in_specs=[
            pl.BlockSpec((TQ, width), lambda b, i: (b * nq + i, 0)),
            pl.BlockSpec((seq, width), lambda b, i: (b, 1)),
            pl.BlockSpec((seq, width), lambda b, i: (b, 2)),
            pl.BlockSpec(lam_vecs.shape, lambda b, i: (0, 0)),
            pl.BlockSpec(subln_g.shape, lambda b, i: (0, 0)),
        ],
        out_specs=pl.BlockSpec((TQ, width), lambda b, i: (b * nq + i, 0)),
        out_shape=jax.ShapeDtypeStruct((n, width), BF16),
        scratch_shapes=[pltpu.VMEM((TQ, seq), F32)] + _flash_scratch(2 * N_HEADS_DIFF, seq),
        compiler_params=_cparams("arbitrary", "arbitrary"),
        name="diff_attention",
    )(hb, hb, hb, lam_vecs, subln_g)


def _dilated_kernel(q_ref, k_ref, v_ref, o_ref, dmat_ref, amat_ref, *scratch):
    i = pl.program_id(1)
    t0 = i * TQ
    w_max = max(w for w, _ in DIL_GROUPS)
    c_lo = jnp.maximum(t0 - w_max, 0) // KC
    c_hi = (t0 + TQ + KC - 1) // KC
    row = t0 + lax.broadcasted_iota(I32, (TQ, KC), 0)
    col_in_chunk = lax.broadcasted_iota(I32, (TQ, KC), 1)

    def mask_body(c, _):
        k0 = pl.multiple_of(c * KC, KC)
        dist = row - (k0 + col_in_chunk)
        mult = jnp.zeros((TQ, KC), F32)
        for w, d in DIL_GROUPS:
            member = (dist >= 0) & (dist <= w) & ((dist & (d - 1)) == 0)
            mult = mult + jnp.where(member, 1.0, 0.0)
        on = mult > 0.0
        dmat_ref[:, pl.ds(k0, KC)] = jnp.where(on, dist.astype(F32), jnp.inf)
        amat_ref[:, pl.ds(k0, KC)] = jnp.log2(jnp.where(on, mult, 1.0))
        return 0

    lax.fori_loop(c_lo, c_hi, mask_body, 0)

    slopes = _alibi_slopes(N_HEADS_DIL)
    n_tiles = N_HEADS_DIL * D_HEAD // LANES
    _split_head_pairs(q_ref, scratch[0], n_tiles, D_HEAD ** -0.5)
    chains = [(h // 2, h // 2, slopes[h]) for h in range(N_HEADS_DIL)]
    _flash_chains(chains, k_ref, v_ref, dmat_ref, amat_ref, c_lo, c_hi, scratch)
    _merge_head_pairs(o_ref, scratch[-2], scratch[-1], n_tiles)


def _dilated_attention(hb, batch, seq):
    n = hb.shape[0]
    nq = seq // TQ
    width = N_HEADS_DIL * D_HEAD
    return pl.pallas_call(
        _dilated_kernel,
        grid=(batch, nq),
        in_specs=[
            pl.BlockSpec((TQ, width), lambda b, i: (b * nq + i, 3)),
            pl.BlockSpec((seq, width), lambda b, i: (b, 4)),
            pl.BlockSpec((seq, width), lambda b, i: (b, 5)),
        ],
        out_specs=pl.BlockSpec((TQ, width), lambda b, i: (b * nq + i, 0)),
        out_shape=jax.ShapeDtypeStruct((n, width), BF16),
        scratch_shapes=[pltpu.VMEM((TQ, seq), F32), pltpu.VMEM((TQ, seq), F32)] + _flash_scratch(N_HEADS_DIL, seq),
        compiler_params=_cparams("arbitrary", "arbitrary"),
        name="dilated_attention",
    )(hb, hb, hb)


def _mix_out_kernel(ya_ref, yb_ref, wa_ref, wb_ref, x_ref, g_ref, b_ref, rw_ref, rb_ref,
                    x1_ref, x1t_ref, eidx_ref, gate_ref, rank_ref, cnt_ref, carry_ref):
    step = pl.program_id(0)

    @pl.when(step == 0)
    def _():
        carry_ref[...] = jnp.zeros_like(carry_ref)

    m = (jnp.dot(ya_ref[...], wa_ref[...], preferred_element_type=F32)
         + jnp.dot(yb_ref[...], wb_ref[...], preferred_element_type=F32))
    x1 = _layer_norm(DEEPNORM_ALPHA * x_ref[...] + m, g_ref[...], b_ref[...])
    x1_ref[...] = x1
    _to_token_tiles(x1t_ref, x1)

    t = x1.shape[0]
    logits = jnp.dot(x1.astype(BF16), rw_ref[...], preferred_element_type=F32) + rb_ref[...]
    lane = lax.broadcasted_iota(I32, (t, LANES), 1)
    vals, idxs = [], []
    lg = logits
    for _ in range(TOP_K):
        mx = jnp.max(lg, axis=1, keepdims=True)
        ix = jnp.min(jnp.where(lg == mx, lane, LANES), axis=1, keepdims=True)
        vals.append(mx)
        idxs.append(ix)
        lg = jnp.where(lane == ix, -jnp.inf, lg)
    exps = [jnp.exp(v - vals[0]) for v in vals]
    denom = exps[0]
    for e in exps[1:]:
        denom = denom + e

    onehot = jnp.zeros((t, LANES), F32)
    for ix in idxs:
        onehot = onehot + jnp.where(lane == ix, 1.0, 0.0)
    strict = (lax.broadcasted_iota(I32, (t, t), 1) < lax.broadcasted_iota(I32, (t, t), 0)).astype(BF16)
    before = jnp.dot(strict, onehot.astype(BF16), preferred_element_type=F32) + carry_ref[...]

    eidx = jnp.zeros((t, LANES), I32)
    gate = jnp.zeros((t, LANES), F32)
    rank = jnp.zeros((t, LANES), I32)
    for k in range(TOP_K):
        rk = jnp.sum(jnp.where(lane == idxs[k], before, 0.0), axis=1, keepdims=True)
        eidx = jnp.where(lane == k, idxs[k], eidx)
        gate = jnp.where(lane == k, exps[k] / denom, gate)
        rank = jnp.where(lane == k, rk.astype(I32), rank)
    eidx_ref[...] = eidx
    gate_ref[...] = gate
    rank_ref[...] = rank
    carry_ref[...] = carry_ref[...] + jnp.sum(onehot, axis=0, keepdims=True)
    cnt_ref[...] = carry_ref[...]


def _mix_out(ya, yb, w_out, x2d, ln_g, ln_b, router_w, router_b):
    n, d = x2d.shape
    half = ya.shape[1]
    wa = w_out[:half].astype(BF16)
    wb = w_out[half:].astype(BF16)
    rw = jnp.zeros((d, LANES), BF16).at[:, :N_EXPERTS].set(router_w.astype(BF16))
    rb = jnp.full((1, LANES), -jnp.inf, F32).at[0, :N_EXPERTS].set(router_b)
    t = ROW_TILE
    full = lambda a: pl.BlockSpec(a.shape, lambda i: (0, 0))
    rows = lambda w: pl.BlockSpec((t, w), lambda i: (i, 0))
    g2, b2 = ln_g.reshape(1, d), ln_b.reshape(1, d)
    return pl.pallas_call(
        _mix_out_kernel,
        grid=(n // t,),
        in_specs=[rows(half), rows(half), full(wa), full(wb), rows(d), full(g2), full(b2), full(rw), full(rb)],
        out_specs=[rows(d), pl.BlockSpec((t * ROW_SUBLANES, LANES), lambda i: (i, 0)),
                   rows(LANES), rows(LANES), rows(LANES), pl.BlockSpec((1, LANES), lambda i: (0, 0))],
        out_shape=[jax.ShapeDtypeStruct((n, d), F32), jax.ShapeDtypeStruct((n * ROW_SUBLANES, LANES), F32),
                   jax.ShapeDtypeStruct((n, LANES), I32),
                   jax.ShapeDtypeStruct((n, LANES), F32), jax.ShapeDtypeStruct((n, LANES), I32),
                   jax.ShapeDtypeStruct((1, LANES), F32)],
        scratch_shapes=[pltpu.VMEM((1, LANES), F32)],
        compiler_params=_cparams("arbitrary"),
        name="mix_out_router",
    )(ya, yb, wa, wb, x2d, g2, b2, rw, rb)


ROW_SUBLANES = 8


def _to_token_tiles(o_ref, x):
    for j in range(ROW_SUBLANES):
        o_ref[pl.ds(j, x.shape[0], stride=ROW_SUBLANES), :] = x[:, j * LANES:(j + 1) * LANES]


def _from_token_tiles(buf_ref, n_rows):
    return [buf_ref[pl.ds(j, n_rows, stride=ROW_SUBLANES), :] for j in range(ROW_SUBLANES)]


def _row_copy(src_hbm, dst_ref, src_row, dst_row, sem):
    src = src_hbm.at[pl.ds(pl.multiple_of(src_row * ROW_SUBLANES, ROW_SUBLANES), ROW_SUBLANES)]
    dst = dst_ref.at[pl.ds(pl.multiple_of(dst_row * ROW_SUBLANES, ROW_SUBLANES), ROW_SUBLANES)]
    return pltpu.make_async_copy(src, dst, sem)


def _rows_wait(src_hbm, dst_ref, sem):
    pltpu.make_async_copy(src_hbm.at[pl.ds(0, dst_ref.shape[0])], dst_ref, sem).wait()


def _expert_kernel(blk_e_ref, n_used_ref, run_start_ref, run_parity_ref, next_e_ref, dest_ref, pad_lo_ref, pad_hi_ref,
                   x_hbm, wgu_hbm, bgu_ref, wdn_hbm, bdn_ref,
                   o_ref, row_tok_ref, xbuf, wgu_raw, wdn_raw, wgu_bf, wdn_bf, sem, wsem):
    i = pl.program_id(0)
    n_used = n_used_ref[0]
    slot = i % 2
    e = blk_e_ref[i]
    parity = run_parity_ref[i]

    def weight_copies(expert, p):
        return (pltpu.make_async_copy(wgu_hbm.at[expert], wgu_raw.at[p], wsem.at[p, 0]),
                pltpu.make_async_copy(wdn_hbm.at[expert], wdn_raw.at[p], wsem.at[p, 1]))

    def gather(block, s):
        base = block * MOE_BM

        def body(g, _):
            for u in range(ISSUE_UNROLL):
                r = g * ISSUE_UNROLL + u
                _row_copy(x_hbm, xbuf.at[s], row_tok_ref[base + r], r, sem.at[s]).start(priority=u % 2)
            return 0
        lax.fori_loop(0, MOE_BM // ISSUE_UNROLL, body, 0)

    def build_row_tokens():
        def clear_segment(s, _):
            def clear(r, _):
                row_tok_ref[r] = 0
                return 0
            lax.fori_loop(pad_lo_ref[s], pad_hi_ref[s], clear, 0)
            return 0

        lax.fori_loop(0, pad_lo_ref.shape[0], clear_segment, 0)

        def put(g, _):
            base = g * ISSUE_UNROLL
            rows = [dest_ref[base + u] for u in range(ISSUE_UNROLL)]
            tok0 = g * (ISSUE_UNROLL // TOP_K)
            for u in range(ISSUE_UNROLL):
                row_tok_ref[rows[u]] = tok0 + u // TOP_K
            return 0

        lax.fori_loop(0, dest_ref.shape[0] // ISSUE_UNROLL, put, 0)

    @pl.when(i == 0)
    def _():
        build_row_tokens()
        gather(0, 0)

        @pl.when(run_start_ref[0] == 1)
        def _():
            for c in weight_copies(e, parity):
                c.start()

    @pl.when(run_start_ref[i] == 1)
    def _():
        for c in weight_copies(e, parity):
            c.wait()
        wgu_bf[...] = wgu_raw[parity].astype(BF16)
        wdn_bf[...] = wdn_raw[parity].astype(BF16)
        e_next = next_e_ref[i]

        @pl.when(e_next >= 0)
        def _():
            for c in weight_copies(e_next, 1 - parity):
                c.start()

    gather(jnp.minimum(i + 1, pl.num_programs(0) - 1), 1 - slot)
    _rows_wait(x_hbm, xbuf.at[slot], sem.at[slot])

    @pl.when(i < n_used)
    def _():
        x = jnp.concatenate(_from_token_tiles(xbuf.at[slot], MOE_BM), axis=1).astype(BF16)
        h = jnp.dot(x, wgu_bf[...], preferred_element_type=F32) + bgu_ref[...]
        gate = jnp.minimum(h[:, :D_FF], SWIGLU_LIMIT)
        up = jnp.clip(h[:, D_FF:], -SWIGLU_LIMIT, SWIGLU_LIMIT)
        glu = gate * (1.0 / (1.0 + jnp.exp(-SWIGLU_ALPHA * gate)))
        act = ((up + 1.0) * glu).astype(BF16)
        _to_token_tiles(o_ref, jnp.dot(act, wdn_bf[...], preferred_element_type=F32) + bdn_ref[...])

    @pl.when(i >= n_used)
    def _():
        o_ref[...] = jnp.zeros_like(o_ref)

    @pl.when(i == pl.num_programs(0) - 1)
    def _():
        _rows_wait(x_hbm, xbuf.at[1 - slot], sem.at[1 - slot])


def _expert_ffn(x_tiles, dest_flat, pad_lo, pad_hi, n_rows, blk_e, n_used, w_gu, b_gu, w_dn, b_dn):
    d = w_gu.shape[1]
    n_blocks = n_rows // MOE_BM
    ne = w_gu.shape[0]
    blk = jnp.arange(n_blocks, dtype=I32)
    prev_e = jnp.concatenate([blk_e[:1], blk_e[:-1]])
    run_start = (blk < n_used[0]) & ((blk == 0) | (blk_e != prev_e))
    run_parity = ((jnp.cumsum(run_start.astype(I32)) - 1) % 2).astype(I32)
    first_start_from = lax.cummin(jnp.where(run_start, blk, n_blocks)[::-1])[::-1]
    next_start = jnp.concatenate([first_start_from[1:], jnp.full((1,), n_blocks, I32)])
    next_e = jnp.where(next_start < n_blocks, blk_e[jnp.minimum(next_start, n_blocks - 1)], -1).astype(I32)
    idx = lambda f: (lambda i, *prefetch: f(i, prefetch[0]))
    grid_spec = pltpu.PrefetchScalarGridSpec(
        num_scalar_prefetch=8,
        grid=(n_blocks,),
        in_specs=[
            pl.BlockSpec(memory_space=pl.ANY),
            pl.BlockSpec(memory_space=pl.ANY),
            pl.BlockSpec((None, 1, 2 * D_FF), idx(lambda i, be: (be[i], 0, 0))),
            pl.BlockSpec(memory_space=pl.ANY),
            pl.BlockSpec((None, 1, d), idx(lambda i, be: (be[i], 0, 0))),
        ],
        out_specs=pl.BlockSpec((MOE_BM * ROW_SUBLANES, LANES), idx(lambda i, be: (i, 0))),
        scratch_shapes=[pltpu.SMEM((n_rows,), I32), pltpu.VMEM((2, MOE_BM * ROW_SUBLANES, LANES), F32),
                        pltpu.VMEM((2, d, 2 * D_FF), F32), pltpu.VMEM((2, D_FF, d), F32),
                        pltpu.VMEM((d, 2 * D_FF), BF16), pltpu.VMEM((D_FF, d), BF16),
                        pltpu.SemaphoreType.DMA((2,)), pltpu.SemaphoreType.DMA((2, 2))],
    )
    return pl.pallas_call(
        _expert_kernel,
        grid_spec=grid_spec,
        out_shape=jax.ShapeDtypeStruct((n_rows * ROW_SUBLANES, LANES), F32),
        compiler_params=_cparams("arbitrary"),
        name="moe_experts",
    )(blk_e, n_used, run_start.astype(I32), run_parity, next_e, dest_flat, pad_lo, pad_hi,
      x_tiles, w_gu, b_gu.reshape(ne, 1, 2 * D_FF), w_dn, b_dn.reshape(ne, 1, d))


def _combine_kernel(dest_ref, y_hbm, gate_ref, x_ref, g_ref, b_ref, o_ref, buf_ref, sem):
    i = pl.program_id(0)
    t = x_ref.shape[0]
    slot = i % 2

    def gather(step, s):
        base = step * (t * TOP_K)

        def body(g, _):
            for u in range(ISSUE_UNROLL // TOP_K):
                r = g * (ISSUE_UNROLL // TOP_K) + u
                for k in range(TOP_K):
                    row = dest_ref[base + r * TOP_K + k]
                    _row_copy(y_hbm, buf_ref.at[s, k], row, r, sem.at[s]).start(priority=k % 2)
            return 0
        lax.fori_loop(0, t * TOP_K // ISSUE_UNROLL, body, 0)

    @pl.when(i == 0)
    def _():
        gather(0, 0)

    @pl.when(i + 1 < pl.num_programs(0))
    def _():
        gather(i + 1, 1 - slot)

    for k in range(TOP_K):
        _rows_wait(y_hbm, buf_ref.at[slot, k], sem.at[slot])
    gate = gate_ref[...]
    f_tiles = None
    for k in range(TOP_K):
        y_tiles = [gate[:, k:k + 1] * yt for yt in _from_token_tiles(buf_ref.at[slot, k], t)]
        f_tiles = y_tiles if f_tiles is None else [a + b for a, b in zip(f_tiles, y_tiles)]
    f = jnp.concatenate(f_tiles, axis=1)
    o_ref[...] = _layer_norm(DEEPNORM_ALPHA * x_ref[...] + f, g_ref[...], b_ref[...])


def _combine(ys, dest_flat, gates, x2d, ln_g, ln_b):
    n, d = x2d.shape
    t = GATHER_T
    g2, b2 = ln_g.reshape(1, d), ln_b.reshape(1, d)
    full = lambda a: pl.BlockSpec(a.shape, lambda i, dest: (0, 0))
    grid_spec = pltpu.PrefetchScalarGridSpec(
        num_scalar_prefetch=1,
        grid=(n // t,),
        in_specs=[pl.BlockSpec(memory_space=pl.ANY),
                  pl.BlockSpec((t, LANES), lambda i, dest: (i, 0)),
                  pl.BlockSpec((t, d), lambda i, dest: (i, 0)),
                  full(g2), full(b2)],
        out_specs=pl.BlockSpec((t, d), lambda i, dest: (i, 0)),
        scratch_shapes=[pltpu.VMEM((2, TOP_K, t * ROW_SUBLANES, LANES), F32), pltpu.SemaphoreType.DMA((2,))],
    )
    return pl.pallas_call(
        _combine_kernel,
        grid_spec=grid_spec,
        out_shape=jax.ShapeDtypeStruct((n, d), F32),
        compiler_params=_cparams("arbitrary"),
        name="moe_combine",
    )(dest_flat, ys, gates, x2d, g2, b2)


def _moe_block(x1, x1_tiles, eidx, gates, rank, counts, w_gu, b_gu, w_dn, b_dn, ln_g, ln_b):
    n = x1.shape[0]
    e_sel = eidx[:, :TOP_K]
    cnt = counts[0, :N_EXPERTS].astype(I32)
    padded = ((cnt + MOE_BM - 1) // MOE_BM) * MOE_BM
    pad_end = jnp.cumsum(padded)
    pad_start = pad_end - padded
    expert_ids = jnp.arange(N_EXPERTS, dtype=I32)
    dest = jnp.sum(jnp.where(e_sel[..., None] == expert_ids, pad_start, 0), axis=-1) + rank[:, :TOP_K]
    n_rows = ((n * TOP_K + N_EXPERTS * (MOE_BM - 1) + MOE_BM - 1) // MOE_BM) * MOE_BM
    n_blocks = n_rows // MOE_BM
    dest_flat = dest.reshape(-1)
    pad_lo = jnp.concatenate([pad_start + cnt, pad_end[-1:]]).astype(I32)
    pad_hi = jnp.concatenate([pad_end, jnp.full((1,), n_rows)]).astype(I32)
    blk_start = jnp.arange(n_blocks, dtype=I32) * MOE_BM
    blk_e = jnp.minimum(jnp.sum((pad_end[None, :] <= blk_start[:, None]).astype(I32), axis=1), N_EXPERTS - 1)
    n_used = (pad_end[-1:] // MOE_BM).astype(I32)
    ys = _expert_ffn(x1_tiles, dest_flat, pad_lo, pad_hi, n_rows, blk_e, n_used, w_gu, b_gu, w_dn, b_dn)
    return _combine(ys, dest_flat, gates, x1, ln_g, ln_b)


def _pad_cols(w, width):
    return jnp.pad(w, ((0, 0), (0, width - w.shape[1])))


def kernel(x, w_in_0, conv_w_0, w_out_0, ln_mix_g_0, ln_mix_b_0, router_w_0, router_b_0, w_gu_0, b_gu_0, w_dn_0, b_dn_0, ln_ffn_g_0, ln_ffn_b_0, w_in_1, lam_q1_1, lam_k1_1, lam_q2_1, lam_k2_1, subln_g_1, w_out_1, ln_mix_g_1, ln_mix_b_1, router_w_1, router_b_1, w_gu_1, b_gu_1, w_dn_1, b_dn_1, ln_ffn_g_1, ln_ffn_b_1):
    batch, seq, d = x.shape
    x0 = x.reshape(batch * seq, d)

    n_f32 = 3 * CONV_W
    n_attn = 3 * N_HEADS_SPARSE * D_HEAD + N_IDX_HEADS * D_IDX + D_IDX
    w_a = _pad_cols(jnp.concatenate([w_in_0[:, :n_f32], w_in_0[:, n_f32 + n_attn:]], axis=1),
                    n_f32 + LANES)
    w_b = _pad_cols(w_in_0[:, n_f32:n_f32 + n_attn], 3 * N_HEADS_SPARSE * D_HEAD + N_IDX_HEADS * D_IDX + LANES)
    ha, hb = _project(x0, [w_a.astype(BF16), w_b.astype(BF16)], [F32, BF16])
    ya = _short_conv(ha, conv_w_0, batch, seq)
    yb = _dsa_attention(ha, hb, batch, seq)
    x1, x1t, eidx, gates, rank, counts = _mix_out(ya, yb, w_out_0, x0, ln_mix_g_0, ln_mix_b_0, router_w_0, router_b_0)
    x2 = _moe_block(x1, x1t, eidx, gates, rank, counts, w_gu_0, b_gu_0, w_dn_0, b_dn_0, ln_ffn_g_0, ln_ffn_b_0)

    (hc,) = _project(x2, [w_in_1.astype(BF16)], [BF16])
    lam_vecs = jnp.stack([lam_q1_1, lam_k1_1, lam_q2_1, lam_k2_1]).astype(F32)
    yc = _diff_attention(hc, lam_vecs, subln_g_1.reshape(1, -1).astype(F32), batch, seq, 1)
    yd = _dilated_attention(hc, batch, seq)
    x3, x3t, eidx, gates, rank, counts = _mix_out(yc, yd, w_out_1, x2, ln_mix_g_1, ln_mix_b_1, router_w_1, router_b_1)
    x4 = _moe_block(x3, x3t, eidx, gates, rank, counts, w_gu_1, b_gu_1, w_dn_1, b_dn_1, ln_ffn_g_1, ln_ffn_b_1)
    return x4.reshape(batch, seq, d)
```

```python
import functools
import math

import jax
import jax.numpy as jnp
from jax import lax
from jax.experimental import pallas as pl
from jax.experimental.pallas import tpu as pltpu

F32 = jnp.float32
BF16 = jnp.bfloat16
I32 = jnp.int32

CONV_W = 512
CONV_TAPS = 3
N_HEADS_SPARSE = 8
D_HEAD = 64
N_IDX_HEADS = 8
D_IDX = 32
TOPK_LIMIT = 256
N_HEADS_DIFF = 4
N_HEADS_DIL = 8
DIL_GROUPS = ((128, 1), (512, 4), (2048, 16))
N_EXPERTS = 32
TOP_K = 4
D_FF = 1024
SWIGLU_LIMIT = 7.0
SWIGLU_ALPHA = 1.702
DEPTH = 2
DEEPNORM_ALPHA = (2 * DEPTH) ** 0.25
LN_EPS = 1e-5
RMS_EPS = 1e-5

LANES = 128
VMEM_LIMIT = 56 * 1024 * 1024
TQ = 256
CHAIN_GROUP = 4
KC = 512
ROW_TILE = 512
MOE_BM = 256
GATHER_T = 256
ISSUE_UNROLL = 8
COUNT_ROWS = 64
LOG2_E = math.log2(math.e)
INT_MIN = -2 ** 31


def _alibi_slopes(n):
    return [2.0 ** (-8.0 * (h + 1) / n) for h in range(n)]


def _cparams(*sem):
    return pltpu.CompilerParams(dimension_semantics=sem, vmem_limit_bytes=VMEM_LIMIT)


def _layer_norm(z, g, b):
    mu = jnp.mean(z, axis=-1, keepdims=True)
    zc = z - mu
    var = jnp.mean(zc * zc, axis=-1, keepdims=True)
    return zc * lax.rsqrt(var + LN_EPS) * g + b


def _proj_kernel(*refs, n_out):
    x_ref = refs[0]
    w_refs = refs[1:1 + n_out]
    o_refs = refs[1 + n_out:]
    xb = x_ref[...].astype(BF16)
    for w_ref, o_ref in zip(w_refs, o_refs):
        o_ref[...] = jnp.dot(xb, w_ref[...], preferred_element_type=F32).astype(o_ref.dtype)


def _project(x2d, weights, out_dtypes):
    n, d = x2d.shape
    n_out = len(weights)
    in_specs = [pl.BlockSpec((ROW_TILE, d), lambda i: (i, 0))]
    in_specs += [pl.BlockSpec(w.shape, lambda i: (0, 0)) for w in weights]
    out_specs = [pl.BlockSpec((ROW_TILE, w.shape[1]), lambda i: (i, 0)) for w in weights]
    out_shape = [jax.ShapeDtypeStruct((n, w.shape[1]), dt) for w, dt in zip(weights, out_dtypes)]
    return pl.pallas_call(
        functools.partial(_proj_kernel, n_out=n_out),
        grid=(n // ROW_TILE,),
        in_specs=in_specs, out_specs=out_specs, out_shape=out_shape,
        compiler_params=_cparams("parallel"),
        name="in_proj",
    )(x2d, *weights)


def _conv_kernel(gb_ref, gc_ref, xa_ref, w_ref, o_ref, prev_ref):
    j = pl.program_id(1)
    t = gb_ref.shape[0]

    @pl.when(j == 0)
    def _():
        prev_ref[...] = jnp.zeros_like(prev_ref)

    z = gc_ref[...] * xa_ref[...]
    row = lax.broadcasted_iota(I32, z.shape, 0)
    prev2 = prev_ref[0:1, :]
    prev1 = prev_ref[1:2, :]
    z1 = jnp.where(row == 0, prev1, pltpu.roll(z, 1, 0))
    z2 = jnp.where(row == 0, prev2, jnp.where(row == 1, prev1, pltpu.roll(z, 2, 0)))
    w = w_ref[...]
    y = w[0:1, :] * z + w[1:2, :] * z1 + w[2:3, :] * z2
    o_ref[...] = (gb_ref[...] * y).astype(o_ref.dtype)
    prev_ref[0:1, :] = z[t - 2:t - 1, :]
    prev_ref[1:2, :] = z[t - 1:t, :]


def _short_conv(ha, conv_w, batch, seq):
    n = ha.shape[0]
    t = min(512, seq)
    nj = seq // t
    spec = lambda c: pl.BlockSpec((t, CONV_W), lambda b, j, c=c: (b * nj + j, c))
    return pl.pallas_call(
        _conv_kernel,
        grid=(batch, nj),
        in_specs=[spec(0), spec(1), spec(2), pl.BlockSpec((CONV_TAPS, CONV_W), lambda b, j: (0, 0))],
        out_specs=pl.BlockSpec((t, CONV_W), lambda b, j: (b * nj + j, 0)),
        out_shape=jax.ShapeDtypeStruct((n, CONV_W), BF16),
        scratch_shapes=[pltpu.VMEM((8, CONV_W), F32)],
        compiler_params=_cparams("arbitrary", "arbitrary"),
        name="short_conv",
    )(ha, ha, ha, conv_w)


def _lane_tiles(x):
    return [x[:, u * LANES:(u + 1) * LANES] for u in range(x.shape[1] // LANES)]


def _flash_chains(chains, k_ref, v_ref, dmat_ref, amat_ref, c_lo, c_hi, scratch):
    qm_ref, s_ref, mx_ref, lp_ref, l_ref, acc_ref = scratch
    mx_ref[...] = jnp.full(mx_ref.shape, -jnp.inf, F32)
    lp_ref[...] = jnp.zeros(lp_ref.shape, F32)
    acc_ref[...] = jnp.zeros(acc_ref.shape, F32)

    for g0 in range(0, len(chains), CHAIN_GROUP):
        group = list(enumerate(chains))[g0:g0 + CHAIN_GROUP]

        def pass_a(c, _, group=group, g0=g0):
            k0 = pl.multiple_of(c * KC, KC)
            for n, (k_tile, _, slope) in group:
                kc = k_ref[pl.ds(k0, KC), k_tile * LANES:(k_tile + 1) * LANES]
                s = lax.dot_general(qm_ref[n], kc, (((1,), (1,)), ((), ())), preferred_element_type=F32)
                s = s * LOG2_E - (slope * LOG2_E) * dmat_ref[:, pl.ds(k0, KC)]
                if amat_ref is not None:
                    s = s + amat_ref[:, pl.ds(k0, KC)]
                s_ref[n - g0, :, pl.ds(k0, KC)] = s
                part = mx_ref[n]
                for t in _lane_tiles(s):
                    part = jnp.maximum(part, t)
                mx_ref[n] = part
            return 0

        lax.fori_loop(c_lo, c_hi, pass_a, 0)

        for n, _ in group:
            row_max = jnp.max(mx_ref[n], axis=1, keepdims=True)
            mx_ref[n] = jnp.broadcast_to(row_max, (TQ, LANES))

        def pass_b(c, _, group=group, g0=g0):
            k0 = pl.multiple_of(c * KC, KC)
            for n, (_, v_tile, _) in group:
                row_max = mx_ref[n]
                p_tiles = [jnp.exp2(t - row_max) for t in _lane_tiles(s_ref[n - g0, :, pl.ds(k0, KC)])]
                part = lp_ref[n]
                for t in p_tiles:
                    part = part + t
                lp_ref[n] = part
                p = jnp.concatenate(p_tiles, axis=1).astype(BF16)
                vc = v_ref[pl.ds(k0, KC), v_tile * LANES:(v_tile + 1) * LANES]
                acc_ref[n] = acc_ref[n] + jnp.dot(p, vc, preferred_element_type=F32)
            return 0

        lax.fori_loop(c_lo, c_hi, pass_b, 0)

    for n in range(len(chains)):
        l_ref[n] = jnp.sum(lp_ref[n], axis=1, keepdims=True)


def _split_head_pairs(q_ref, qm_ref, n_tiles, scale):
    low = lax.broadcasted_iota(I32, (TQ, LANES), 1) < D_HEAD
    for j in range(n_tiles):
        qt = q_ref[:, j * LANES:(j + 1) * LANES] * scale
        qm_ref[2 * j] = jnp.where(low, qt, 0).astype(BF16)
        qm_ref[2 * j + 1] = jnp.where(low, 0, qt).astype(BF16)


def _merge_head_pairs(o_ref, l_ref, acc_ref, n_tiles):
    low = lax.broadcasted_iota(I32, (TQ, LANES), 1) < D_HEAD
    for j in range(n_tiles):
        out = jnp.where(low, acc_ref[2 * j] / l_ref[2 * j], acc_ref[2 * j + 1] / l_ref[2 * j + 1])
        o_ref[:, j * LANES:(j + 1) * LANES] = out.astype(o_ref.dtype)


def _flash_scratch(n_chains, seq):
    return [pltpu.VMEM((n_chains, TQ, LANES), BF16), pltpu.VMEM((CHAIN_GROUP, TQ, seq), F32),
            pltpu.VMEM((n_chains, TQ, LANES), F32), pltpu.VMEM((n_chains, TQ, LANES), F32),
            pltpu.VMEM((n_chains, TQ, 1), F32), pltpu.VMEM((n_chains, TQ, LANES), F32)]


def _dsa_kernel(q_ref, k_ref, v_ref, iq_ref, ik_ref, iw_ref, o_ref, sc_ref, sct_ref, dmat_ref, *scratch, k_sel):
    i = pl.program_id(1)
    seq_len = sc_ref.shape[1]
    t0 = i * TQ
    n_chunks = (t0 + TQ + KC - 1) // KC
    idx_scale = (D_IDX ** -0.5) * (N_IDX_HEADS ** -0.5)
    row = t0 + lax.broadcasted_iota(I32, (TQ, KC), 0)
    col_in_chunk = lax.broadcasted_iota(I32, (TQ, KC), 1)

    iq = iq_ref[...]
    iw = iw_ref[:, 0:N_IDX_HEADS]

    def score_body(c, _):
        k0 = pl.multiple_of(c * KC, KC)
        ik = ik_ref[pl.ds(k0, KC), 0:D_IDX]
        sc = jnp.zeros((TQ, KC), F32)
        for h in range(N_IDX_HEADS):
            rel = lax.dot_general(iq[:, h * D_IDX:(h + 1) * D_IDX], ik,
                                  (((1,), (1,)), ((), ())), preferred_element_type=F32)
            sc = sc + iw[:, h:h + 1] * jnp.maximum(rel, 0.0)
        causal = (k0 + col_in_chunk) <= row
        sc = jnp.where(causal, sc * idx_scale, -jnp.inf)
        sc_ref[:, pl.ds(k0, KC)] = sc
        sct_ref[pl.ds(k0, KC), :] = sc.T
        return 0

    lax.fori_loop(0, n_chunks, score_body, 0)

    def code_to_float(code):
        return lax.bitcast_convert_type(jnp.where(code < 0, code ^ 0x7FFFFFFF, code), F32)

    def count(pred):
        def body(c, acc):
            k0 = pl.multiple_of(c * KC, KC)
            hit = jnp.where(pred(sct_ref[pl.ds(k0, KC), :]), 1.0, 0.0)
            return acc + jnp.sum(hit.reshape(KC // COUNT_ROWS, COUNT_ROWS, TQ), axis=0)
        acc = lax.fori_loop(0, n_chunks, body, jnp.zeros((COUNT_ROWS, TQ), F32))
        return jnp.sum(acc, axis=0, keepdims=True)

    kf = float(k_sel)
    has_k = count(lambda t: t > -jnp.inf) >= kf
    code = jnp.where(count(lambda t: t >= 0.0) >= kf, 0, INT_MIN).astype(I32)

    def bit_body(b, code):
        cand = code + lax.shift_left(jnp.int32(1), 30 - b)
        cand_f = code_to_float(cand)
        return jnp.where(count(lambda t: t >= cand_f) >= kf, cand, code)

    code = lax.fori_loop(0, 31, bit_body, code)
    thr_q = jnp.where(has_k, code_to_float(code), -jnp.inf)
    need_q = jnp.where(has_k, kf - count(lambda t: t > thr_q), float(seq_len))
    thr = jnp.broadcast_to(thr_q, (LANES, TQ)).T
    need = jnp.broadcast_to(need_q, (LANES, TQ)).T

    tri = (lax.broadcasted_iota(I32, (KC, KC), 0) <= lax.broadcasted_iota(I32, (KC, KC), 1)).astype(BF16)
    thr_w = jnp.concatenate([thr] * (KC // LANES), axis=1)
    need_w = jnp.concatenate([need] * (KC // LANES), axis=1)

    def mask_body(c, ties_before):
        k0 = pl.multiple_of(c * KC, KC)
        sc = sc_ref[:, pl.ds(k0, KC)]
        eq = sc == thr_w
        eqf = jnp.where(eq, 1.0, 0.0)
        rank = ties_before + jnp.dot(eqf.astype(BF16), tri, preferred_element_type=F32)
        dist = row - (k0 + col_in_chunk)
        sel = ((sc > thr_w) | (eq & (rank <= need_w))) & (dist >= 0)
        dmat_ref[:, pl.ds(k0, KC)] = jnp.where(sel, dist.astype(F32), jnp.inf)
        return ties_before + jnp.sum(eqf, axis=1, keepdims=True)

    lax.fori_loop(0, n_chunks, mask_body, jnp.zeros((TQ, 1), F32))

    slopes = _alibi_slopes(N_HEADS_SPARSE)
    n_tiles = N_HEADS_SPARSE * D_HEAD // LANES
    _split_head_pairs(q_ref, scratch[0], n_tiles, D_HEAD ** -0.5)
    chains = [(h // 2, h // 2, slopes[h]) for h in range(N_HEADS_SPARSE)]
    _flash_chains(chains, k_ref, v_ref, dmat_ref, None, 0, n_chunks, scratch)
    _merge_head_pairs(o_ref, scratch[-2], scratch[-1], n_tiles)


def _dsa_attention(ha, hb, batch, seq):
    n = hb.shape[0]
    nq = seq // TQ
    k_sel = min(TOPK_LIMIT, seq // 4)
    width = N_HEADS_SPARSE * D_HEAD
    iq_w = N_IDX_HEADS * D_IDX
    return pl.pallas_call(
        functools.partial(_dsa_kernel, k_sel=k_sel),
        grid=(batch, nq),
        in_specs=[
            pl.BlockSpec((TQ, width), lambda b, i: (b * nq + i, 0)),
            pl.BlockSpec((seq, width), lambda b, i: (b, 1)),
            pl.BlockSpec((seq, width), lambda b, i: (b, 2)),
            pl.BlockSpec((TQ, iq_w), lambda b, i: (b * nq + i, 3 * width // iq_w)),
            pl.BlockSpec((seq, LANES), lambda b, i: (b, (3 * width + iq_w) // LANES)),
            pl.BlockSpec((TQ, LANES), lambda b, i: (b * nq + i, 3 * CONV_W // LANES)),
        ],
        out_specs=pl.BlockSpec((TQ, width), lambda b, i: (b * nq + i, 0)),
        out_shape=jax.ShapeDtypeStruct((n, width), BF16),
        scratch_shapes=[pltpu.VMEM((TQ, seq), F32), pltpu.VMEM((seq, TQ), F32), pltpu.VMEM((TQ, seq), F32)]
        + _flash_scratch(N_HEADS_SPARSE, seq),
        compiler_params=_cparams("arbitrary", "arbitrary"),
        name="dsa_attention",
    )(hb, hb, hb, hb, hb, ha)


def _diff_kernel(q_ref, k_ref, v_ref, lam_ref, g_ref, o_ref, dmat_ref, *scratch, lam_init):
    i = pl.program_id(1)
    t0 = i * TQ
    n_chunks = (t0 + TQ + KC - 1) // KC
    row = t0 + lax.broadcasted_iota(I32, (TQ, KC), 0)
    col_in_chunk = lax.broadcasted_iota(I32, (TQ, KC), 1)

    def mask_body(c, _):
        k0 = pl.multiple_of(c * KC, KC)
        dist = row - (k0 + col_in_chunk)
        dmat_ref[:, pl.ds(k0, KC)] = jnp.where(dist >= 0, dist.astype(F32), jnp.inf)
        return 0

    lax.fori_loop(0, n_chunks, mask_body, 0)

    lv = lam_ref[...]
    lam = (jnp.exp(jnp.sum(lv[0:1, :] * lv[1:2, :], axis=1, keepdims=True))
           - jnp.exp(jnp.sum(lv[2:3, :] * lv[3:4, :], axis=1, keepdims=True)) + lam_init)

    slopes = _alibi_slopes(N_HEADS_DIFF)
    _split_head_pairs(q_ref, scratch[0], N_HEADS_DIFF, D_HEAD ** -0.5)
    chains = [(n // 2, n // 2, slopes[n // 2]) for n in range(2 * N_HEADS_DIFF)]
    _flash_chains(chains, k_ref, v_ref, dmat_ref, None, 0, n_chunks, scratch)
    l_ref, acc_ref = scratch[-2], scratch[-1]
    g = g_ref[...]
    for h in range(N_HEADS_DIFF):
        of = acc_ref[2 * h] / l_ref[2 * h] - lam * (acc_ref[2 * h + 1] / l_ref[2 * h + 1])
        of = of * lax.rsqrt(jnp.mean(of * of, axis=1, keepdims=True) + RMS_EPS) * g
        o_ref[:, h * LANES:(h + 1) * LANES] = (of * (1.0 - lam_init)).astype(o_ref.dtype)


def _diff_attention(hb, lam_vecs, subln_g, batch, seq, layer):
    n = hb.shape[0]
    nq = seq // TQ
    width = N_HEADS_DIFF * 2 * D_HEAD
    lam_init = 0.8 - 0.6 * math.exp(-0.3 * layer)
    return pl.pallas_call(
        functools.partial(_diff_kernel, lam_init=lam_init),
        grid=(batch, nq),
        in_specs=[
            pl.BlockSpec((TQ, width), lambda b, i: (b * nq + i, 0)),
            pl.BlockSpec((seq, width), lambda b, i: (b, 1)),
            pl.BlockSpec((seq, width), lambda b, i: (b, 2)),
            pl.BlockSpec(lam_vecs.shape, lambda b, i: (0, 0)),
            pl.BlockSpec(subln_g.shape, lambda b, i: (0, 0)),
        ],
        out_specs=pl.BlockSpec((TQ, width), lambda b, i: (b * nq + i, 0)),
        out_shape=jax.ShapeDtypeStruct((n, width), BF16),
        scratch_shapes=[pltpu.VMEM((TQ, seq), F32)] + _flash_scratch(2 * N_HEADS_DIFF, seq),
        compiler_params=_cparams("arbitrary", "arbitrary"),
        name="diff_attention",
    )(hb, hb, hb, lam_vecs, subln_g)


def _dilated_kernel(q_ref, k_ref, v_ref, o_ref, dmat_ref, amat_ref, *scratch):
    i = pl.program_id(1)
    t0 = i * TQ
    w_max = max(w for w, _ in DIL_GROUPS)
    c_lo = jnp.maximum(t0 - w_max, 0) // KC
    c_hi = (t0 + TQ + KC - 1) // KC
    row = t0 + lax.broadcasted_iota(I32, (TQ, KC), 0)
    col_in_chunk = lax.broadcasted_iota(I32, (TQ, KC), 1)

    def mask_body(c, _):
        k0 = pl.multiple_of(c * KC, KC)
        dist = row - (k0 + col_in_chunk)
        mult = jnp.zeros((TQ, KC), F32)
        for w, d in DIL_GROUPS:
            member = (dist >= 0) & (dist <= w) & ((dist & (d - 1)) == 0)
            mult = mult + jnp.where(member, 1.0, 0.0)
        on = mult > 0.0
        dmat_ref[:, pl.ds(k0, KC)] = jnp.where(on, dist.astype(F32), jnp.inf)
        amat_ref[:, pl.ds(k0, KC)] = jnp.log2(jnp.where(on, mult, 1.0))
        return 0

    lax.fori_loop(c_lo, c_hi, mask_body, 0)

    slopes = _alibi_slopes(N_HEADS_DIL)
    n_tiles = N_HEADS_DIL * D_HEAD // LANES
    _split_head_pairs(q_ref, scratch[0], n_tiles, D_HEAD ** -0.5)
    chains = [(h // 2, h // 2, slopes[h]) for h in range(N_HEADS_DIL)]
    _flash_chains(chains, k_ref, v_ref, dmat_ref, amat_ref, c_lo, c_hi, scratch)
    _merge_head_pairs(o_ref, scratch[-2], scratch[-1], n_tiles)


def _dilated_attention(hb, batch, seq):
    n = hb.shape[0]
    nq = seq // TQ
    width = N_HEADS_DIL * D_HEAD
    return pl.pallas_call(
        _dilated_kernel,
        grid=(batch, nq),
        in_specs=[
            pl.BlockSpec((TQ, width), lambda b, i: (b * nq + i, 3)),
            pl.BlockSpec((seq, width), lambda b, i: (b, 4)),
            pl.BlockSpec((seq, width), lambda b, i: (b, 5)),
        ],
        out_specs=pl.BlockSpec((TQ, width), lambda b, i: (b * nq + i, 0)),
        out_shape=jax.ShapeDtypeStruct((n, width), BF16),
        scratch_shapes=[pltpu.VMEM((TQ, seq), F32), pltpu.VMEM((TQ, seq), F32)] + _flash_scratch(N_HEADS_DIL, seq),
        compiler_params=_cparams("arbitrary", "arbitrary"),
        name="dilated_attention",
    )(hb, hb, hb)


def _mix_out_kernel(ya_ref, yb_ref, wa_ref, wb_ref, x_ref, g_ref, b_ref, rw_ref, rb_ref,
                    x1_ref, x1t_ref, eidx_ref, gate_ref, rank_ref, cnt_ref, carry_ref):
    step = pl.program_id(0)

    @pl.when(step == 0)
    def _():
        carry_ref[...] = jnp.zeros_like(carry_ref)

    m = (jnp.dot(ya_ref[...], wa_ref[...], preferred_element_type=F32)
         + jnp.dot(yb_ref[...], wb_ref[...], preferred_element_type=F32))
    x1 = _layer_norm(DEEPNORM_ALPHA * x_ref[...] + m, g_ref[...], b_ref[...])
    x1_ref[...] = x1
    _to_token_tiles(x1t_ref, x1)

    t = x1.shape[0]
    logits = jnp.dot(x1.astype(BF16), rw_ref[...], preferred_element_type=F32) + rb_ref[...]
    lane = lax.broadcasted_iota(I32, (t, LANES), 1)
    vals, idxs = [], []
    lg = logits
    for _ in range(TOP_K):
        mx = jnp.max(lg, axis=1, keepdims=True)
        ix = jnp.min(jnp.where(lg == mx, lane, LANES), axis=1, keepdims=True)
        vals.append(mx)
        idxs.append(ix)
        lg = jnp.where(lane == ix, -jnp.inf, lg)
    exps = [jnp.exp(v - vals[0]) for v in vals]
    denom = exps[0]
    for e in exps[1:]:
        denom = denom + e

    onehot = jnp.zeros((t, LANES), F32)
    for ix in idxs:
        onehot = onehot + jnp.where(lane == ix, 1.0, 0.0)
    strict = (lax.broadcasted_iota(I32, (t, t), 1) < lax.broadcasted_iota(I32, (t, t), 0)).astype(BF16)
    before = jnp.dot(strict, onehot.astype(BF16), preferred_element_type=F32) + carry_ref[...]

    eidx = jnp.zeros((t, LANES), I32)
    gate = jnp.zeros((t, LANES), F32)
    rank = jnp.zeros((t, LANES), I32)
    for k in range(TOP_K):
        rk = jnp.sum(jnp.where(lane == idxs[k], before, 0.0), axis=1, keepdims=True)
        eidx = jnp.where(lane == k, idxs[k], eidx)
        gate = jnp.where(lane == k, exps[k] / denom, gate)
        rank = jnp.where(lane == k, rk.astype(I32), rank)
    eidx_ref[...] = eidx
    gate_ref[...] = gate
    rank_ref[...] = rank
    carry_ref[...] = carry_ref[...] + jnp.sum(onehot, axis=0, keepdims=True)
    cnt_ref[...] = carry_ref[...]


def _mix_out(ya, yb, w_out, x2d, ln_g, ln_b, router_w, router_b):
    n, d = x2d.shape
    half = ya.shape[1]
    wa = w_out[:half].astype(BF16)
    wb = w_out[half:].astype(BF16)
    rw = jnp.zeros((d, LANES), BF16).at[:, :N_EXPERTS].set(router_w.astype(BF16))
    rb = jnp.full((1, LANES), -jnp.inf, F32).at[0, :N_EXPERTS].set(router_b)
    t = ROW_TILE
    full = lambda a: pl.BlockSpec(a.shape, lambda i: (0, 0))
    rows = lambda w: pl.BlockSpec((t, w), lambda i: (i, 0))
    g2, b2 = ln_g.reshape(1, d), ln_b.reshape(1, d)
    return pl.pallas_call(
        _mix_out_kernel,
        grid=(n // t,),
        in_specs=[rows(half), rows(half), full(wa), full(wb), rows(d), full(g2), full(b2), full(rw), full(rb)],
        out_specs=[rows(d), pl.BlockSpec((t * ROW_SUBLANES, LANES), lambda i: (i, 0)),
                   rows(LANES), rows(LANES), rows(LANES), pl.BlockSpec((1, LANES), lambda i: (0, 0))],
        out_shape=[jax.ShapeDtypeStruct((n, d), F32), jax.ShapeDtypeStruct((n * ROW_SUBLANES, LANES), F32),
                   jax.ShapeDtypeStruct((n, LANES), I32),
                   jax.ShapeDtypeStruct((n, LANES), F32), jax.ShapeDtypeStruct((n, LANES), I32),
                   jax.ShapeDtypeStruct((1, LANES), F32)],
        scratch_shapes=[pltpu.VMEM((1, LANES), F32)],
        compiler_params=_cparams("arbitrary"),
        name="mix_out_router",
    )(ya, yb, wa, wb, x2d, g2, b2, rw, rb)


ROW_SUBLANES = 8


def _to_token_tiles(o_ref, x):
    for j in range(ROW_SUBLANES):
        o_ref[pl.ds(j, x.shape[0], stride=ROW_SUBLANES), :] = x[:, j * LANES:(j + 1) * LANES]


def _from_token_tiles(buf_ref, n_rows):
    return [buf_ref[pl.ds(j, n_rows, stride=ROW_SUBLANES), :] for j in range(ROW_SUBLANES)]


def _row_copy(src_hbm, dst_ref, src_row, dst_row, sem):
    src = src_hbm.at[pl.ds(pl.multiple_of(src_row * ROW_SUBLANES, ROW_SUBLANES), ROW_SUBLANES)]
    dst = dst_ref.at[pl.ds(pl.multiple_of(dst_row * ROW_SUBLANES, ROW_SUBLANES), ROW_SUBLANES)]
    return pltpu.make_async_copy(src, dst, sem)


def _rows_wait(src_hbm, dst_ref, sem):
    pltpu.make_async_copy(src_hbm.at[pl.ds(0, dst_ref.shape[0])], dst_ref, sem).wait()


def _expert_kernel(blk_e_ref, n_used_ref, run_start_ref, run_parity_ref, next_e_ref, dest_ref, pad_lo_ref, pad_hi_ref,
                   x_hbm, wgu_hbm, bgu_ref, wdn_hbm, bdn_ref,
                   o_ref, row_tok_ref, xbuf, wgu_raw, wdn_raw, wgu_bf, wdn_bf, sem, wsem):
    i = pl.program_id(0)
    n_used = n_used_ref[0]
    slot = i % 2
    e = blk_e_ref[i]
    parity = run_parity_ref[i]

    def weight_copies(expert, p):
        return (pltpu.make_async_copy(wgu_hbm.at[expert], wgu_raw.at[p], wsem.at[p, 0]),
                pltpu.make_async_copy(wdn_hbm.at[expert], wdn_raw.at[p], wsem.at[p, 1]))

    def gather(block, s):
        base = block * MOE_BM

        def body(g, _):
            for u in range(ISSUE_UNROLL):
                r = g * ISSUE_UNROLL + u
                _row_copy(x_hbm, xbuf.at[s], row_tok_ref[base + r], r, sem.at[s]).start(priority=u % 2)
            return 0
        lax.fori_loop(0, MOE_BM // ISSUE_UNROLL, body, 0)

    def build_row_tokens():
        def clear_segment(s, _):
            def clear(r, _):
                row_tok_ref[r] = 0
                return 0
            lax.fori_loop(pad_lo_ref[s], pad_hi_ref[s], clear, 0)
            return 0

        lax.fori_loop(0, pad_lo_ref.shape[0], clear_segment, 0)

        def put(g, _):
            base = g * ISSUE_UNROLL
            rows = [dest_ref[base + u] for u in range(ISSUE_UNROLL)]
            tok0 = g * (ISSUE_UNROLL // TOP_K)
            for u in range(ISSUE_UNROLL):
                row_tok_ref[rows[u]] = tok0 + u // TOP_K
            return 0

        lax.fori_loop(0, dest_ref.shape[0] // ISSUE_UNROLL, put, 0)

    @pl.when(i == 0)
    def _():
        build_row_tokens()
        gather(0, 0)

        @pl.when(run_start_ref[0] == 1)
        def _():
            for c in weight_copies(e, parity):
                c.start()

    @pl.when(run_start_ref[i] == 1)
    def _():
        for c in weight_copies(e, parity):
            c.wait()
        wgu_bf[...] = wgu_raw[parity].astype(BF16)
        wdn_bf[...] = wdn_raw[parity].astype(BF16)
        e_next = next_e_ref[i]

        @pl.when(e_next >= 0)
        def _():
            for c in weight_copies(e_next, 1 - parity):
                c.start()

    gather(jnp.minimum(i + 1, pl.num_programs(0) - 1), 1 - slot)
    _rows_wait(x_hbm, xbuf.at[slot], sem.at[slot])

    @pl.when(i < n_used)
    def _():
        x = jnp.concatenate(_from_token_tiles(xbuf.at[slot], MOE_BM), axis=1).astype(BF16)
        h = jnp.dot(x, wgu_bf[...], preferred_element_type=F32) + bgu_ref[...]
        gate = jnp.minimum(h[:, :D_FF], SWIGLU_LIMIT)
        up = jnp.clip(h[:, D_FF:], -SWIGLU_LIMIT, SWIGLU_LIMIT)
        glu = gate * (1.0 / (1.0 + jnp.exp(-SWIGLU_ALPHA * gate)))
        act = ((up + 1.0) * glu).astype(BF16)
        _to_token_tiles(o_ref, jnp.dot(act, wdn_bf[...], preferred_element_type=F32) + bdn_ref[...])

    @pl.when(i >= n_used)
    def _():
        o_ref[...] = jnp.zeros_like(o_ref)

    @pl.when(i == pl.num_programs(0) - 1)
    def _():
        _rows_wait(x_hbm, xbuf.at[1 - slot], sem.at[1 - slot])


def _expert_ffn(x_tiles, dest_flat, pad_lo, pad_hi, n_rows, blk_e, n_used, w_gu, b_gu, w_dn, b_dn):
    d = w_gu.shape[1]
    n_blocks = n_rows // MOE_BM
    ne = w_gu.shape[0]
    blk = jnp.arange(n_blocks, dtype=I32)
    prev_e = jnp.concatenate([blk_e[:1], blk_e[:-1]])
    run_start = (blk < n_used[0]) & ((blk == 0) | (blk_e != prev_e))
    run_parity = ((jnp.cumsum(run_start.astype(I32)) - 1) % 2).astype(I32)
    first_start_from = lax.cummin(jnp.where(run_start, blk, n_blocks)[::-1])[::-1]
    next_start = jnp.concatenate([first_start_from[1:], jnp.full((1,), n_blocks, I32)])
    next_e = jnp.where(next_start < n_blocks, blk_e[jnp.minimum(next_start, n_blocks - 1)], -1).astype(I32)
    idx = lambda f: (lambda i, *prefetch: f(i, prefetch[0]))
    grid_spec = pltpu.PrefetchScalarGridSpec(
        num_scalar_prefetch=8,
        grid=(n_blocks,),
        in_specs=[
            pl.BlockSpec(memory_space=pl.ANY),
            pl.BlockSpec(memory_space=pl.ANY),
            pl.BlockSpec((None, 1, 2 * D_FF), idx(lambda i, be: (be[i], 0, 0))),
            pl.BlockSpec(memory_space=pl.ANY),
            pl.BlockSpec((None, 1, d), idx(lambda i, be: (be[i], 0, 0))),
        ],
        out_specs=pl.BlockSpec((MOE_BM * ROW_SUBLANES, LANES), idx(lambda i, be: (i, 0))),
        scratch_shapes=[pltpu.SMEM((n_rows,), I32), pltpu.VMEM((2, MOE_BM * ROW_SUBLANES, LANES), F32),
                        pltpu.VMEM((2, d, 2 * D_FF), F32), pltpu.VMEM((2, D_FF, d), F32),
                        pltpu.VMEM((d, 2 * D_FF), BF16), pltpu.VMEM((D_FF, d), BF16),
                        pltpu.SemaphoreType.DMA((2,)), pltpu.SemaphoreType.DMA((2, 2))],
    )
    return pl.pallas_call(
        _expert_kernel,
        grid_spec=grid_spec,
        out_shape=jax.ShapeDtypeStruct((n_rows * ROW_SUBLANES, LANES), F32),
        compiler_params=_cparams("arbitrary"),
        name="moe_experts",
    )(blk_e, n_used, run_start.astype(I32), run_parity, next_e, dest_flat, pad_lo, pad_hi,
      x_tiles, w_gu, b_gu.reshape(ne, 1, 2 * D_FF), w_dn, b_dn.reshape(ne, 1, d))


def _combine_kernel(dest_ref, y_hbm, gate_ref, x_ref, g_ref, b_ref, o_ref, buf_ref, sem):
    i = pl.program_id(0)
    t = x_ref.shape[0]
    slot = i % 2

    def gather(step, s):
        base = step * (t * TOP_K)

        def body(g, _):
            for u in range(ISSUE_UNROLL // TOP_K):
                r = g * (ISSUE_UNROLL // TOP_K) + u
                for k in range(TOP_K):
                    row = dest_ref[base + r * TOP_K + k]
                    _row_copy(y_hbm, buf_ref.at[s, k], row, r, sem.at[s]).start(priority=k % 2)
            return 0
        lax.fori_loop(0, t * TOP_K // ISSUE_UNROLL, body, 0)

    @pl.when(i == 0)
    def _():
        gather(0, 0)

    @pl.when(i + 1 < pl.num_programs(0))
    def _():
        gather(i + 1, 1 - slot)

    for k in range(TOP_K):
        _rows_wait(y_hbm, buf_ref.at[slot, k], sem.at[slot])
    gate = gate_ref[...]
    f_tiles = None
    for k in range(TOP_K):
        y_tiles = [gate[:, k:k + 1] * yt for yt in _from_token_tiles(buf_ref.at[slot, k], t)]
        f_tiles = y_tiles if f_tiles is None else [a + b for a, b in zip(f_tiles, y_tiles)]
    f = jnp.concatenate(f_tiles, axis=1)
    o_ref[...] = _layer_norm(DEEPNORM_ALPHA * x_ref[...] + f, g_ref[...], b_ref[...])


def _combine(ys, dest_flat, gates, x2d, ln_g, ln_b):
    n, d = x2d.shape
    t = GATHER_T
    g2, b2 = ln_g.reshape(1, d), ln_b.reshape(1, d)
    full = lambda a: pl.BlockSpec(a.shape, lambda i, dest: (0, 0))
    grid_spec = pltpu.PrefetchScalarGridSpec(
        num_scalar_prefetch=1,
        grid=(n // t,),
        in_specs=[pl.BlockSpec(memory_space=pl.ANY),
                  pl.BlockSpec((t, LANES), lambda i, dest: (i, 0)),
                  pl.BlockSpec((t, d), lambda i, dest: (i, 0)),
                  full(g2), full(b2)],
        out_specs=pl.BlockSpec((t, d), lambda i, dest: (i, 0)),
        scratch_shapes=[pltpu.VMEM((2, TOP_K, t * ROW_SUBLANES, LANES), F32), pltpu.SemaphoreType.DMA((2,))],
    )
    return pl.pallas_call(
        _combine_kernel,
        grid_spec=grid_spec,
        out_shape=jax.ShapeDtypeStruct((n, d), F32),
        compiler_params=_cparams("arbitrary"),
        name="moe_combine",
    )(dest_flat, ys, gates, x2d, g2, b2)


def _moe_block(x1, x1_tiles, eidx, gates, rank, counts, w_gu, b_gu, w_dn, b_dn, ln_g, ln_b):
    n = x1.shape[0]
    e_sel = eidx[:, :TOP_K]
    cnt = counts[0, :N_EXPERTS].astype(I32)
    padded = ((cnt + MOE_BM - 1) // MOE_BM) * MOE_BM
    pad_end = jnp.cumsum(padded)
    pad_start = pad_end - padded
    expert_ids = jnp.arange(N_EXPERTS, dtype=I32)
    dest = jnp.sum(jnp.where(e_sel[..., None] == expert_ids, pad_start, 0), axis=-1) + rank[:, :TOP_K]
    n_rows = ((n * TOP_K + N_EXPERTS * (MOE_BM - 1) + MOE_BM - 1) // MOE_BM) * MOE_BM
    n_blocks = n_rows // MOE_BM
    dest_flat = dest.reshape(-1)
    pad_lo = jnp.concatenate([pad_start + cnt, pad_end[-1:]]).astype(I32)
    pad_hi = jnp.concatenate([pad_end, jnp.full((1,), n_rows)]).astype(I32)
    blk_start = jnp.arange(n_blocks, dtype=I32) * MOE_BM
    blk_e = jnp.minimum(jnp.sum((pad_end[None, :] <= blk_start[:, None]).astype(I32), axis=1), N_EXPERTS - 1)
    n_used = (pad_end[-1:] // MOE_BM).astype(I32)
    ys = _expert_ffn(x1_tiles, dest_flat, pad_lo, pad_hi, n_rows, blk_e, n_used, w_gu, b_gu, w_dn, b_dn)
    return _combine(ys, dest_flat, gates, x1, ln_g, ln_b)


def _pad_cols(w, width):
    return jnp.pad(w, ((0, 0), (0, width - w.shape[1])))


def kernel(x, w_in_0, conv_w_0, w_out_0, ln_mix_g_0, ln_mix_b_0, router_w_0, router_b_0, w_gu_0, b_gu_0, w_dn_0, b_dn_0, ln_ffn_g_0, ln_ffn_b_0, w_in_1, lam_q1_1, lam_k1_1, lam_q2_1, lam_k2_1, subln_g_1, w_out_1, ln_mix_g_1, ln_mix_b_1, router_w_1, router_b_1, w_gu_1, b_gu_1, w_dn_1, b_dn_1, ln_ffn_g_1, ln_ffn_b_1):
    batch, seq, d = x.shape
    x0 = x.reshape(batch * seq, d)

    n_f32 = 3 * CONV_W
    n_attn = 3 * N_HEADS_SPARSE * D_HEAD + N_IDX_HEADS * D_IDX + D_IDX
    w_a = _pad_cols(jnp.concatenate([w_in_0[:, :n_f32], w_in_0[:, n_f32 + n_attn:]], axis=1),
                    n_f32 + LANES)
    w_b = _pad_cols(w_in_0[:, n_f32:n_f32 + n_attn], 3 * N_HEADS_SPARSE * D_HEAD + N_IDX_HEADS * D_IDX + LANES)
    ha, hb = _project(x0, [w_a.astype(BF16), w_b.astype(BF16)], [F32, BF16])
    ya = _short_conv(ha, conv_w_0, batch, seq)
    yb = _dsa_attention(ha, hb, batch, seq)
    x1, x1t, eidx, gates, rank, counts = _mix_out(ya, yb, w_out_0, x0, ln_mix_g_0, ln_mix_b_0, router_w_0, router_b_0)
    x2 = _moe_block(x1, x1t, eidx, gates, rank, counts, w_gu_0, b_gu_0, w_dn_0, b_dn_0, ln_ffn_g_0, ln_ffn_b_0)

    (hc,) = _project(x2, [w_in_1.astype(BF16)], [BF16])
    lam_vecs = jnp.stack([lam_q1_1, lam_k1_1, lam_q2_1, lam_k2_1]).astype(F32)
    yc = _diff_attention(hc, lam_vecs, subln_g_1.reshape(1, -1).astype(F32), batch, seq, 1)
    yd = _dilated_attention(hc, batch, seq)
    x3, x3t, eidx, gates, rank, counts = _mix_out(yc, yd, w_out_1, x2, ln_mix_g_1, ln_mix_b_1, router_w_1, router_b_1)
    x4 = _moe_block(x3, x3t, eidx, gates, rank, counts, w_gu_1, b_gu_1, w_dn_1, b_dn_1, ln_ffn_g_1, ln_ffn_b_1)
    return x4.reshape(batch, seq, d)
```

```python
import functools
import math

import jax
import jax.numpy as jnp
from jax import lax
from jax.experimental import pallas as pl
from jax.experimental.pallas import tpu as pltpu

F32 = jnp.float32
BF16 = jnp.bfloat16
I32 = jnp.int32

CONV_W = 512
CONV_TAPS = 3
N_HEADS_SPARSE = 8
D_HEAD = 64
N_IDX_HEADS = 8
D_IDX = 32
TOPK_LIMIT = 256
N_HEADS_DIFF = 4
N_HEADS_DIL = 8
DIL_GROUPS = ((128, 1), (512, 4), (2048, 16))
N_EXPERTS = 32
TOP_K = 4
D_FF = 1024
SWIGLU_LIMIT = 7.0
SWIGLU_ALPHA = 1.702
DEPTH = 2
DEEPNORM_ALPHA = (2 * DEPTH) ** 0.25
LN_EPS = 1e-5
RMS_EPS = 1e-5

LANES = 128
VMEM_LIMIT = 56 * 1024 * 1024
TQ = 256
CHAIN_GROUP = 4
KC = 512
ROW_TILE = 512
MOE_BM = 256
GATHER_T = 256
ISSUE_UNROLL = 8
COUNT_ROWS = 64
LOG2_E = math.log2(math.e)
INT_MIN = -2 ** 31


def _alibi_slopes(n):
    return [2.0 ** (-8.0 * (h + 1) / n) for h in range(n)]


def _cparams(*sem):
    return pltpu.CompilerParams(dimension_semantics=sem, vmem_limit_bytes=VMEM_LIMIT)


def _layer_norm(z, g, b):
    mu = jnp.mean(z, axis=-1, keepdims=True)
    zc = z - mu
    var = jnp.mean(zc * zc, axis=-1, keepdims=True)
    return zc * lax.rsqrt(var + LN_EPS) * g + b


def _proj_kernel(*refs, n_out):
    x_ref = refs[0]
    w_refs = refs[1:1 + n_out]
    o_refs = refs[1 + n_out:]
    xb = x_ref[...].astype(BF16)
    for w_ref, o_ref in zip(w_refs, o_refs):
        o_ref[...] = jnp.dot(xb, w_ref[...], preferred_element_type=F32).astype(o_ref.dtype)


def _project(x2d, weights, out_dtypes):
    n, d = x2d.shape
    n_out = len(weights)
    in_specs = [pl.BlockSpec((ROW_TILE, d), lambda i: (i, 0))]
    in_specs += [pl.BlockSpec(w.shape, lambda i: (0, 0)) for w in weights]
    out_specs = [pl.BlockSpec((ROW_TILE, w.shape[1]), lambda i: (i, 0)) for w in weights]
    out_shape = [jax.ShapeDtypeStruct((n, w.shape[1]), dt) for w, dt in zip(weights, out_dtypes)]
    return pl.pallas_call(
        functools.partial(_proj_kernel, n_out=n_out),
        grid=(n // ROW_TILE,),
        in_specs=in_specs, out_specs=out_specs, out_shape=out_shape,
        compiler_params=_cparams("parallel"),
        name="in_proj",
    )(x2d, *weights)


def _conv_kernel(gb_ref, gc_ref, xa_ref, w_ref, o_ref, prev_ref):
    j = pl.program_id(1)
    t = gb_ref.shape[0]

    @pl.when(j == 0)
    def _():
        prev_ref[...] = jnp.zeros_like(prev_ref)

    z = gc_ref[...] * xa_ref[...]
    row = lax.broadcasted_iota(I32, z.shape, 0)
    prev2 = prev_ref[0:1, :]
    prev1 = prev_ref[1:2, :]
    z1 = jnp.where(row == 0, prev1, pltpu.roll(z, 1, 0))
    z2 = jnp.where(row == 0, prev2, jnp.where(row == 1, prev1, pltpu.roll(z, 2, 0)))
    w = w_ref[...]
    y = w[0:1, :] * z + w[1:2, :] * z1 + w[2:3, :] * z2
    o_ref[...] = (gb_ref[...] * y).astype(o_ref.dtype)
    prev_ref[0:1, :] = z[t - 2:t - 1, :]
    prev_ref[1:2, :] = z[t - 1:t, :]


def _short_conv(ha, conv_w, batch, seq):
    n = ha.shape[0]
    t = min(512, seq)
    nj = seq // t
    spec = lambda c: pl.BlockSpec((t, CONV_W), lambda b, j, c=c: (b * nj + j, c))
    return pl.pallas_call(
        _conv_kernel,
        grid=(batch, nj),
        in_specs=[spec(0), spec(1), spec(2), pl.BlockSpec((CONV_TAPS, CONV_W), lambda b, j: (0, 0))],
        out_specs=pl.BlockSpec((t, CONV_W), lambda b, j: (b * nj + j, 0)),
        out_shape=jax.ShapeDtypeStruct((n, CONV_W), BF16),
        scratch_shapes=[pltpu.VMEM((8, CONV_W), F32)],
        compiler_params=_cparams("arbitrary", "arbitrary"),
        name="short_conv",
    )(ha, ha, ha, conv_w)


def _lane_tiles(x):
    return [x[:, u * LANES:(u + 1) * LANES] for u in range(x.shape[1] // LANES)]


def _flash_chains(chains, k_ref, v_ref, dmat_ref, amat_ref, c_lo, c_hi, scratch):
    qm_ref, s_ref, mx_ref, lp_ref, l_ref, acc_ref = scratch
    mx_ref[...] = jnp.full(mx_ref.shape, -jnp.inf, F32)
    lp_ref[...] = jnp.zeros(lp_ref.shape, F32)
    acc_ref[...] = jnp.zeros(acc_ref.shape, F32)

    for g0 in range(0, len(chains), CHAIN_GROUP):
        group = list(enumerate(chains))[g0:g0 + CHAIN_GROUP]

        def pass_a(c, _, group=group, g0=g0):
            k0 = pl.multiple_of(c * KC, KC)
            for n, (k_tile, _, slope) in group:
                kc = k_ref[pl.ds(k0, KC), k_tile * LANES:(k_tile + 1) * LANES]
                s = lax.dot_general(qm_ref[n], kc, (((1,), (1,)), ((), ())), preferred_element_type=F32)
                s = s * LOG2_E - (slope * LOG2_E) * dmat_ref[:, pl.ds(k0, KC)]
                if amat_ref is not None:
                    s = s + amat_ref[:, pl.ds(k0, KC)]
                s_ref[n - g0, :, pl.ds(k0, KC)] = s
                part = mx_ref[n]
                for t in _lane_tiles(s):
                    part = jnp.maximum(part, t)
                mx_ref[n] = part
            return 0

        lax.fori_loop(c_lo, c_hi, pass_a, 0)

        for n, _ in group:
            row_max = jnp.max(mx_ref[n], axis=1, keepdims=True)
            mx_ref[n] = jnp.broadcast_to(row_max, (TQ, LANES))

        def pass_b(c, _, group=group, g0=g0):
            k0 = pl.multiple_of(c * KC, KC)
            for n, (_, v_tile, _) in group:
                row_max = mx_ref[n]
                p_tiles = [jnp.exp2(t - row_max) for t in _lane_tiles(s_ref[n - g0, :, pl.ds(k0, KC)])]
                part = lp_ref[n]
                for t in p_tiles:
                    part = part + t
                lp_ref[n] = part
                p = jnp.concatenate(p_tiles, axis=1).astype(BF16)
                vc = v_ref[pl.ds(k0, KC), v_tile * LANES:(v_tile + 1) * LANES]
                acc_ref[n] = acc_ref[n] + jnp.dot(p, vc, preferred_element_type=F32)
            return 0

        lax.fori_loop(c_lo, c_hi, pass_b, 0)

    for n in range(len(chains)):
        l_ref[n] = jnp.sum(lp_ref[n], axis=1, keepdims=True)


def _split_head_pairs(q_ref, qm_ref, n_tiles, scale):
    low = lax.broadcasted_iota(I32, (TQ, LANES), 1) < D_HEAD
    for j in range(n_tiles):
        qt = q_ref[:, j * LANES:(j + 1) * LANES] * scale
        qm_ref[2 * j] = jnp.where(low, qt, 0).astype(BF16)
        qm_ref[2 * j + 1] = jnp.where(low, 0, qt).astype(BF16)


def _merge_head_pairs(o_ref, l_ref, acc_ref, n_tiles):
    low = lax.broadcasted_iota(I32, (TQ, LANES), 1) < D_HEAD
    for j in range(n_tiles):
        out = jnp.where(low, acc_ref[2 * j] / l_ref[2 * j], acc_ref[2 * j + 1] / l_ref[2 * j + 1])
        o_ref[:, j * LANES:(j + 1) * LANES] = out.astype(o_ref.dtype)


def _flash_scratch(n_chains, seq):
    return [pltpu.VMEM((n_chains, TQ, LANES), BF16), pltpu.VMEM((CHAIN_GROUP, TQ, seq), F32),
            pltpu.VMEM((n_chains, TQ, LANES), F32), pltpu.VMEM((n_chains, TQ, LANES), F32),
            pltpu.VMEM((n_chains, TQ, 1), F32), pltpu.VMEM((n_chains, TQ, LANES), F32)]


def _dsa_kernel(q_ref, k_ref, v_ref, iq_ref, ik_ref, iw_ref, o_ref, sc_ref, sct_ref, dmat_ref, *scratch, k_sel):
    i = pl.program_id(1)
    seq_len = sc_ref.shape[1]
    t0 = i * TQ
    n_chunks = (t0 + TQ + KC - 1) // KC
    idx_scale = (D_IDX ** -0.5) * (N_IDX_HEADS ** -0.5)
    row = t0 + lax.broadcasted_iota(I32, (TQ, KC), 0)
    col_in_chunk = lax.broadcasted_iota(I32, (TQ, KC), 1)

    iq = iq_ref[...]
    iw = iw_ref[:, 0:N_IDX_HEADS]

    def score_body(c, _):
        k0 = pl.multiple_of(c * KC, KC)
        ik = ik_ref[pl.ds(k0, KC), 0:D_IDX]
        sc = jnp.zeros((TQ, KC), F32)
        for h in range(N_IDX_HEADS):
            rel = lax.dot_general(iq[:, h * D_IDX:(h + 1) * D_IDX], ik,
                                  (((1,), (1,)), ((), ())), preferred_element_type=F32)
            sc = sc + iw[:, h:h + 1] * jnp.maximum(rel, 0.0)
        causal = (k0 + col_in_chunk) <= row
        sc = jnp.where(causal, sc * idx_scale, -jnp.inf)
        sc_ref[:, pl.ds(k0, KC)] = sc
        sct_ref[pl.ds(k0, KC), :] = sc.T
        return 0

    lax.fori_loop(0, n_chunks, score_body, 0)

    def code_to_float(code):
        return lax.bitcast_convert_type(jnp.where(code < 0, code ^ 0x7FFFFFFF, code), F32)

    def count(pred):
        def body(c, acc):
            k0 = pl.multiple_of(c * KC, KC)
            hit = jnp.where(pred(sct_ref[pl.ds(k0, KC), :]), 1.0, 0.0)
            return acc + jnp.sum(hit.reshape(KC // COUNT_ROWS, COUNT_ROWS, TQ), axis=0)
        acc = lax.fori_loop(0, n_chunks, body, jnp.zeros((COUNT_ROWS, TQ), F32))
        return jnp.sum(acc, axis=0, keepdims=True)

    kf = float(k_sel)
    has_k = count(lambda t: t > -jnp.inf) >= kf
    code = jnp.where(count(lambda t: t >= 0.0) >= kf, 0, INT_MIN).astype(I32)

    def bit_body(b, code):
        cand = code + lax.shift_left(jnp.int32(1), 30 - b)
        cand_f = code_to_float(cand)
        return jnp.where(count(lambda t: t >= cand_f) >= kf, cand, code)

    code = lax.fori_loop(0, 31, bit_body, code)
    thr_q = jnp.where(has_k, code_to_float(code), -jnp.inf)
    need_q = jnp.where(has_k, kf - count(lambda t: t > thr_q), float(seq_len))
    thr = jnp.broadcast_to(thr_q, (LANES, TQ)).T
    need = jnp.broadcast_to(need_q, (LANES, TQ)).T

    tri = (lax.broadcasted_iota(I32, (KC, KC), 0) <= lax.broadcasted_iota(I32, (KC, KC), 1)).astype(BF16)
    thr_w = jnp.concatenate([thr] * (KC // LANES), axis=1)
    need_w = jnp.concatenate([need] * (KC // LANES), axis=1)

    def mask_body(c, ties_before):
        k0 = pl.multiple_of(c * KC, KC)
        sc = sc_ref[:, pl.ds(k0, KC)]
        eq = sc == thr_w
        eqf = jnp.where(eq, 1.0, 0.0)
        rank = ties_before + jnp.dot(eqf.astype(BF16), tri, preferred_element_type=F32)
        dist = row - (k0 + col_in_chunk)
        sel = ((sc > thr_w) | (eq & (rank <= need_w))) & (dist >= 0)
        dmat_ref[:, pl.ds(k0, KC)] = jnp.where(sel, dist.astype(F32), jnp.inf)
        return ties_before + jnp.sum(eqf, axis=1, keepdims=True)

    lax.fori_loop(0, n_chunks, mask_body, jnp.zeros((TQ, 1), F32))

    slopes = _alibi_slopes(N_HEADS_SPARSE)
    n_tiles = N_HEADS_SPARSE * D_HEAD // LANES
    _split_head_pairs(q_ref, scratch[0], n_tiles, D_HEAD ** -0.5)
    chains = [(h // 2, h // 2, slopes[h]) for h in range(N_HEADS_SPARSE)]
    _flash_chains(chains, k_ref, v_ref, dmat_ref, None, 0, n_chunks, scratch)
    _merge_head_pairs(o_ref, scratch[-2], scratch[-1], n_tiles)


def _dsa_attention(ha, hb, batch, seq):
    n = hb.shape[0]
    nq = seq // TQ
    k_sel = min(TOPK_LIMIT, seq // 4)
    width = N_HEADS_SPARSE * D_HEAD
    iq_w = N_IDX_HEADS * D_IDX
    return pl.pallas_call(
        functools.partial(_dsa_kernel, k_sel=k_sel),
        grid=(batch, nq),
        in_specs=[
            pl.BlockSpec((TQ, width), lambda b, i: (b * nq + i, 0)),
            pl.BlockSpec((seq, width), lambda b, i: (b, 1)),
            pl.BlockSpec((seq, width), lambda b, i: (b, 2)),
            pl.BlockSpec((TQ, iq_w), lambda b, i: (b * nq + i, 3 * width // iq_w)),
            pl.BlockSpec((seq, LANES), lambda b, i: (b, (3 * width + iq_w) // LANES)),
            pl.BlockSpec((TQ, LANES), lambda b, i: (b * nq + i, 3 * CONV_W // LANES)),
        ],
        out_specs=pl.BlockSpec((TQ, width), lambda b, i: (b * nq + i, 0)),
        out_shape=jax.ShapeDtypeStruct((n, width), BF16),
        scratch_shapes=[pltpu.VMEM((TQ, seq), F32), pltpu.VMEM((seq, TQ), F32), pltpu.VMEM((TQ, seq), F32)]
        + _flash_scratch(N_HEADS_SPARSE, seq),
        compiler_params=_cparams("arbitrary", "arbitrary"),
        name="dsa_attention",
    )(hb, hb, hb, hb, hb, ha)


def _diff_kernel(q_ref, k_ref, v_ref, lam_ref, g_ref, o_ref, dmat_ref, *scratch, lam_init):
    i = pl.program_id(1)
    t0 = i * TQ
    n_chunks = (t0 + TQ + KC - 1) // KC
    row = t0 + lax.broadcasted_iota(I32, (TQ, KC), 0)
    col_in_chunk = lax.broadcasted_iota(I32, (TQ, KC), 1)

    def mask_body(c, _):
        k0 = pl.multiple_of(c * KC, KC)
        dist = row - (k0 + col_in_chunk)
        dmat_ref[:, pl.ds(k0, KC)] = jnp.where(dist >= 0, dist.astype(F32), jnp.inf)
        return 0

    lax.fori_loop(0, n_chunks, mask_body, 0)

    lv = lam_ref[...]
    lam = (jnp.exp(jnp.sum(lv[0:1, :] * lv[1:2, :], axis=1, keepdims=True))
           - jnp.exp(jnp.sum(lv[2:3, :] * lv[3:4, :], axis=1, keepdims=True)) + lam_init)

    slopes = _alibi_slopes(N_HEADS_DIFF)
    _split_head_pairs(q_ref, scratch[0], N_HEADS_DIFF, D_HEAD ** -0.5)
    chains = [(n // 2, n // 2, slopes[n // 2]) for n in range(2 * N_HEADS_DIFF)]
    _flash_chains(chains, k_ref, v_ref, dmat_ref, None, 0, n_chunks, scratch)
    l_ref, acc_ref = scratch[-2], scratch[-1]
    g = g_ref[...]
    for h in range(N_HEADS_DIFF):
        of = acc_ref[2 * h] / l_ref[2 * h] - lam * (acc_ref[2 * h + 1] / l_ref[2 * h + 1])
        of = of * lax.rsqrt(jnp.mean(of * of, axis=1, keepdims=True) + RMS_EPS) * g
        o_ref[:, h * LANES:(h + 1) * LANES] = (of * (1.0 - lam_init)).astype(o_ref.dtype)


def _diff_attention(hb, lam_vecs, subln_g, batch, seq, layer):
    n = hb.shape[0]
    nq = seq // TQ
    width = N_HEADS_DIFF * 2 * D_HEAD
    lam_init = 0.8 - 0.6 * math.exp(-0.3 * layer)
    return pl.pallas_call(
        functools.partial(_diff_kernel, lam_init=lam_init),
        grid=(batch, nq),
        in_specs=[
            pl.BlockSpec((TQ, width), lambda b, i: (b * nq + i, 0)),
            pl.BlockSpec((seq, width), lambda b, i: (b, 1)),
            pl.BlockSpec((seq, width), lambda b, i: (b, 2)),
            pl.BlockSpec(lam_vecs.shape, lambda b, i: (0, 0)),
            pl.BlockSpec(subln_g.shape, lambda b, i: (0, 0)),
        ],
        out_specs=pl.BlockSpec((TQ, width), lambda b, i: (b * nq + i, 0)),
        out_shape=jax.ShapeDtypeStruct((n, width), BF16),
        scratch_shapes=[pltpu.VMEM((TQ, seq), F32)] + _flash_scratch(2 * N_HEADS_DIFF, seq),
        compiler_params=_cparams("arbitrary", "arbitrary"),
        name="diff_attention",
    )(hb, hb, hb, lam_vecs, subln_g)


def _dilated_kernel(q_ref, k_ref, v_ref, o_ref, dmat_ref, amat_ref, *scratch):
    i = pl.program_id(1)
    t0 = i * TQ
    w_max = max(w for w, _ in DIL_GROUPS)
    c_lo = jnp.maximum(t0 - w_max, 0) // KC
    c_hi = (t0 + TQ + KC - 1) // KC
    row = t0 + lax.broadcasted_iota(I32, (TQ, KC), 0)
    col_in_chunk = lax.broadcasted_iota(I32, (TQ, KC), 1)

    def mask_body(c, _):
        k0 = pl.multiple_of(c * KC, KC)
        dist = row - (k0 + col_in_chunk)
        mult = jnp.zeros((TQ, KC), F32)
        for w, d in DIL_GROUPS:
            member = (dist >= 0) & (dist <= w) & ((dist & (d - 1)) == 0)
            mult = mult + jnp.where(member, 1.0, 0.0)
        on = mult > 0.0
        dmat_ref[:, pl.ds(k0, KC)] = jnp.where(on, dist.astype(F32), jnp.inf)
        amat_ref[:, pl.ds(k0, KC)] = jnp.log2(jnp.where(on, mult, 1.0))
        return 0

    lax.fori_loop(c_lo, c_hi, mask_body, 0)

    slopes = _alibi_slopes(N_HEADS_DIL)
    n_tiles = N_HEADS_DIL * D_HEAD // LANES
    _split_head_pairs(q_ref, scratch[0], n_tiles, D_HEAD ** -0.5)
    chains = [(h // 2, h // 2, slopes[h]) for h in range(N_HEADS_DIL)]
    _flash_chains(chains, k_ref, v_ref, dmat_ref, amat_ref, c_lo, c_hi, scratch)
    _merge_head_pairs(o_ref, scratch[-2], scratch[-1], n_tiles)


def _dilated_attention(hb, batch, seq):
    n = hb.shape[0]
    nq = seq // TQ
    width = N_HEADS_DIL * D_HEAD
    return pl.pallas_call(
        _dilated_kernel,
        grid=(batch, nq),
        in_specs=[
            pl.BlockSpec((TQ, width), lambda b, i: (b * nq + i, 3)),
            pl.BlockSpec((seq, width), lambda b, i: (b, 4)),
            pl.BlockSpec((seq, width), lambda b, i: (b, 5)),
        ],
        out_specs=pl.BlockSpec((TQ, width), lambda b, i: (b * nq + i, 0)),
        out_shape=jax.ShapeDtypeStruct((n, width), BF16),
        scratch_shapes=[pltpu.VMEM((TQ, seq), F32), pltpu.VMEM((TQ, seq), F32)] + _flash_scratch(N_HEADS_DIL, seq),
        compiler_params=_cparams("arbitrary", "arbitrary"),
        name="dilated_attention",
    )(hb, hb, hb)


def _mix_out_kernel(ya_ref, yb_ref, wa_ref, wb_ref, x_ref, g_ref, b_ref, rw_ref, rb_ref,
                    x1_ref, x1t_ref, eidx_ref, gate_ref, rank_ref, cnt_ref, carry_ref):
    step = pl.program_id(0)

    @pl.when(step == 0)
    def _():
        carry_ref[...] = jnp.zeros_like(carry_ref)

    m = (jnp.dot(ya_ref[...], wa_ref[...], preferred_element_type=F32)
         + jnp.dot(yb_ref[...], wb_ref[...], preferred_element_type=F32))
    x1 = _layer_norm(DEEPNORM_ALPHA * x_ref[...] + m, g_ref[...], b_ref[...])
    x1_ref[...] = x1
    _to_token_tiles(x1t_ref, x1)

    t = x1.shape[0]
    logits = jnp.dot(x1.astype(BF16), rw_ref[...], preferred_element_type=F32) + rb_ref[...]
    lane = lax.broadcasted_iota(I32, (t, LANES), 1)
    vals, idxs = [], []
    lg = logits
    for _ in range(TOP_K):
        mx = jnp.max(lg, axis=1, keepdims=True)
        ix = jnp.min(jnp.where(lg == mx, lane, LANES), axis=1, keepdims=True)
        vals.append(mx)
        idxs.append(ix)
        lg = jnp.where(lane == ix, -jnp.inf, lg)
    exps = [jnp.exp(v - vals[0]) for v in vals]
    denom = exps[0]
    for e in exps[1:]:
        denom = denom + e

    onehot = jnp.zeros((t, LANES), F32)
    for ix in idxs:
        onehot = onehot + jnp.where(lane == ix, 1.0, 0.0)
    strict = (lax.broadcasted_iota(I32, (t, t), 1) < lax.broadcasted_iota(I32, (t, t), 0)).astype(BF16)
    before = jnp.dot(strict, onehot.astype(BF16), preferred_element_type=F32) + carry_ref[...]

    eidx = jnp.zeros((t, LANES), I32)
    gate = jnp.zeros((t, LANES), F32)
    rank = jnp.zeros((t, LANES), I32)
    for k in range(TOP_K):
        rk = jnp.sum(jnp.where(lane == idxs[k], before, 0.0), axis=1, keepdims=True)
        eidx = jnp.where(lane == k, idxs[k], eidx)
        gate = jnp.where(lane == k, exps[k] / denom, gate)
        rank = jnp.where(lane == k, rk.astype(I32), rank)
    eidx_ref[...] = eidx
    gate_ref[...] = gate
    rank_ref[...] = rank
    carry_ref[...] = carry_ref[...] + jnp.sum(onehot, axis=0, keepdims=True)
    cnt_ref[...] = carry_ref[...]


def _mix_out(ya, yb, w_out, x2d, ln_g, ln_b, router_w, router_b):
    n, d = x2d.shape
    half = ya.shape[1]
    wa = w_out[:half].astype(BF16)
    wb = w_out[half:].astype(BF16)
    rw = jnp.zeros((d, LANES), BF16).at[:, :N_EXPERTS].set(router_w.astype(BF16))
    rb = jnp.full((1, LANES), -jnp.inf, F32).at[0, :N_EXPERTS].set(router_b)
    t = ROW_TILE
    full = lambda a: pl.BlockSpec(a.shape, lambda i: (0, 0))
    rows = lambda w: pl.BlockSpec((t, w), lambda i: (i, 0))
    g2, b2 = ln_g.reshape(1, d), ln_b.reshape(1, d)
    return pl.pallas_call(
        _mix_out_kernel,
        grid=(n // t,),
        in_specs=[rows(half), rows(half), full(wa), full(wb), rows(d), full(g2), full(b2), full(rw), full(rb)],
        out_specs=[rows(d), pl.BlockSpec((t * ROW_SUBLANES, LANES), lambda i: (i, 0)),
                   rows(LANES), rows(LANES), rows(LANES), pl.BlockSpec((1, LANES), lambda i: (0, 0))],
        out_shape=[jax.ShapeDtypeStruct((n, d), F32), jax.ShapeDtypeStruct((n * ROW_SUBLANES, LANES), F32),
                   jax.ShapeDtypeStruct((n, LANES), I32),
                   jax.ShapeDtypeStruct((n, LANES), F32), jax.ShapeDtypeStruct((n, LANES), I32),
                   jax.ShapeDtypeStruct((1, LANES), F32)],
        scratch_shapes=[pltpu.VMEM((1, LANES), F32)],
        compiler_params=_cparams("arbitrary"),
        name="mix_out_router",
    )(ya, yb, wa, wb, x2d, g2, b2, rw, rb)


ROW_SUBLANES = 8


def _to_token_tiles(o_ref, x):
    for j in range(ROW_SUBLANES):
        o_ref[pl.ds(j, x.shape[0], stride=ROW_SUBLANES), :] = x[:, j * LANES:(j + 1) * LANES]


def _from_token_tiles(buf_ref, n_rows):
    return [buf_ref[pl.ds(j, n_rows, stride=ROW_SUBLANES), :] for j in range(ROW_SUBLANES)]


def _row_copy(src_hbm, dst_ref, src_row, dst_row, sem):
    src = src_hbm.at[pl.ds(pl.multiple_of(src_row * ROW_SUBLANES, ROW_SUBLANES), ROW_SUBLANES)]
    dst = dst_ref.at[pl.ds(pl.multiple_of(dst_row * ROW_SUBLANES, ROW_SUBLANES), ROW_SUBLANES)]
    return pltpu.make_async_copy(src, dst, sem)


def _rows_wait(src_hbm, dst_ref, sem):
    pltpu.make_async_copy(src_hbm.at[pl.ds(0, dst_ref.shape[0])], dst_ref, sem).wait()


def _dispatch_kernel(dest_ref, pad_lo_ref, pad_hi_ref, x_ref, o_hbm, zero_ref, sem, zsem):
    i = pl.program_id(0)
    t = x_ref.shape[0] // ROW_SUBLANES
    zero_rows = zero_ref.shape[0] // ROW_SUBLANES

    def zero_fill(s):
        last = pl.multiple_of((pad_hi_ref[s] - zero_rows) * ROW_SUBLANES, ROW_SUBLANES)
        return pltpu.make_async_copy(zero_ref, o_hbm.at[pl.ds(last, zero_ref.shape[0])], zsem)

    @pl.when(i == 0)
    def _():
        zero_ref[...] = jnp.zeros_like(zero_ref)
        for wait in (False, True):
            def body(s, _, wait=wait):
                @pl.when(pad_lo_ref[s] < pad_hi_ref[s])
                def _():
                    if wait:
                        zero_fill(s).wait()
                    else:
                        zero_fill(s).start()
                return 0
            lax.fori_loop(0, pad_lo_ref.shape[0], body, 0)

    base = i * (t * TOP_K)

    def body(g, _):
        for u in range(ISSUE_UNROLL // TOP_K):
            r = g * (ISSUE_UNROLL // TOP_K) + u
            src = x_ref.at[pl.ds(pl.multiple_of(r * ROW_SUBLANES, ROW_SUBLANES), ROW_SUBLANES)]
            for k in range(TOP_K):
                row = dest_ref[base + r * TOP_K + k]
                dst = o_hbm.at[pl.ds(pl.multiple_of(row * ROW_SUBLANES, ROW_SUBLANES), ROW_SUBLANES)]
                pltpu.make_async_copy(src, dst, sem).start(priority=k % 2)
        return 0

    lax.fori_loop(0, t * TOP_K // ISSUE_UNROLL, body, 0)
    for _ in range(TOP_K):
        pltpu.make_async_copy(x_ref, o_hbm.at[pl.ds(0, x_ref.shape[0])], sem).wait()


def _dispatch(x_tiles, dest_flat, pad_lo, pad_hi, n_rows):
    n = x_tiles.shape[0] // ROW_SUBLANES
    t = GATHER_T
    grid_spec = pltpu.PrefetchScalarGridSpec(
        num_scalar_prefetch=3,
        grid=(n // t,),
        in_specs=[pl.BlockSpec((t * ROW_SUBLANES, LANES), lambda i, *_: (i, 0))],
        out_specs=pl.BlockSpec(memory_space=pl.ANY),
        scratch_shapes=[pltpu.VMEM((MOE_BM * ROW_SUBLANES, LANES), F32),
                        pltpu.SemaphoreType.DMA(()), pltpu.SemaphoreType.DMA(())],
    )
    return pl.pallas_call(
        _dispatch_kernel,
        grid_spec=grid_spec,
        out_shape=jax.ShapeDtypeStruct((n_rows * ROW_SUBLANES, LANES), F32),
        compiler_params=_cparams("arbitrary"),
        name="moe_dispatch",
    )(dest_flat, pad_lo, pad_hi, x_tiles)


def _expert_kernel(blk_e_ref, n_used_ref, run_start_ref, run_parity_ref, next_e_ref,
                   x_ref, wgu_hbm, bgu_ref, wdn_hbm, bdn_ref,
                   o_ref, wgu_raw, wdn_raw, wgu_bf, wdn_bf, wsem):
    i = pl.program_id(0)
    n_used = n_used_ref[0]
    e = blk_e_ref[i]
    parity = run_parity_ref[i]

    def weight_copies(expert, p):
        return (pltpu.make_async_copy(wgu_hbm.at[expert], wgu_raw.at[p], wsem.at[p, 0]),
                pltpu.make_async_copy(wdn_hbm.at[expert], wdn_raw.at[p], wsem.at[p, 1]))

    @pl.when((i == 0) & (run_start_ref[0] == 1))
    def _():
        for c in weight_copies(e, parity):
            c.start()

    @pl.when(run_start_ref[i] == 1)
    def _():
        for c in weight_copies(e, parity):
            c.wait()
        wgu_bf[...] = wgu_raw[parity].astype(BF16)
        wdn_bf[...] = wdn_raw[parity].astype(BF16)
        e_next = next_e_ref[i]

        @pl.when(e_next >= 0)
        def _():
            for c in weight_copies(e_next, 1 - parity):
                c.start()

    @pl.when(i < n_used)
    def _():
        x = jnp.concatenate(_from_token_tiles(x_ref, MOE_BM), axis=1).astype(BF16)
        h = jnp.dot(x, wgu_bf[...], preferred_element_type=F32) + bgu_ref[...]
        gate = jnp.minimum(h[:, :D_FF], SWIGLU_LIMIT)
        up = jnp.clip(h[:, D_FF:], -SWIGLU_LIMIT, SWIGLU_LIMIT)
        glu = gate * (1.0 / (1.0 + jnp.exp(-SWIGLU_ALPHA * gate)))
        act = ((up + 1.0) * glu).astype(BF16)
        _to_token_tiles(o_ref, jnp.dot(act, wdn_bf[...], preferred_element_type=F32) + bdn_ref[...])

    @pl.when(i >= n_used)
    def _():
        o_ref[...] = jnp.zeros_like(o_ref)


def _expert_ffn(xs, blk_e, n_used, w_gu, b_gu, w_dn, b_dn):
    n_rows = xs.shape[0] // ROW_SUBLANES
    d = w_gu.shape[1]
    n_blocks = n_rows // MOE_BM
    ne = w_gu.shape[0]
    blk = jnp.arange(n_blocks, dtype=I32)
    prev_e = jnp.concatenate([blk_e[:1], blk_e[:-1]])
    run_start = (blk < n_used[0]) & ((blk == 0) | (blk_e != prev_e))
    run_parity = ((jnp.cumsum(run_start.astype(I32)) - 1) % 2).astype(I32)
    first_start_from = lax.cummin(jnp.where(run_start, blk, n_blocks)[::-1])[::-1]
    next_start = jnp.concatenate([first_start_from[1:], jnp.full((1,), n_blocks, I32)])
    next_e = jnp.where(next_start < n_blocks, blk_e[jnp.minimum(next_start, n_blocks - 1)], -1).astype(I32)
    idx = lambda f: (lambda i, *prefetch: f(i, prefetch[0]))
    grid_spec = pltpu.PrefetchScalarGridSpec(
        num_scalar_prefetch=5,
        grid=(n_blocks,),
        in_specs=[
            pl.BlockSpec((MOE_BM * ROW_SUBLANES, LANES),
                         lambda i, be, nu, *_: (jnp.maximum(jnp.minimum(i, nu[0] - 1), 0), 0)),
            pl.BlockSpec(memory_space=pl.ANY),
            pl.BlockSpec((None, 1, 2 * D_FF), idx(lambda i, be: (be[i], 0, 0))),
            pl.BlockSpec(memory_space=pl.ANY),
            pl.BlockSpec((None, 1, d), idx(lambda i, be: (be[i], 0, 0))),
        ],
        out_specs=pl.BlockSpec((MOE_BM * ROW_SUBLANES, LANES), idx(lambda i, be: (i, 0))),
        scratch_shapes=[pltpu.VMEM((2, d, 2 * D_FF), F32), pltpu.VMEM((2, D_FF, d), F32),
                        pltpu.VMEM((d, 2 * D_FF), BF16), pltpu.VMEM((D_FF, d), BF16),
                        pltpu.SemaphoreType.DMA((2, 2))],
    )
    return pl.pallas_call(
        _expert_kernel,
        grid_spec=grid_spec,
        out_shape=jax.ShapeDtypeStruct((n_rows * ROW_SUBLANES, LANES), F32),
        compiler_params=_cparams("arbitrary"),
        name="moe_experts",
    )(blk_e, n_used, run_start.astype(I32), run_parity, next_e,
      xs, w_gu, b_gu.reshape(ne, 1, 2 * D_FF), w_dn, b_dn.reshape(ne, 1, d))


def _combine_kernel(dest_ref, y_hbm, gate_ref, x_ref, g_ref, b_ref, o_ref, buf_ref, sem):
    i = pl.program_id(0)
    t = x_ref.shape[0]
    slot = i % 2

    def gather(step, s):
        base = step * (t * TOP_K)

        def body(g, _):
            for u in range(ISSUE_UNROLL // TOP_K):
                r = g * (ISSUE_UNROLL // TOP_K) + u
                for k in range(TOP_K):
                    row = dest_ref[base + r * TOP_K + k]
                    _row_copy(y_hbm, buf_ref.at[s, k], row, r, sem.at[s]).start(priority=k % 2)
            return 0
        lax.fori_loop(0, t * TOP_K // ISSUE_UNROLL, body, 0)

    @pl.when(i == 0)
    def _():
        gather(0, 0)

    @pl.when(i + 1 < pl.num_programs(0))
    def _():
        gather(i + 1, 1 - slot)

    for k in range(TOP_K):
        _rows_wait(y_hbm, buf_ref.at[slot, k], sem.at[slot])
    gate = gate_ref[...]
    f_tiles = None
    for k in range(TOP_K):
        y_tiles = [gate[:, k:k + 1] * yt for yt in _from_token_tiles(buf_ref.at[slot, k], t)]
        f_tiles = y_tiles if f_tiles is None else [a + b for a, b in zip(f_tiles, y_tiles)]
    f = jnp.concatenate(f_tiles, axis=1)
    o_ref[...] = _layer_norm(DEEPNORM_ALPHA * x_ref[...] + f, g_ref[...], b_ref[...])


def _combine(ys, dest_flat, gates, x2d, ln_g, ln_b):
    n, d = x2d.shape
    t = GATHER_T
    g2, b2 = ln_g.reshape(1, d), ln_b.reshape(1, d)
    full = lambda a: pl.BlockSpec(a.shape, lambda i, dest: (0, 0))
    grid_spec = pltpu.PrefetchScalarGridSpec(
        num_scalar_prefetch=1,
        grid=(n // t,),
        in_specs=[pl.BlockSpec(memory_space=pl.ANY),
                  pl.BlockSpec((t, LANES), lambda i, dest: (i, 0)),
                  pl.BlockSpec((t, d), lambda i, dest: (i, 0)),
                  full(g2), full(b2)],
        out_specs=pl.BlockSpec((t, d), lambda i, dest: (i, 0)),
        scratch_shapes=[pltpu.VMEM((2, TOP_K, t * ROW_SUBLANES, LANES), F32), pltpu.SemaphoreType.DMA((2,))],
    )
    return pl.pallas_call(
        _combine_kernel,
        grid_spec=grid_spec,
        out_shape=jax.ShapeDtypeStruct((n, d), F32),
        compiler_params=_cparams("arbitrary"),
        name="moe_combine",
    )(dest_flat, ys, gates, x2d, g2, b2)


def _moe_block(x1, x1_tiles, eidx, gates, rank, counts, w_gu, b_gu, w_dn, b_dn, ln_g, ln_b):
    n = x1.shape[0]
    e_sel = eidx[:, :TOP_K]
    cnt = counts[0, :N_EXPERTS].astype(I32)
    padded = ((cnt + MOE_BM - 1) // MOE_BM) * MOE_BM
    pad_end = jnp.cumsum(padded)
    pad_start = pad_end - padded
    expert_ids = jnp.arange(N_EXPERTS, dtype=I32)
    dest = jnp.sum(jnp.where(e_sel[..., None] == expert_ids, pad_start, 0), axis=-1) + rank[:, :TOP_K]
    n_rows = ((n * TOP_K + N_EXPERTS * (MOE_BM - 1) + MOE_BM - 1) // MOE_BM) * MOE_BM
    n_blocks = n_rows // MOE_BM
    dest_flat = dest.reshape(-1)
    tail_hi = pad_end[-1] + MOE_BM * (1 + jnp.arange(n_blocks - (n * TOP_K) // MOE_BM, dtype=I32))
    tail_ok = tail_hi <= n_rows
    pad_lo = jnp.concatenate([pad_start + cnt, jnp.where(tail_ok, tail_hi - MOE_BM, MOE_BM)]).astype(I32)
    pad_hi = jnp.concatenate([pad_end, jnp.where(tail_ok, tail_hi, MOE_BM)]).astype(I32)
    blk_start = jnp.arange(n_blocks, dtype=I32) * MOE_BM
    blk_e = jnp.minimum(jnp.sum((pad_end[None, :] <= blk_start[:, None]).astype(I32), axis=1), N_EXPERTS - 1)
    n_used = (pad_end[-1:] // MOE_BM).astype(I32)
    xs = _dispatch(x1_tiles, dest_flat, pad_lo, pad_hi, n_rows)
    ys = _expert_ffn(xs, blk_e, n_used, w_gu, b_gu, w_dn, b_dn)
    return _combine(ys, dest_flat, gates, x1, ln_g, ln_b)


def _pad_cols(w, width):
    return jnp.pad(w, ((0, 0), (0, width - w.shape[1])))


def kernel(x, w_in_0, conv_w_0, w_out_0, ln_mix_g_0, ln_mix_b_0, router_w_0, router_b_0, w_gu_0, b_gu_0, w_dn_0, b_dn_0, ln_ffn_g_0, ln_ffn_b_0, w_in_1, lam_q1_1, lam_k1_1, lam_q2_1, lam_k2_1, subln_g_1, w_out_1, ln_mix_g_1, ln_mix_b_1, router_w_1, router_b_1, w_gu_1, b_gu_1, w_dn_1, b_dn_1, ln_ffn_g_1, ln_ffn_b_1):
    batch, seq, d = x.shape
    x0 = x.reshape(batch * seq, d)

    n_f32 = 3 * CONV_W
    n_attn = 3 * N_HEADS_SPARSE * D_HEAD + N_IDX_HEADS * D_IDX + D_IDX
    w_a = _pad_cols(jnp.concatenate([w_in_0[:, :n_f32], w_in_0[:, n_f32 + n_attn:]], axis=1),
                    n_f32 + LANES)
    w_b = _pad_cols(w_in_0[:, n_f32:n_f32 + n_attn], 3 * N_HEADS_SPARSE * D_HEAD + N_IDX_HEADS * D_IDX + LANES)
    ha, hb = _project(x0, [w_a.astype(BF16), w_b.astype(BF16)], [F32, BF16])
    ya = _short_conv(ha, conv_w_0, batch, seq)
    yb = _dsa_attention(ha, hb, batch, seq)
    x1, x1t, eidx, gates, rank, counts = _mix_out(ya, yb, w_out_0, x0, ln_mix_g_0, ln_mix_b_0, router_w_0, router_b_0)
    x2 = _moe_block(x1, x1t, eidx, gates, rank, counts, w_gu_0, b_gu_0, w_dn_0, b_dn_0, ln_ffn_g_0, ln_ffn_b_0)

    (hc,) = _project(x2, [w_in_1.astype(BF16)], [BF16])
    lam_vecs = jnp.stack([lam_q1_1, lam_k1_1, lam_q2_1, lam_k2_1]).astype(F32)
    yc = _diff_attention(hc, lam_vecs, subln_g_1.reshape(1, -1).astype(F32), batch, seq, 1)
    yd = _dilated_attention(hc, batch, seq)
    x3, x3t, eidx, gates, rank, counts = _mix_out(yc, yd, w_out_1, x2, ln_mix_g_1, ln_mix_b_1, router_w_1, router_b_1)
    x4 = _moe_block(x3, x3t, eidx, gates, rank, counts, w_gu_1, b_gu_1, w_dn_1, b_dn_1, ln_ffn_g_1, ln_ffn_b_1)
    return x4.reshape(batch, seq, d)
```

```python
import functools
import math

import jax
import jax.numpy as jnp
from jax import lax
from jax.experimental import pallas as pl
from jax.experimental.pallas import tpu as pltpu

F32 = jnp.float32
BF16 = jnp.bfloat16
I32 = jnp.int32

CONV_W = 512
CONV_TAPS = 3
N_HEADS_SPARSE = 8
D_HEAD = 64
N_IDX_HEADS = 8
D_IDX = 32
TOPK_LIMIT = 256
N_HEADS_DIFF = 4
N_HEADS_DIL = 8
DIL_GROUPS = ((128, 1), (512, 4), (2048, 16))
N_EXPERTS = 32
TOP_K = 4
D_FF = 1024
SWIGLU_LIMIT = 7.0
SWIGLU_ALPHA = 1.702
DEPTH = 2
DEEPNORM_ALPHA = (2 * DEPTH) ** 0.25
LN_EPS = 1e-5
RMS_EPS = 1e-5

LANES = 128
VMEM_LIMIT = 56 * 1024 * 1024
TQ = 256
CHAIN_GROUP = 4
KC = 512
ROW_TILE = 512
MOE_BM = 256
GATHER_T = 256
DISPATCH_T = 512
CONV_TILE = 512
ISSUE_UNROLL = 8
COUNT_ROWS = 64
LOG2_E = math.log2(math.e)
INT_MIN = -2 ** 31


def _alibi_slopes(n):
    return [2.0 ** (-8.0 * (h + 1) / n) for h in range(n)]


def _cparams(*sem):
    return pltpu.CompilerParams(dimension_semantics=sem, vmem_limit_bytes=VMEM_LIMIT)


def _layer_norm(z, g, b):
    mu = jnp.mean(z, axis=-1, keepdims=True)
    zc = z - mu
    var = jnp.mean(zc * zc, axis=-1, keepdims=True)
    return zc * lax.rsqrt(var + LN_EPS) * g + b


def _proj_kernel(*refs, n_out):
    x_ref = refs[0]
    w_refs = refs[1:1 + n_out]
    o_refs = refs[1 + n_out:]
    xb = x_ref[...].astype(BF16)
    for w_ref, o_ref in zip(w_refs, o_refs):
        o_ref[...] = jnp.dot(xb, w_ref[...], preferred_element_type=F32).astype(o_ref.dtype)


def _project(x2d, weights, out_dtypes):
    n, d = x2d.shape
    n_out = len(weights)
    in_specs = [pl.BlockSpec((ROW_TILE, d), lambda i: (i, 0))]
    in_specs += [pl.BlockSpec(w.shape, lambda i: (0, 0)) for w in weights]
    out_specs = [pl.BlockSpec((ROW_TILE, w.shape[1]), lambda i: (i, 0)) for w in weights]
    out_shape = [jax.ShapeDtypeStruct((n, w.shape[1]), dt) for w, dt in zip(weights, out_dtypes)]
    return pl.pallas_call(
        functools.partial(_proj_kernel, n_out=n_out),
        grid=(n // ROW_TILE,),
        in_specs=in_specs, out_specs=out_specs, out_shape=out_shape,
        compiler_params=_cparams("parallel"),
        name="in_proj",
    )(x2d, *weights)


def _conv_kernel(gb_ref, gc_ref, xa_ref, w_ref, o_ref, prev_ref):
    j = pl.program_id(1)
    t = gb_ref.shape[0]

    @pl.when(j == 0)
    def _():
        prev_ref[...] = jnp.zeros_like(prev_ref)

    z = gc_ref[...] * xa_ref[...]
    row = lax.broadcasted_iota(I32, z.shape, 0)
    prev2 = prev_ref[0:1, :]
    prev1 = prev_ref[1:2, :]
    z1 = jnp.where(row == 0, prev1, pltpu.roll(z, 1, 0))
    z2 = jnp.where(row == 0, prev2, jnp.where(row == 1, prev1, pltpu.roll(z, 2, 0)))
    w = w_ref[...]
    y = w[0:1, :] * z + w[1:2, :] * z1 + w[2:3, :] * z2
    o_ref[...] = (gb_ref[...] * y).astype(o_ref.dtype)
    prev_ref[0:1, :] = z[t - 2:t - 1, :]
    prev_ref[1:2, :] = z[t - 1:t, :]


def _short_conv(ha, conv_w, batch, seq):
    n = ha.shape[0]
    t = min(CONV_TILE, seq)
    nj = seq // t
    spec = lambda c: pl.BlockSpec((t, CONV_W), lambda b, j, c=c: (b * nj + j, c))
    return pl.pallas_call(
        _conv_kernel,
        grid=(batch, nj),
        in_specs=[spec(0), spec(1), spec(2), pl.BlockSpec((CONV_TAPS, CONV_W), lambda b, j: (0, 0))],
        out_specs=pl.BlockSpec((t, CONV_W), lambda b, j: (b * nj + j, 0)),
        out_shape=jax.ShapeDtypeStruct((n, CONV_W), BF16),
        scratch_shapes=[pltpu.VMEM((8, CONV_W), F32)],
        compiler_params=_cparams("arbitrary", "arbitrary"),
        name="short_conv",
    )(ha, ha, ha, conv_w)


def _lane_tiles(x):
    return [x[:, u * LANES:(u + 1) * LANES] for u in range(x.shape[1] // LANES)]


def _flash_chains(chains, k_ref, v_ref, dmat_ref, amat_ref, c_lo, c_hi, scratch):
    qm_ref, s_ref, mx_ref, lp_ref, l_ref, acc_ref = scratch
    mx_ref[...] = jnp.full(mx_ref.shape, -jnp.inf, F32)
    lp_ref[...] = jnp.zeros(lp_ref.shape, F32)
    acc_ref[...] = jnp.zeros(acc_ref.shape, F32)

    for g0 in range(0, len(chains), CHAIN_GROUP):
        group = list(enumerate(chains))[g0:g0 + CHAIN_GROUP]

        def pass_a(c, _, group=group, g0=g0):
            k0 = pl.multiple_of(c * KC, KC)
            for n, (k_tile, _, slope) in group:
                kc = k_ref[pl.ds(k0, KC), k_tile * LANES:(k_tile + 1) * LANES]
                s = lax.dot_general(qm_ref[n], kc, (((1,), (1,)), ((), ())), preferred_element_type=F32)
                s = s * LOG2_E - (slope * LOG2_E) * dmat_ref[:, pl.ds(k0, KC)]
                if amat_ref is not None:
                    s = s + amat_ref[:, pl.ds(k0, KC)]
                s_ref[n - g0, :, pl.ds(k0, KC)] = s
                part = mx_ref[n]
                for t in _lane_tiles(s):
                    part = jnp.maximum(part, t)
                mx_ref[n] = part
            return 0

        lax.fori_loop(c_lo, c_hi, pass_a, 0)

        for n, _ in group:
            row_max = jnp.max(mx_ref[n], axis=1, keepdims=True)
            mx_ref[n] = jnp.broadcast_to(row_max, (TQ, LANES))

        def pass_b(c, _, group=group, g0=g0):
            k0 = pl.multiple_of(c * KC, KC)
            for n, (_, v_tile, _) in group:
                row_max = mx_ref[n]
                p_tiles = [jnp.exp2(t - row_max) for t in _lane_tiles(s_ref[n - g0, :, pl.ds(k0, KC)])]
                part = lp_ref[n]
                for t in p_tiles:
                    part = part + t
                lp_ref[n] = part
                p = jnp.concatenate(p_tiles, axis=1).astype(BF16)
                vc = v_ref[pl.ds(k0, KC), v_tile * LANES:(v_tile + 1) * LANES]
                acc_ref[n] = acc_ref[n] + jnp.dot(p, vc, preferred_element_type=F32)
            return 0

        lax.fori_loop(c_lo, c_hi, pass_b, 0)

    for n in range(len(chains)):
        l_ref[n] = jnp.sum(lp_ref[n], axis=1, keepdims=True)


def _split_head_pairs(q_ref, qm_ref, n_tiles, scale):
    low = lax.broadcasted_iota(I32, (TQ, LANES), 1) < D_HEAD
    for j in range(n_tiles):
        qt = q_ref[:, j * LANES:(j + 1) * LANES] * scale
        qm_ref[2 * j] = jnp.where(low, qt, 0).astype(BF16)
        qm_ref[2 * j + 1] = jnp.where(low, 0, qt).astype(BF16)


def _merge_head_pairs(o_ref, l_ref, acc_ref, n_tiles):
    low = lax.broadcasted_iota(I32, (TQ, LANES), 1) < D_HEAD
    for j in range(n_tiles):
        out = jnp.where(low, acc_ref[2 * j] / l_ref[2 * j], acc_ref[2 * j + 1] / l_ref[2 * j + 1])
        o_ref[:, j * LANES:(j + 1) * LANES] = out.astype(o_ref.dtype)


def _flash_scratch(n_chains, seq):
    return [pltpu.VMEM((n_chains, TQ, LANES), BF16), pltpu.VMEM((CHAIN_GROUP, TQ, seq), F32),
            pltpu.VMEM((n_chains, TQ, LANES), F32), pltpu.VMEM((n_chains, TQ, LANES), F32),
            pltpu.VMEM((n_chains, TQ, 1), F32), pltpu.VMEM((n_chains, TQ, LANES), F32)]


def _dsa_kernel(q_ref, k_ref, v_ref, iq_ref, ik_ref, iw_ref, o_ref, sc_ref, sct_ref, dmat_ref, *scratch, k_sel):
    i = pl.program_id(1)
    seq_len = sc_ref.shape[1]
    t0 = i * TQ
    n_chunks = (t0 + TQ + KC - 1) // KC
    idx_scale = (D_IDX ** -0.5) * (N_IDX_HEADS ** -0.5)
    row = t0 + lax.broadcasted_iota(I32, (TQ, KC), 0)
    col_in_chunk = lax.broadcasted_iota(I32, (TQ, KC), 1)

    iq = iq_ref[...]
    iw = iw_ref[:, 0:N_IDX_HEADS]

    def score_body(c, _):
        k0 = pl.multiple_of(c * KC, KC)
        ik = ik_ref[pl.ds(k0, KC), 0:D_IDX]
        sc = jnp.zeros((TQ, KC), F32)
        for h in range(N_IDX_HEADS):
            rel = lax.dot_general(iq[:, h * D_IDX:(h + 1) * D_IDX], ik,
                                  (((1,), (1,)), ((), ())), preferred_element_type=F32)
            sc = sc + iw[:, h:h + 1] * jnp.maximum(rel, 0.0)
        causal = (k0 + col_in_chunk) <= row
        sc = jnp.where(causal, sc * idx_scale, -jnp.inf)
        sc_ref[:, pl.ds(k0, KC)] = sc
        sct_ref[pl.ds(k0, KC), :] = sc.T
        return 0

    lax.fori_loop(0, n_chunks, score_body, 0)

    def code_to_float(code):
        return lax.bitcast_convert_type(jnp.where(code < 0, code ^ 0x7FFFFFFF, code), F32)

    def count(pred):
        def body(c, acc):
            k0 = pl.multiple_of(c * KC, KC)
            hit = jnp.where(pred(sct_ref[pl.ds(k0, KC), :]), 1.0, 0.0)
            return acc + jnp.sum(hit.reshape(KC // COUNT_ROWS, COUNT_ROWS, TQ), axis=0)
        acc = lax.fori_loop(0, n_chunks, body, jnp.zeros((COUNT_ROWS, TQ), F32))
        return jnp.sum(acc, axis=0, keepdims=True)

    kf = float(k_sel)
    has_k = count(lambda t: t > -jnp.inf) >= kf
    code = jnp.where(count(lambda t: t >= 0.0) >= kf, 0, INT_MIN).astype(I32)

    def bit_body(b, code):
        cand = code + lax.shift_left(jnp.int32(1), 30 - b)
        cand_f = code_to_float(cand)
        return jnp.where(count(lambda t: t >= cand_f) >= kf, cand, code)

    code = lax.fori_loop(0, 31, bit_body, code)
    thr_q = jnp.where(has_k, code_to_float(code), -jnp.inf)
    need_q = jnp.where(has_k, kf - count(lambda t: t > thr_q), float(seq_len))
    thr = jnp.broadcast_to(thr_q, (LANES, TQ)).T
    need = jnp.broadcast_to(need_q, (LANES, TQ)).T

    tri = (lax.broadcasted_iota(I32, (KC, KC), 0) <= lax.broadcasted_iota(I32, (KC, KC), 1)).astype(BF16)
    thr_w = jnp.concatenate([thr] * (KC // LANES), axis=1)
    need_w = jnp.concatenate([need] * (KC // LANES), axis=1)

    def mask_body(c, ties_before):
        k0 = pl.multiple_of(c * KC, KC)
        sc = sc_ref[:, pl.ds(k0, KC)]
        eq = sc == thr_w
        eqf = jnp.where(eq, 1.0, 0.0)
        rank = ties_before + jnp.dot(eqf.astype(BF16), tri, preferred_element_type=F32)
        dist = row - (k0 + col_in_chunk)
        sel = ((sc > thr_w) | (eq & (rank <= need_w))) & (dist >= 0)
        dmat_ref[:, pl.ds(k0, KC)] = jnp.where(sel, dist.astype(F32), jnp.inf)
        return ties_before + jnp.sum(eqf, axis=1, keepdims=True)

    lax.fori_loop(0, n_chunks, mask_body, jnp.zeros((TQ, 1), F32))

    slopes = _alibi_slopes(N_HEADS_SPARSE)
    n_tiles = N_HEADS_SPARSE * D_HEAD // LANES
    _split_head_pairs(q_ref, scratch[0], n_tiles, D_HEAD ** -0.5)
    chains = [(h // 2, h // 2, slopes[h]) for h in range(N_HEADS_SPARSE)]
    _flash_chains(chains, k_ref, v_ref, dmat_ref, None, 0, n_chunks, scratch)
    _merge_head_pairs(o_ref, scratch[-2], scratch[-1], n_tiles)


def _dsa_attention(ha, hb, batch, seq):
    n = hb.shape[0]
    nq = seq // TQ
    k_sel = min(TOPK_LIMIT, seq // 4)
    width = N_HEADS_SPARSE * D_HEAD
    iq_w = N_IDX_HEADS * D_IDX
    return pl.pallas_call(
        functools.partial(_dsa_kernel, k_sel=k_sel),
        grid=(batch, nq),
        in_specs=[
            pl.BlockSpec((TQ, width), lambda b, i: (b * nq + i, 0)),
            pl.BlockSpec((seq, width), lambda b, i: (b, 1)),
            pl.BlockSpec((seq, width), lambda b, i: (b, 2)),
            pl.BlockSpec((TQ, iq_w), lambda b, i: (b * nq + i, 3 * width // iq_w)),
            pl.BlockSpec((seq, LANES), lambda b, i: (b, (3 * width + iq_w) // LANES)),
            pl.BlockSpec((TQ, LANES), lambda b, i: (b * nq + i, 3 * CONV_W // LANES)),
        ],
        out_specs=pl.BlockSpec((TQ, width), lambda b, i: (b * nq + i, 0)),
        out_shape=jax.ShapeDtypeStruct((n, width), BF16),
        scratch_shapes=[pltpu.VMEM((TQ, seq), F32), pltpu.VMEM((seq, TQ), F32), pltpu.VMEM((TQ, seq), F32)]
        + _flash_scratch(N_HEADS_SPARSE, seq),
        compiler_params=_cparams("arbitrary", "arbitrary"),
        name="dsa_attention",
    )(hb, hb, hb, hb, hb, ha)


def _diff_kernel(q_ref, k_ref, v_ref, lam_ref, g_ref, o_ref, dmat_ref, *scratch, lam_init):
    i = pl.program_id(1)
    t0 = i * TQ
    n_chunks = (t0 + TQ + KC - 1) // KC
    row = t0 + lax.broadcasted_iota(I32, (TQ, KC), 0)
    col_in_chunk = lax.broadcasted_iota(I32, (TQ, KC), 1)

    def mask_body(c, _):
        k0 = pl.multiple_of(c * KC, KC)
        dist = row - (k0 + col_in_chunk)
        dmat_ref[:, pl.ds(k0, KC)] = jnp.where(dist >= 0, dist.astype(F32), jnp.inf)
        return 0

    lax.fori_loop(0, n_chunks, mask_body, 0)

    lv = lam_ref[...]
    lam = (jnp.exp(jnp.sum(lv[0:1, :] * lv[1:2, :], axis=1, keepdims=True))
           - jnp.exp(jnp.sum(lv[2:3, :] * lv[3:4, :], axis=1, keepdims=True)) + lam_init)

    slopes = _alibi_slopes(N_HEADS_DIFF)
    _split_head_pairs(q_ref, scratch[0], N_HEADS_DIFF, D_HEAD ** -0.5)
    chains = [(n // 2, n // 2, slopes[n // 2]) for n in range(2 * N_HEADS_DIFF)]
    _flash_chains(chains, k_ref, v_ref, dmat_ref, None, 0, n_chunks, scratch)
    l_ref, acc_ref = scratch[-2], scratch[-1]
    g = g_ref[...]
    for h in range(N_HEADS_DIFF):
        of = acc_ref[2 * h] / l_ref[2 * h] - lam * (acc_ref[2 * h + 1] / l_ref[2 * h + 1])
        of = of * lax.rsqrt(jnp.mean(of * of, axis=1, keepdims=True) + RMS_EPS) * g
        o_ref[:, h * LANES:(h + 1) * LANES] = (of * (1.0 - lam_init)).astype(o_ref.dtype)


def _diff_attention(hb, lam_vecs, subln_g, batch, seq, layer):
    n = hb.shape[0]
    nq = seq // TQ
    width = N_HEADS_DIFF * 2 * D_HEAD
    lam_init = 0.8 - 0.6 * math.exp(-0.3 * layer)
    return pl.pallas_call(
        functools.partial(_diff_kernel, lam_init=lam_init),
        grid=(batch, nq),
        in_specs=[
            pl.BlockSpec((TQ, width), lambda b, i: (b * nq + i, 0)),
            pl.BlockSpec((seq, width), lambda b, i: (b, 1)),
            pl.BlockSpec((seq, width), lambda b, i: (b, 2)),
            pl.BlockSpec(lam_vecs.shape, lambda b, i: (0, 0)),
            pl.BlockSpec(subln_g.shape, lambda b, i: (0, 0)),
        ],
        out_specs=pl.BlockSpec((TQ, width), lambda b, i: (b * nq + i, 0)),
        out_shape=jax.ShapeDtypeStruct((n, width), BF16),
        scratch_shapes=[pltpu.VMEM((TQ, seq), F32)] + _flash_scratch(2 * N_HEADS_DIFF, seq),
        compiler_params=_cparams("arbitrary", "arbitrary"),
        name="diff_attention",
    )(hb, hb, hb, lam_vecs, subln_g)


def _dilated_kernel(q_ref, k_ref, v_ref, o_ref, dmat_ref, amat_ref, *scratch):
    i = pl.program_id(1)
    t0 = i * TQ
    w_max = max(w for w, _ in DIL_GROUPS)
    c_lo = jnp.maximum(t0 - w_max, 0) // KC
    c_hi = (t0 + TQ + KC - 1) // KC
    row = t0 + lax.broadcasted_iota(I32, (TQ, KC), 0)
    col_in_chunk = lax.broadcasted_iota(I32, (TQ, KC), 1)

    def mask_body(c, _):
        k0 = pl.multiple_of(c * KC, KC)
        dist = row - (k0 + col_in_chunk)
        mult = jnp.zeros((TQ, KC), F32)
        for w, d in DIL_GROUPS:
            member = (dist >= 0) & (dist <= w) & ((dist & (d - 1)) == 0)
            mult = mult + jnp.where(member, 1.0, 0.0)
        on = mult > 0.0
        dmat_ref[:, pl.ds(k0, KC)] = jnp.where(on, dist.astype(F32), jnp.inf)
        amat_ref[:, pl.ds(k0, KC)] = jnp.log2(jnp.where(on, mult, 1.0))
        return 0

    lax.fori_loop(c_lo, c_hi, mask_body, 0)

    slopes = _alibi_slopes(N_HEADS_DIL)
    n_tiles = N_HEADS_DIL * D_HEAD // LANES
    _split_head_pairs(q_ref, scratch[0], n_tiles, D_HEAD ** -0.5)
    chains = [(h // 2, h // 2, slopes[h]) for h in range(N_HEADS_DIL)]
    _flash_chains(chains, k_ref, v_ref, dmat_ref, amat_ref, c_lo, c_hi, scratch)
    _merge_head_pairs(o_ref, scratch[-2], scratch[-1], n_tiles)


def _dilated_attention(hb, batch, seq):
    n = hb.shape[0]
    nq = seq // TQ
    width = N_HEADS_DIL * D_HEAD
    return pl.pallas_call(
        _dilated_kernel,
        grid=(batch, nq),
        in_specs=[
            pl.BlockSpec((TQ, width), lambda b, i: (b * nq + i, 3)),
            pl.BlockSpec((seq, width), lambda b, i: (b, 4)),
            pl.BlockSpec((seq, width), lambda b, i: (b, 5)),
        ],
        out_specs=pl.BlockSpec((TQ, width), lambda b, i: (b * nq + i, 0)),
        out_shape=jax.ShapeDtypeStruct((n, width), BF16),
        scratch_shapes=[pltpu.VMEM((TQ, seq), F32), pltpu.VMEM((TQ, seq), F32)] + _flash_scratch(N_HEADS_DIL, seq),
        compiler_params=_cparams("arbitrary", "arbitrary"),
        name="dilated_attention",
    )(hb, hb, hb)


def _mix_out_kernel(ya_ref, yb_ref, wa_ref, wb_ref, x_ref, g_ref, b_ref, rw_ref, rb_ref,
                    x1_ref, x1t_ref, eidx_ref, gate_ref, rank_ref, cnt_ref, carry_ref):
    step = pl.program_id(0)

    @pl.when(step == 0)
    def _():
        carry_ref[...] = jnp.zeros_like(carry_ref)

    m = (jnp.dot(ya_ref[...], wa_ref[...], preferred_element_type=F32)
         + jnp.dot(yb_ref[...], wb_ref[...], preferred_element_type=F32))
    x1 = _layer_norm(DEEPNORM_ALPHA * x_ref[...] + m, g_ref[...], b_ref[...])
    x1_ref[...] = x1
    _to_token_tiles(x1t_ref, x1)

    t = x1.shape[0]
    logits = jnp.dot(x1.astype(BF16), rw_ref[...], preferred_element_type=F32) + rb_ref[...]
    lane = lax.broadcasted_iota(I32, (t, LANES), 1)
    vals, idxs = [], []
    lg = logits
    for _ in range(TOP_K):
        mx = jnp.max(lg, axis=1, keepdims=True)
        ix = jnp.min(jnp.where(lg == mx, lane, LANES), axis=1, keepdims=True)
        vals.append(mx)
        idxs.append(ix)
        lg = jnp.where(lane == ix, -jnp.inf, lg)
    exps = [jnp.exp(v - vals[0]) for v in vals]
    denom = exps[0]
    for e in exps[1:]:
        denom = denom + e

    onehot = jnp.zeros((t, LANES), F32)
    for ix in idxs:
        onehot = onehot + jnp.where(lane == ix, 1.0, 0.0)
    strict = (lax.broadcasted_iota(I32, (t, t), 1) < lax.broadcasted_iota(I32, (t, t), 0)).astype(BF16)
    before = jnp.dot(strict, onehot.astype(BF16), preferred_element_type=F32) + carry_ref[...]

    eidx = jnp.zeros((t, LANES), I32)
    gate = jnp.zeros((t, LANES), F32)
    rank = jnp.zeros((t, LANES), I32)
    for k in range(TOP_K):
        rk = jnp.sum(jnp.where(lane == idxs[k], before, 0.0), axis=1, keepdims=True)
        eidx = jnp.where(lane == k, idxs[k], eidx)
        gate = jnp.where(lane == k, exps[k] / denom, gate)
        rank = jnp.where(lane == k, rk.astype(I32), rank)
    eidx_ref[...] = eidx
    gate_ref[...] = gate
    rank_ref[...] = rank
    carry_ref[...] = carry_ref[...] + jnp.sum(onehot, axis=0, keepdims=True)
    cnt_ref[...] = carry_ref[...]


def _mix_out(ya, yb, w_out, x2d, ln_g, ln_b, router_w, router_b):
    n, d = x2d.shape
    half = ya.shape[1]
    wa = w_out[:half].astype(BF16)
    wb = w_out[half:].astype(BF16)
    rw = jnp.zeros((d, LANES), BF16).at[:, :N_EXPERTS].set(router_w.astype(BF16))
    rb = jnp.full((1, LANES), -jnp.inf, F32).at[0, :N_EXPERTS].set(router_b)
    t = ROW_TILE
    full = lambda a: pl.BlockSpec(a.shape, lambda i: (0, 0))
    rows = lambda w: pl.BlockSpec((t, w), lambda i: (i, 0))
    g2, b2 = ln_g.reshape(1, d), ln_b.reshape(1, d)
    return pl.pallas_call(
        _mix_out_kernel,
        grid=(n // t,),
        in_specs=[rows(half), rows(half), full(wa), full(wb), rows(d), full(g2), full(b2), full(rw), full(rb)],
        out_specs=[rows(d), pl.BlockSpec((t * ROW_SUBLANES, LANES), lambda i: (i, 0)),
                   rows(LANES), rows(LANES), rows(LANES), pl.BlockSpec((1, LANES), lambda i: (0, 0))],
        out_shape=[jax.ShapeDtypeStruct((n, d), F32), jax.ShapeDtypeStruct((n * ROW_SUBLANES, LANES), F32),
                   jax.ShapeDtypeStruct((n, LANES), I32),
                   jax.ShapeDtypeStruct((n, LANES), F32), jax.ShapeDtypeStruct((n, LANES), I32),
                   jax.ShapeDtypeStruct((1, LANES), F32)],
        scratch_shapes=[pltpu.VMEM((1, LANES), F32)],
        compiler_params=_cparams("arbitrary"),
        name="mix_out_router",
    )(ya, yb, wa, wb, x2d, g2, b2, rw, rb)


ROW_SUBLANES = 8


def _to_token_tiles(o_ref, x):
    for j in range(ROW_SUBLANES):
        o_ref[pl.ds(j, x.shape[0], stride=ROW_SUBLANES), :] = x[:, j * LANES:(j + 1) * LANES]


def _from_token_tiles(buf_ref, n_rows):
    return [buf_ref[pl.ds(j, n_rows, stride=ROW_SUBLANES), :] for j in range(ROW_SUBLANES)]


def _row_copy(src_hbm, dst_ref, src_row, dst_row, sem):
    src = src_hbm.at[pl.ds(pl.multiple_of(src_row * ROW_SUBLANES, ROW_SUBLANES), ROW_SUBLANES)]
    dst = dst_ref.at[pl.ds(pl.multiple_of(dst_row * ROW_SUBLANES, ROW_SUBLANES), ROW_SUBLANES)]
    return pltpu.make_async_copy(src, dst, sem)


def _rows_wait(src_hbm, dst_ref, sem):
    pltpu.make_async_copy(src_hbm.at[pl.ds(0, dst_ref.shape[0])], dst_ref, sem).wait()


def _dispatch_kernel(dest_ref, pad_lo_ref, pad_hi_ref, x_ref, o_hbm, zero_ref, sem, zsem):
    i = pl.program_id(0)
    t = x_ref.shape[0] // ROW_SUBLANES
    zero_rows = zero_ref.shape[0] // ROW_SUBLANES

    def zero_fill(s):
        last = pl.multiple_of((pad_hi_ref[s] - zero_rows) * ROW_SUBLANES, ROW_SUBLANES)
        return pltpu.make_async_copy(zero_ref, o_hbm.at[pl.ds(last, zero_ref.shape[0])], zsem)

    @pl.when(i == 0)
    def _():
        zero_ref[...] = jnp.zeros_like(zero_ref)
        for wait in (False, True):
            def body(s, _, wait=wait):
                @pl.when(pad_lo_ref[s] < pad_hi_ref[s])
                def _():
                    if wait:
                        zero_fill(s).wait()
                    else:
                        zero_fill(s).start()
                return 0
            lax.fori_loop(0, pad_lo_ref.shape[0], body, 0)

    base = i * (t * TOP_K)

    def body(g, _):
        for u in range(ISSUE_UNROLL // TOP_K):
            r = g * (ISSUE_UNROLL // TOP_K) + u
            src = x_ref.at[pl.ds(pl.multiple_of(r * ROW_SUBLANES, ROW_SUBLANES), ROW_SUBLANES)]
            for k in range(TOP_K):
                row = dest_ref[base + r * TOP_K + k]
                dst = o_hbm.at[pl.ds(pl.multiple_of(row * ROW_SUBLANES, ROW_SUBLANES), ROW_SUBLANES)]
                pltpu.make_async_copy(src, dst, sem).start(priority=k % 2)
        return 0

    lax.fori_loop(0, t * TOP_K // ISSUE_UNROLL, body, 0)
    for _ in range(TOP_K):
        pltpu.make_async_copy(x_ref, o_hbm.at[pl.ds(0, x_ref.shape[0])], sem).wait()


def _dispatch(x_tiles, dest_flat, pad_lo, pad_hi, n_rows):
    n = x_tiles.shape[0] // ROW_SUBLANES
    t = min(DISPATCH_T, n)
    grid_spec = pltpu.PrefetchScalarGridSpec(
        num_scalar_prefetch=3,
        grid=(n // t,),
        in_specs=[pl.BlockSpec((t * ROW_SUBLANES, LANES), lambda i, *_: (i, 0))],
        out_specs=pl.BlockSpec(memory_space=pl.ANY),
        scratch_shapes=[pltpu.VMEM((MOE_BM * ROW_SUBLANES, LANES), F32),
                        pltpu.SemaphoreType.DMA(()), pltpu.SemaphoreType.DMA(())],
    )
    return pl.pallas_call(
        _dispatch_kernel,
        grid_spec=grid_spec,
        out_shape=jax.ShapeDtypeStruct((n_rows * ROW_SUBLANES, LANES), F32),
        compiler_params=_cparams("arbitrary"),
        name="moe_dispatch",
    )(dest_flat, pad_lo, pad_hi, x_tiles)


def _expert_kernel(blk_e_ref, n_used_ref, run_start_ref, run_parity_ref, next_e_ref,
                   x_ref, wgu_hbm, bgu_ref, wdn_hbm, bdn_ref,
                   o_ref, wgu_raw, wdn_raw, wgu_bf, wdn_bf, wsem):
    i = pl.program_id(0)
    n_used = n_used_ref[0]
    e = blk_e_ref[i]
    parity = run_parity_ref[i]

    def weight_copies(expert, p):
        return (pltpu.make_async_copy(wgu_hbm.at[expert], wgu_raw.at[p], wsem.at[p, 0]),
                pltpu.make_async_copy(wdn_hbm.at[expert], wdn_raw.at[p], wsem.at[p, 1]))

    @pl.when((i == 0) & (run_start_ref[0] == 1))
    def _():
        for c in weight_copies(e, parity):
            c.start()

    @pl.when(run_start_ref[i] == 1)
    def _():
        for c in weight_copies(e, parity):
            c.wait()
        wgu_bf[...] = wgu_raw[parity].astype(BF16)
        wdn_bf[...] = wdn_raw[parity].astype(BF16)
        e_next = next_e_ref[i]

        @pl.when(e_next >= 0)
        def _():
            for c in weight_copies(e_next, 1 - parity):
                c.start()

    @pl.when(i < n_used)
    def _():
        x = jnp.concatenate(_from_token_tiles(x_ref, MOE_BM), axis=1).astype(BF16)
        h = jnp.dot(x, wgu_bf[...], preferred_element_type=F32) + bgu_ref[...]
        gate = jnp.minimum(h[:, :D_FF], SWIGLU_LIMIT)
        up = jnp.clip(h[:, D_FF:], -SWIGLU_LIMIT, SWIGLU_LIMIT)
        glu = gate * (1.0 / (1.0 + jnp.exp(-SWIGLU_ALPHA * gate)))
        act = ((up + 1.0) * glu).astype(BF16)
        _to_token_tiles(o_ref, jnp.dot(act, wdn_bf[...], preferred_element_type=F32) + bdn_ref[...])

    @pl.when(i >= n_used)
    def _():
        o_ref[...] = jnp.zeros_like(o_ref)


def _expert_ffn(xs, blk_e, n_used, w_gu, b_gu, w_dn, b_dn):
    n_rows = xs.shape[0] // ROW_SUBLANES
    d = w_gu.shape[1]
    n_blocks = n_rows // MOE_BM
    ne = w_gu.shape[0]
    blk = jnp.arange(n_blocks, dtype=I32)
    prev_e = jnp.concatenate([blk_e[:1], blk_e[:-1]])
    run_start = (blk < n_used[0]) & ((blk == 0) | (blk_e != prev_e))
    run_parity = ((jnp.cumsum(run_start.astype(I32)) - 1) % 2).astype(I32)
    first_start_from = lax.cummin(jnp.where(run_start, blk, n_blocks)[::-1])[::-1]
    next_start = jnp.concatenate([first_start_from[1:], jnp.full((1,), n_blocks, I32)])
    next_e = jnp.where(next_start < n_blocks, blk_e[jnp.minimum(next_start, n_blocks - 1)], -1).astype(I32)
    idx = lambda f: (lambda i, *prefetch: f(i, prefetch[0]))
    grid_spec = pltpu.PrefetchScalarGridSpec(
        num_scalar_prefetch=5,
        grid=(n_blocks,),
        in_specs=[
            pl.BlockSpec((MOE_BM * ROW_SUBLANES, LANES),
                         lambda i, be, nu, *_: (jnp.maximum(jnp.minimum(i, nu[0] - 1), 0), 0)),
            pl.BlockSpec(memory_space=pl.ANY),
            pl.BlockSpec((None, 1, 2 * D_FF), idx(lambda i, be: (be[i], 0, 0))),
            pl.BlockSpec(memory_space=pl.ANY),
            pl.BlockSpec((None, 1, d), idx(lambda i, be: (be[i], 0, 0))),
        ],
        out_specs=pl.BlockSpec((MOE_BM * ROW_SUBLANES, LANES), idx(lambda i, be: (i, 0))),
        scratch_shapes=[pltpu.VMEM((2, d, 2 * D_FF), F32), pltpu.VMEM((2, D_FF, d), F32),
                        pltpu.VMEM((d, 2 * D_FF), BF16), pltpu.VMEM((D_FF, d), BF16),
                        pltpu.SemaphoreType.DMA((2, 2))],
    )
    return pl.pallas_call(
        _expert_kernel,
        grid_spec=grid_spec,
        out_shape=jax.ShapeDtypeStruct((n_rows * ROW_SUBLANES, LANES), F32),
        compiler_params=_cparams("arbitrary"),
        name="moe_experts",
    )(blk_e, n_used, run_start.astype(I32), run_parity, next_e,
      xs, w_gu, b_gu.reshape(ne, 1, 2 * D_FF), w_dn, b_dn.reshape(ne, 1, d))


def _combine_kernel(dest_ref, y_hbm, gate_ref, x_ref, g_ref, b_ref, o_ref, buf_ref, sem):
    i = pl.program_id(0)
    t = x_ref.shape[0]
    slot = i % 2

    def gather(step, s):
        base = step * (t * TOP_K)

        def body(g, _):
            for u in range(ISSUE_UNROLL // TOP_K):
                r = g * (ISSUE_UNROLL // TOP_K) + u
                for k in range(TOP_K):
                    row = dest_ref[base + r * TOP_K + k]
                    _row_copy(y_hbm, buf_ref.at[s, k], row, r, sem.at[s]).start(priority=k % 2)
            return 0
        lax.fori_loop(0, t * TOP_K // ISSUE_UNROLL, body, 0)

    @pl.when(i == 0)
    def _():
        gather(0, 0)

    @pl.when(i + 1 < pl.num_programs(0))
    def _():
        gather(i + 1, 1 - slot)

    for k in range(TOP_K):
        _rows_wait(y_hbm, buf_ref.at[slot, k], sem.at[slot])
    gate = gate_ref[...]
    f_tiles = None
    for k in range(TOP_K):
        y_tiles = [gate[:, k:k + 1] * yt for yt in _from_token_tiles(buf_ref.at[slot, k], t)]
        f_tiles = y_tiles if f_tiles is None else [a + b for a, b in zip(f_tiles, y_tiles)]
    f = jnp.concatenate(f_tiles, axis=1)
    o_ref[...] = _layer_norm(DEEPNORM_ALPHA * x_ref[...] + f, g_ref[...], b_ref[...])


def _combine(ys, dest_flat, gates, x2d, ln_g, ln_b):
    n, d = x2d.shape
    t = GATHER_T
    g2, b2 = ln_g.reshape(1, d), ln_b.reshape(1, d)
    full = lambda a: pl.BlockSpec(a.shape, lambda i, dest: (0, 0))
    grid_spec = pltpu.PrefetchScalarGridSpec(
        num_scalar_prefetch=1,
        grid=(n // t,),
        in_specs=[pl.BlockSpec(memory_space=pl.ANY),
                  pl.BlockSpec((t, LANES), lambda i, dest: (i, 0)),
                  pl.BlockSpec((t, d), lambda i, dest: (i, 0)),
                  full(g2), full(b2)],
        out_specs=pl.BlockSpec((t, d), lambda i, dest: (i, 0)),
        scratch_shapes=[pltpu.VMEM((2, TOP_K, t * ROW_SUBLANES, LANES), F32), pltpu.SemaphoreType.DMA((2,))],
    )
    return pl.pallas_call(
        _combine_kernel,
        grid_spec=grid_spec,
        out_shape=jax.ShapeDtypeStruct((n, d), F32),
        compiler_params=_cparams("arbitrary"),
        name="moe_combine",
    )(dest_flat, ys, gates, x2d, g2, b2)


def _moe_block(x1, x1_tiles, eidx, gates, rank, counts, w_gu, b_gu, w_dn, b_dn, ln_g, ln_b):
    n = x1.shape[0]
    e_sel = eidx[:, :TOP_K]
    cnt = counts[0, :N_EXPERTS].astype(I32)
    padded = ((cnt + MOE_BM - 1) // MOE_BM) * MOE_BM
    pad_end = jnp.cumsum(padded)
    pad_start = pad_end - padded
    expert_ids = jnp.arange(N_EXPERTS, dtype=I32)
    dest = jnp.sum(jnp.where(e_sel[..., None] == expert_ids, pad_start, 0), axis=-1) + rank[:, :TOP_K]
    n_rows = ((n * TOP_K + N_EXPERTS * (MOE_BM - 1) + MOE_BM - 1) // MOE_BM) * MOE_BM
    n_blocks = n_rows // MOE_BM
    dest_flat = dest.reshape(-1)
    tail_hi = pad_end[-1] + MOE_BM * (1 + jnp.arange(n_blocks - (n * TOP_K) // MOE_BM, dtype=I32))
    tail_ok = tail_hi <= n_rows
    pad_lo = jnp.concatenate([pad_start + cnt, jnp.where(tail_ok, tail_hi - MOE_BM, MOE_BM)]).astype(I32)
    pad_hi = jnp.concatenate([pad_end, jnp.where(tail_ok, tail_hi, MOE_BM)]).astype(I32)
    blk_start = jnp.arange(n_blocks, dtype=I32) * MOE_BM
    blk_e = jnp.minimum(jnp.sum((pad_end[None, :] <= blk_start[:, None]).astype(I32), axis=1), N_EXPERTS - 1)
    n_used = (pad_end[-1:] // MOE_BM).astype(I32)
    xs = _dispatch(x1_tiles, dest_flat, pad_lo, pad_hi, n_rows)
    ys = _expert_ffn(xs, blk_e, n_used, w_gu, b_gu, w_dn, b_dn)
    return _combine(ys, dest_flat, gates, x1, ln_g, ln_b)


def _pad_cols(w, width):
    return jnp.pad(w, ((0, 0), (0, width - w.shape[1])))


def kernel(x, w_in_0, conv_w_0, w_out_0, ln_mix_g_0, ln_mix_b_0, router_w_0, router_b_0, w_gu_0, b_gu_0, w_dn_0, b_dn_0, ln_ffn_g_0, ln_ffn_b_0, w_in_1, lam_q1_1, lam_k1_1, lam_q2_1, lam_k2_1, subln_g_1, w_out_1, ln_mix_g_1, ln_mix_b_1, router_w_1, router_b_1, w_gu_1, b_gu_1, w_dn_1, b_dn_1, ln_ffn_g_1, ln_ffn_b_1):
    batch, seq, d = x.shape
    x0 = x.reshape(batch * seq, d)

    n_f32 = 3 * CONV_W
    n_attn = 3 * N_HEADS_SPARSE * D_HEAD + N_IDX_HEADS * D_IDX + D_IDX
    w_a = _pad_cols(jnp.concatenate([w_in_0[:, :n_f32], w_in_0[:, n_f32 + n_attn:]], axis=1),
                    n_f32 + LANES)
    w_b = _pad_cols(w_in_0[:, n_f32:n_f32 + n_attn], 3 * N_HEADS_SPARSE * D_HEAD + N_IDX_HEADS * D_IDX + LANES)
    ha, hb = _project(x0, [w_a.astype(BF16), w_b.astype(BF16)], [F32, BF16])
    ya = _short_conv(ha, conv_w_0, batch, seq)
    yb = _dsa_attention(ha, hb, batch, seq)
    x1, x1t, eidx, gates, rank, counts = _mix_out(ya, yb, w_out_0, x0, ln_mix_g_0, ln_mix_b_0, router_w_0, router_b_0)
    x2 = _moe_block(x1, x1t, eidx, gates, rank, counts, w_gu_0, b_gu_0, w_dn_0, b_dn_0, ln_ffn_g_0, ln_ffn_b_0)

    (hc,) = _project(x2, [w_in_1.astype(BF16)], [BF16])
    lam_vecs = jnp.stack([lam_q1_1, lam_k1_1, lam_q2_1, lam_k2_1]).astype(F32)
    yc = _diff_attention(hc, lam_vecs, subln_g_1.reshape(1, -1).astype(F32), batch, seq, 1)
    yd = _dilated_attention(hc, batch, seq)
    x3, x3t, eidx, gates, rank, counts = _mix_out(yc, yd, w_out_1, x2, ln_mix_g_1, ln_mix_b_1, router_w_1, router_b_1)
    x4 = _moe_block(x3, x3t, eidx, gates, rank, counts, w_gu_1, b_gu_1, w_dn_1, b_dn_1, ln_ffn_g_1, ln_ffn_b_1)
    return x4.reshape(batch, seq, d)
```

```python
import functools
import math

import jax
import jax.numpy as jnp
from jax import lax
from jax.experimental import pallas as pl
from jax.experimental.pallas import tpu as pltpu

F32 = jnp.float32
BF16 = jnp.bfloat16
I32 = jnp.int32

CONV_W = 512
CONV_TAPS = 3
N_HEADS_SPARSE = 8
D_HEAD = 64
N_IDX_HEADS = 8
D_IDX = 32
TOPK_LIMIT = 256
N_HEADS_DIFF = 4
N_HEADS_DIL = 8
DIL_GROUPS = ((128, 1), (512, 4), (2048, 16))
N_EXPERTS = 32
TOP_K = 4
D_FF = 1024
SWIGLU_LIMIT = 7.0
SWIGLU_ALPHA = 1.702
DEPTH = 2
DEEPNORM_ALPHA = (2 * DEPTH) ** 0.25
LN_EPS = 1e-5
RMS_EPS = 1e-5

LANES = 128
VMEM_LIMIT = 56 * 1024 * 1024
TQ = 256
CHAIN_GROUP = 4
KC = 512
ROW_TILE = 512
MOE_BM = 256
GATHER_T = 256
DISPATCH_T = 512
CONV_TILE = 512
ISSUE_UNROLL = 8
COUNT_ROWS = 64
LOG2_E = math.log2(math.e)
INT_MIN = -2 ** 31


def _alibi_slopes(n):
    return [2.0 ** (-8.0 * (h + 1) / n) for h in range(n)]


def _cparams(*sem):
    return pltpu.CompilerParams(dimension_semantics=sem, vmem_limit_bytes=VMEM_LIMIT)


def _layer_norm(z, g, b):
    mu = jnp.mean(z, axis=-1, keepdims=True)
    zc = z - mu
    var = jnp.mean(zc * zc, axis=-1, keepdims=True)
    return zc * lax.rsqrt(var + LN_EPS) * g + b


def _proj_kernel(*refs, n_out):
    x_ref = refs[0]
    w_refs = refs[1:1 + n_out]
    o_refs = refs[1 + n_out:]
    xb = x_ref[...].astype(BF16)
    for w_ref, o_ref in zip(w_refs, o_refs):
        o_ref[...] = jnp.dot(xb, w_ref[...], preferred_element_type=F32).astype(o_ref.dtype)


def _project(x2d, weights, out_dtypes):
    n, d = x2d.shape
    n_out = len(weights)
    in_specs = [pl.BlockSpec((ROW_TILE, d), lambda i: (i, 0))]
    in_specs += [pl.BlockSpec(w.shape, lambda i: (0, 0)) for w in weights]
    out_specs = [pl.BlockSpec((ROW_TILE, w.shape[1]), lambda i: (i, 0)) for w in weights]
    out_shape = [jax.ShapeDtypeStruct((n, w.shape[1]), dt) for w, dt in zip(weights, out_dtypes)]
    return pl.pallas_call(
        functools.partial(_proj_kernel, n_out=n_out),
        grid=(n // ROW_TILE,),
        in_specs=in_specs, out_specs=out_specs, out_shape=out_shape,
        compiler_params=_cparams("parallel"),
        name="in_proj",
    )(x2d, *weights)


def _conv_kernel(gb_ref, gc_ref, xa_ref, w_ref, o_ref, prev_ref):
    j = pl.program_id(1)
    t = gb_ref.shape[0]

    @pl.when(j == 0)
    def _():
        prev_ref[...] = jnp.zeros_like(prev_ref)

    z = gc_ref[...] * xa_ref[...]
    row = lax.broadcasted_iota(I32, z.shape, 0)
    prev2 = prev_ref[0:1, :]
    prev1 = prev_ref[1:2, :]
    z1 = jnp.where(row == 0, prev1, pltpu.roll(z, 1, 0))
    z2 = jnp.where(row == 0, prev2, jnp.where(row == 1, prev1, pltpu.roll(z, 2, 0)))
    w = w_ref[...]
    y = w[0:1, :] * z + w[1:2, :] * z1 + w[2:3, :] * z2
    o_ref[...] = (gb_ref[...] * y).astype(o_ref.dtype)
    prev_ref[0:1, :] = z[t - 2:t - 1, :]
    prev_ref[1:2, :] = z[t - 1:t, :]


def _short_conv(ha, conv_w, batch, seq):
    n = ha.shape[0]
    t = min(CONV_TILE, seq)
    nj = seq // t
    spec = lambda c: pl.BlockSpec((t, CONV_W), lambda b, j, c=c: (b * nj + j, c))
    return pl.pallas_call(
        _conv_kernel,
        grid=(batch, nj),
        in_specs=[spec(0), spec(1), spec(2), pl.BlockSpec((CONV_TAPS, CONV_W), lambda b, j: (0, 0))],
        out_specs=pl.BlockSpec((t, CONV_W), lambda b, j: (b * nj + j, 0)),
        out_shape=jax.ShapeDtypeStruct((n, CONV_W), BF16),
        scratch_shapes=[pltpu.VMEM((8, CONV_W), F32)],
        compiler_params=_cparams("arbitrary", "arbitrary"),
        name="short_conv",
    )(ha, ha, ha, conv_w)


def _lane_tiles(x):
    return [x[:, u * LANES:(u + 1) * LANES] for u in range(x.shape[1] // LANES)]


def _flash_chains(chains, k_ref, v_ref, dmat_ref, amat_ref, c_lo, c_hi, scratch):
    qm_ref, s_ref, mx_ref, lp_ref, l_ref, acc_ref = scratch
    mx_ref[...] = jnp.full(mx_ref.shape, -jnp.inf, F32)
    lp_ref[...] = jnp.zeros(lp_ref.shape, F32)
    acc_ref[...] = jnp.zeros(acc_ref.shape, F32)

    for g0 in range(0, len(chains), CHAIN_GROUP):
        group = list(enumerate(chains))[g0:g0 + CHAIN_GROUP]

        def pass_a(c, _, group=group, g0=g0):
            k0 = pl.multiple_of(c * KC, KC)
            for n, (k_tile, _, slope) in group:
                kc = k_ref[pl.ds(k0, KC), k_tile * LANES:(k_tile + 1) * LANES]
                s = lax.dot_general(qm_ref[n], kc, (((1,), (1,)), ((), ())), preferred_element_type=F32)
                s = s * LOG2_E - (slope * LOG2_E) * dmat_ref[:, pl.ds(k0, KC)]
                if amat_ref is not None:
                    s = s + amat_ref[:, pl.ds(k0, KC)]
                s_ref[n - g0, :, pl.ds(k0, KC)] = s
                part = mx_ref[n]
                for t in _lane_tiles(s):
                    part = jnp.maximum(part, t)
                mx_ref[n] = part
            return 0

        lax.fori_loop(c_lo, c_hi, pass_a, 0)

        for n, _ in group:
            row_max = jnp.max(mx_ref[n], axis=1, keepdims=True)
            mx_ref[n] = jnp.broadcast_to(row_max, (TQ, LANES))

        def pass_b(c, _, group=group, g0=g0):
            k0 = pl.multiple_of(c * KC, KC)
            for n, (_, v_tile, _) in group:
                row_max = mx_ref[n]
                p_tiles = [jnp.exp2(t - row_max) for t in _lane_tiles(s_ref[n - g0, :, pl.ds(k0, KC)])]
                part = lp_ref[n]
                for t in p_tiles:
                    part = part + t
                lp_ref[n] = part
                p = jnp.concatenate(p_tiles, axis=1).astype(BF16)
                vc = v_ref[pl.ds(k0, KC), v_tile * LANES:(v_tile + 1) * LANES]
                acc_ref[n] = acc_ref[n] + jnp.dot(p, vc, preferred_element_type=F32)
            return 0

        lax.fori_loop(c_lo, c_hi, pass_b, 0)

    for n in range(len(chains)):
        l_ref[n] = jnp.sum(lp_ref[n], axis=1, keepdims=True)


def _split_head_pairs(q_ref, qm_ref, n_tiles, scale):
    low = lax.broadcasted_iota(I32, (TQ, LANES), 1) < D_HEAD
    for j in range(n_tiles):
        qt = q_ref[:, j * LANES:(j + 1) * LANES] * scale
        qm_ref[2 * j] = jnp.where(low, qt, 0).astype(BF16)
        qm_ref[2 * j + 1] = jnp.where(low, 0, qt).astype(BF16)


def _merge_head_pairs(o_ref, l_ref, acc_ref, n_tiles):
    low = lax.broadcasted_iota(I32, (TQ, LANES), 1) < D_HEAD
    for j in range(n_tiles):
        out = jnp.where(low, acc_ref[2 * j] / l_ref[2 * j], acc_ref[2 * j + 1] / l_ref[2 * j + 1])
        o_ref[:, j * LANES:(j + 1) * LANES] = out.astype(o_ref.dtype)


def _flash_scratch(n_chains, seq):
    return [pltpu.VMEM((n_chains, TQ, LANES), BF16), pltpu.VMEM((CHAIN_GROUP, TQ, seq), F32),
            pltpu.VMEM((n_chains, TQ, LANES), F32), pltpu.VMEM((n_chains, TQ, LANES), F32),
            pltpu.VMEM((n_chains, TQ, 1), F32), pltpu.VMEM((n_chains, TQ, LANES), F32)]


def _dsa_kernel(q_ref, k_ref, v_ref, iq_ref, ik_ref, iw_ref, o_ref, sc_ref, sct_ref, dmat_ref, *scratch, k_sel):
    i = pl.program_id(1)
    seq_len = sc_ref.shape[1]
    t0 = i * TQ
    n_chunks = (t0 + TQ + KC - 1) // KC
    idx_scale = (D_IDX ** -0.5) * (N_IDX_HEADS ** -0.5)
    row = t0 + lax.broadcasted_iota(I32, (TQ, KC), 0)
    col_in_chunk = lax.broadcasted_iota(I32, (TQ, KC), 1)

    iq = iq_ref[...]
    iw = iw_ref[:, 0:N_IDX_HEADS]

    def score_body(c, _):
        k0 = pl.multiple_of(c * KC, KC)
        ik = ik_ref[pl.ds(k0, KC), 0:D_IDX]
        sc = jnp.zeros((TQ, KC), F32)
        for h in range(N_IDX_HEADS):
            rel = lax.dot_general(iq[:, h * D_IDX:(h + 1) * D_IDX], ik,
                                  (((1,), (1,)), ((), ())), preferred_element_type=F32)
            sc = sc + iw[:, h:h + 1] * jnp.maximum(rel, 0.0)
        causal = (k0 + col_in_chunk) <= row
        sc = jnp.where(causal, sc * idx_scale, -jnp.inf)
        sc_ref[:, pl.ds(k0, KC)] = sc
        sct_ref[pl.ds(k0, KC), :] = sc.T
        return 0

    lax.fori_loop(0, n_chunks, score_body, 0)

    def code_to_float(code):
        return lax.bitcast_convert_type(jnp.where(code < 0, code ^ 0x7FFFFFFF, code), F32)

    def count(pred):
        def body(c, acc):
            k0 = pl.multiple_of(c * KC, KC)
            hit = jnp.where(pred(sct_ref[pl.ds(k0, KC), :]), 1.0, 0.0)
            return acc + jnp.sum(hit.reshape(KC // COUNT_ROWS, COUNT_ROWS, TQ), axis=0)
        acc = lax.fori_loop(0, n_chunks, body, jnp.zeros((COUNT_ROWS, TQ), F32))
        return jnp.sum(acc, axis=0, keepdims=True)

    kf = float(k_sel)
    has_k = count(lambda t: t > -jnp.inf) >= kf
    n_nonneg = count(lambda t: t >= 0.0)
    code = jnp.where(n_nonneg >= kf, 0, INT_MIN).astype(I32)
    n_at_code = jnp.where(n_nonneg >= kf, n_nonneg, float(seq_len))

    def bit_body(b, carry):
        code, n_at_code = carry
        cand = code + lax.shift_left(jnp.int32(1), 30 - b)
        cand_f = code_to_float(cand)
        n_at_cand = count(lambda t: t >= cand_f)
        keep = n_at_cand >= kf
        return jnp.where(keep, cand, code), jnp.where(keep, n_at_cand, n_at_code)

    code, n_at_code = lax.fori_loop(0, 31, bit_body, (code, n_at_code))
    thr_q = jnp.where(has_k, code_to_float(code), -jnp.inf)
    thr = jnp.broadcast_to(thr_q, (LANES, TQ)).T
    thr_w = jnp.concatenate([thr] * (KC // LANES), axis=1)
    surplus_ties = jnp.max(jnp.where(has_k, n_at_code, 0.0)) > kf

    @pl.when(jnp.logical_not(surplus_ties))
    def _():
        def mask_body(c, _):
            k0 = pl.multiple_of(c * KC, KC)
            dist = row - (k0 + col_in_chunk)
            sel = (sc_ref[:, pl.ds(k0, KC)] >= thr_w) & (dist >= 0)
            dmat_ref[:, pl.ds(k0, KC)] = jnp.where(sel, dist.astype(F32), jnp.inf)
            return 0

        lax.fori_loop(0, n_chunks, mask_body, 0)

    @pl.when(surplus_ties)
    def _():
        need_q = jnp.where(has_k, kf - count(lambda t: t > thr_q), float(seq_len))
        need = jnp.broadcast_to(need_q, (LANES, TQ)).T
        need_w = jnp.concatenate([need] * (KC // LANES), axis=1)
        tri = (lax.broadcasted_iota(I32, (KC, KC), 0) <= lax.broadcasted_iota(I32, (KC, KC), 1)).astype(BF16)

        def mask_body(c, ties_before):
            k0 = pl.multiple_of(c * KC, KC)
            sc = sc_ref[:, pl.ds(k0, KC)]
            eq = sc == thr_w
            eqf = jnp.where(eq, 1.0, 0.0)
            rank = ties_before + jnp.dot(eqf.astype(BF16), tri, preferred_element_type=F32)
            dist = row - (k0 + col_in_chunk)
            sel = ((sc > thr_w) | (eq & (rank <= need_w))) & (dist >= 0)
            dmat_ref[:, pl.ds(k0, KC)] = jnp.where(sel, dist.astype(F32), jnp.inf)
            return ties_before + jnp.sum(eqf, axis=1, keepdims=True)

        lax.fori_loop(0, n_chunks, mask_body, jnp.zeros((TQ, 1), F32))

    slopes = _alibi_slopes(N_HEADS_SPARSE)
    n_tiles = N_HEADS_SPARSE * D_HEAD // LANES
    _split_head_pairs(q_ref, scratch[0], n_tiles, D_HEAD ** -0.5)
    chains = [(h // 2, h // 2, slopes[h]) for h in range(N_HEADS_SPARSE)]
    _flash_chains(chains, k_ref, v_ref, dmat_ref, None, 0, n_chunks, scratch)
    _merge_head_pairs(o_ref, scratch[-2], scratch[-1], n_tiles)


def _dsa_attention(ha, hb, batch, seq):
    n = hb.shape[0]
    nq = seq // TQ
    k_sel = min(TOPK_LIMIT, seq // 4)
    width = N_HEADS_SPARSE * D_HEAD
    iq_w = N_IDX_HEADS * D_IDX
    return pl.pallas_call(
        functools.partial(_dsa_kernel, k_sel=k_sel),
        grid=(batch, nq),
        in_specs=[
            pl.BlockSpec((TQ, width), lambda b, i: (b * nq + i, 0)),
            pl.BlockSpec((seq, width), lambda b, i: (b, 1)),
            pl.BlockSpec((seq, width), lambda b, i: (b, 2)),
            pl.BlockSpec((TQ, iq_w), lambda b, i: (b * nq + i, 3 * width // iq_w)),
            pl.BlockSpec((seq, LANES), lambda b, i: (b, (3 * width + iq_w) // LANES)),
            pl.BlockSpec((TQ, LANES), lambda b, i: (b * nq + i, 3 * CONV_W // LANES)),
        ],
        out_specs=pl.BlockSpec((TQ, width), lambda b, i: (b * nq + i, 0)),
        out_shape=jax.ShapeDtypeStruct((n, width), BF16),
        scratch_shapes=[pltpu.VMEM((TQ, seq), F32), pltpu.VMEM((seq, TQ), F32), pltpu.VMEM((TQ, seq), F32)]
        + _flash_scratch(N_HEADS_SPARSE, seq),
        compiler_params=_cparams("arbitrary", "arbitrary"),
        name="dsa_attention",
    )(hb, hb, hb, hb, hb, ha)


def _diff_kernel(q_ref, k_ref, v_ref, lam_ref, g_ref, o_ref, dmat_ref, *scratch, lam_init):
    i = pl.program_id(1)
    t0 = i * TQ
    n_chunks = (t0 + TQ + KC - 1) // KC
    row = t0 + lax.broadcasted_iota(I32, (TQ, KC), 0)
    col_in_chunk = lax.broadcasted_iota(I32, (TQ, KC), 1)

    def mask_body(c, _):
        k0 = pl.multiple_of(c * KC, KC)
        dist = row - (k0 + col_in_chunk)
        dmat_ref[:, pl.ds(k0, KC)] = jnp.where(dist >= 0, dist.astype(F32), jnp.inf)
        return 0

    lax.fori_loop(0, n_chunks, mask_body, 0)

    lv = lam_ref[...]
    lam = (jnp.exp(jnp.sum(lv[0:1, :] * lv[1:2, :], axis=1, keepdims=True))
           - jnp.exp(jnp.sum(lv[2:3, :] * lv[3:4, :], axis=1, keepdims=True)) + lam_init)

    slopes = _alibi_slopes(N_HEADS_DIFF)
    _split_head_pairs(q_ref, scratch[0], N_HEADS_DIFF, D_HEAD ** -0.5)
    chains = [(n // 2, n // 2, slopes[n // 2]) for n in range(2 * N_HEADS_DIFF)]
    _flash_chains(chains, k_ref, v_ref, dmat_ref, None, 0, n_chunks, scratch)
    l_ref, acc_ref = scratch[-2], scratch[-1]
    g = g_ref[...]
    for h in range(N_HEADS_DIFF):
        of = acc_ref[2 * h] / l_ref[2 * h] - lam * (acc_ref[2 * h + 1] / l_ref[2 * h + 1])
        of = of * lax.rsqrt(jnp.mean(of * of, axis=1, keepdims=True) + RMS_EPS) * g
        o_ref[:, h * LANES:(h + 1) * LANES] = (of * (1.0 - lam_init)).astype(o_ref.dtype)


def _diff_attention(hb, lam_vecs, subln_g, batch, seq, layer):
    n = hb.shape[0]
    nq = seq // TQ
    width = N_HEADS_DIFF * 2 * D_HEAD
    lam_init = 0.8 - 0.6 * math.exp(-0.3 * layer)
    return pl.pallas_call(
        functools.partial(_diff_kernel, lam_init=lam_init),
        grid=(batch, nq),
        in_specs=[
            pl.BlockSpec((TQ, width), lambda b, i: (b * nq + i, 0)),
            pl.BlockSpec((seq, width), lambda b, i: (b, 1)),
            pl.BlockSpec((seq, width), lambda b, i: (b, 2)),
            pl.BlockSpec(lam_vecs.shape, lambda b, i: (0, 0)),
            pl.BlockSpec(subln_g.shape, lambda b, i: (0, 0)),
        ],
        out_specs=pl.BlockSpec((TQ, width), lambda b, i: (b * nq + i, 0)),
        out_shape=jax.ShapeDtypeStruct((n, width), BF16),
        scratch_shapes=[pltpu.VMEM((TQ, seq), F32)] + _flash_scratch(2 * N_HEADS_DIFF, seq),
        compiler_params=_cparams("arbitrary", "arbitrary"),
        name="diff_attention",
    )(hb, hb, hb, lam_vecs, subln_g)


def _dilated_kernel(q_ref, k_ref, v_ref, o_ref, dmat_ref, amat_ref, *scratch):
    i = pl.program_id(1)
    t0 = i * TQ
    w_max = max(w for w, _ in DIL_GROUPS)
    c_lo = jnp.maximum(t0 - w_max, 0) // KC
    c_hi = (t0 + TQ + KC - 1) // KC
    row = t0 + lax.broadcasted_iota(I32, (TQ, KC), 0)
    col_in_chunk = lax.broadcasted_iota(I32, (TQ, KC), 1)

    def mask_body(c, _):
        k0 = pl.multiple_of(c * KC, KC)
        dist = row - (k0 + col_in_chunk)
        mult = jnp.zeros((TQ, KC), F32)
        for w, d in DIL_GROUPS:
            member = (dist >= 0) & (dist <= w) & ((dist & (d - 1)) == 0)
            mult = mult + jnp.where(member, 1.0, 0.0)
        on = mult > 0.0
        dmat_ref[:, pl.ds(k0, KC)] = jnp.where(on, dist.astype(F32), jnp.inf)
        amat_ref[:, pl.ds(k0, KC)] = jnp.log2(jnp.where(on, mult, 1.0))
        return 0

    lax.fori_loop(c_lo, c_hi, mask_body, 0)

    slopes = _alibi_slopes(N_HEADS_DIL)
    n_tiles = N_HEADS_DIL * D_HEAD // LANES
    _split_head_pairs(q_ref, scratch[0], n_tiles, D_HEAD ** -0.5)
    chains = [(h // 2, h // 2, slopes[h]) for h in range(N_HEADS_DIL)]
    _flash_chains(chains, k_ref, v_ref, dmat_ref, amat_ref, c_lo, c_hi, scratch)
    _merge_head_pairs(o_ref, scratch[-2], scratch[-1], n_tiles)


def _dilated_attention(hb, batch, seq):
    n = hb.shape[0]
    nq = seq // TQ
    width = N_HEADS_DIL * D_HEAD
    return pl.pallas_call(
        _dilated_kernel,
        grid=(batch, nq),
        in_specs=[
            pl.BlockSpec((TQ, width), lambda b, i: (b * nq + i, 3)),
            pl.BlockSpec((seq, width), lambda b, i: (b, 4)),
            pl.BlockSpec((seq, width), lambda b, i: (b, 5)),
        ],
        out_specs=pl.BlockSpec((TQ, width), lambda b, i: (b * nq + i, 0)),
        out_shape=jax.ShapeDtypeStruct((n, width), BF16),
        scratch_shapes=[pltpu.VMEM((TQ, seq), F32), pltpu.VMEM((TQ, seq), F32)] + _flash_scratch(N_HEADS_DIL, seq),
        compiler_params=_cparams("arbitrary", "arbitrary"),
        name="dilated_attention",
    )(hb, hb, hb)


def _mix_out_kernel(ya_ref, yb_ref, wa_ref, wb_ref, x_ref, g_ref, b_ref, rw_ref, rb_ref,
                    x1_ref, x1t_ref, eidx_ref, gate_ref, rank_ref, cnt_ref, carry_ref):
    step = pl.program_id(0)

    @pl.when(step == 0)
    def _():
        carry_ref[...] = jnp.zeros_like(carry_ref)

    m = (jnp.dot(ya_ref[...], wa_ref[...], preferred_element_type=F32)
         + jnp.dot(yb_ref[...], wb_ref[...], preferred_element_type=F32))
    x1 = _layer_norm(DEEPNORM_ALPHA * x_ref[...] + m, g_ref[...], b_ref[...])
    x1_ref[...] = x1
    _to_token_tiles(x1t_ref, x1)

    t = x1.shape[0]
    logits = jnp.dot(x1.astype(BF16), rw_ref[...], preferred_element_type=F32) + rb_ref[...]
    lane = lax.broadcasted_iota(I32, (t, LANES), 1)
    vals, idxs = [], []
    lg = logits
    for _ in range(TOP_K):
        mx = jnp.max(lg, axis=1, keepdims=True)
        ix = jnp.min(jnp.where(lg == mx, lane, LANES), axis=1, keepdims=True)
        vals.append(mx)
        idxs.append(ix)
        lg = jnp.where(lane == ix, -jnp.inf, lg)
    exps = [jnp.exp(v - vals[0]) for v in vals]
    denom = exps[0]
    for e in exps[1:]:
        denom = denom + e

    onehot = jnp.zeros((t, LANES), F32)
    for ix in idxs:
        onehot = onehot + jnp.where(lane == ix, 1.0, 0.0)
    strict = (lax.broadcasted_iota(I32, (t, t), 1) < lax.broadcasted_iota(I32, (t, t), 0)).astype(BF16)
    before = jnp.dot(strict, onehot.astype(BF16), preferred_element_type=F32) + carry_ref[...]

    eidx = jnp.zeros((t, LANES), I32)
    gate = jnp.zeros((t, LANES), F32)
    rank = jnp.zeros((t, LANES), I32)
    for k in range(TOP_K):
        rk = jnp.sum(jnp.where(lane == idxs[k], before, 0.0), axis=1, keepdims=True)
        eidx = jnp.where(lane == k, idxs[k], eidx)
        gate = jnp.where(lane == k, exps[k] / denom, gate)
        rank = jnp.where(lane == k, rk.astype(I32), rank)
    eidx_ref[...] = eidx
    gate_ref[...] = gate
    rank_ref[...] = rank
    carry_ref[...] = carry_ref[...] + jnp.sum(onehot, axis=0, keepdims=True)
    cnt_ref[...] = carry_ref[...]


def _mix_out(ya, yb, w_out, x2d, ln_g, ln_b, router_w, router_b):
    n, d = x2d.shape
    half = ya.shape[1]
    wa = w_out[:half].astype(BF16)
    wb = w_out[half:].astype(BF16)
    rw = jnp.zeros((d, LANES), BF16).at[:, :N_EXPERTS].set(router_w.astype(BF16))
    rb = jnp.full((1, LANES), -jnp.inf, F32).at[0, :N_EXPERTS].set(router_b)
    t = ROW_TILE
    full = lambda a: pl.BlockSpec(a.shape, lambda i: (0, 0))
    rows = lambda w: pl.BlockSpec((t, w), lambda i: (i, 0))
    g2, b2 = ln_g.reshape(1, d), ln_b.reshape(1, d)
    return pl.pallas_call(
        _mix_out_kernel,
        grid=(n // t,),
        in_specs=[rows(half), rows(half), full(wa), full(wb), rows(d), full(g2), full(b2), full(rw), full(rb)],
        out_specs=[rows(d), pl.BlockSpec((t * ROW_SUBLANES, LANES), lambda i: (i, 0)),
                   rows(LANES), rows(LANES), rows(LANES), pl.BlockSpec((1, LANES), lambda i: (0, 0))],
        out_shape=[jax.ShapeDtypeStruct((n, d), F32), jax.ShapeDtypeStruct((n * ROW_SUBLANES, LANES), F32),
                   jax.ShapeDtypeStruct((n, LANES), I32),
                   jax.ShapeDtypeStruct((n, LANES), F32), jax.ShapeDtypeStruct((n, LANES), I32),
                   jax.ShapeDtypeStruct((1, LANES), F32)],
        scratch_shapes=[pltpu.VMEM((1, LANES), F32)],
        compiler_params=_cparams("arbitrary"),
        name="mix_out_router",
    )(ya, yb, wa, wb, x2d, g2, b2, rw, rb)


ROW_SUBLANES = 8


def _to_token_tiles(o_ref, x):
    for j in range(ROW_SUBLANES):
        o_ref[pl.ds(j, x.shape[0], stride=ROW_SUBLANES), :] = x[:, j * LANES:(j + 1) * LANES]


def _from_token_tiles(buf_ref, n_rows):
    return [buf_ref[pl.ds(j, n_rows, stride=ROW_SUBLANES), :] for j in range(ROW_SUBLANES)]


def _row_copy(src_hbm, dst_ref, src_row, dst_row, sem):
    src = src_hbm.at[pl.ds(pl.multiple_of(src_row * ROW_SUBLANES, ROW_SUBLANES), ROW_SUBLANES)]
    dst = dst_ref.at[pl.ds(pl.multiple_of(dst_row * ROW_SUBLANES, ROW_SUBLANES), ROW_SUBLANES)]
    return pltpu.make_async_copy(src, dst, sem)


def _rows_wait(src_hbm, dst_ref, sem):
    pltpu.make_async_copy(src_hbm.at[pl.ds(0, dst_ref.shape[0])], dst_ref, sem).wait()


def _dispatch_kernel(dest_ref, pad_lo_ref, pad_hi_ref, x_ref, o_hbm, zero_ref, sem, zsem):
    i = pl.program_id(0)
    t = x_ref.shape[0] // ROW_SUBLANES
    zero_rows = zero_ref.shape[0] // ROW_SUBLANES

    def zero_fill(s):
        last = pl.multiple_of((pad_hi_ref[s] - zero_rows) * ROW_SUBLANES, ROW_SUBLANES)
        return pltpu.make_async_copy(zero_ref, o_hbm.at[pl.ds(last, zero_ref.shape[0])], zsem)

    @pl.when(i == 0)
    def _():
        zero_ref[...] = jnp.zeros_like(zero_ref)
        for wait in (False, True):
            def body(s, _, wait=wait):
                @pl.when(pad_lo_ref[s] < pad_hi_ref[s])
                def _():
                    if wait:
                        zero_fill(s).wait()
                    else:
                        zero_fill(s).start()
                return 0
            lax.fori_loop(0, pad_lo_ref.shape[0], body, 0)

    base = i * (t * TOP_K)

    def body(g, _):
        for u in range(ISSUE_UNROLL // TOP_K):
            r = g * (ISSUE_UNROLL // TOP_K) + u
            src = x_ref.at[pl.ds(pl.multiple_of(r * ROW_SUBLANES, ROW_SUBLANES), ROW_SUBLANES)]
            for k in range(TOP_K):
                row = dest_ref[base + r * TOP_K + k]
                dst = o_hbm.at[pl.ds(pl.multiple_of(row * ROW_SUBLANES, ROW_SUBLANES), ROW_SUBLANES)]
                pltpu.make_async_copy(src, dst, sem).start(priority=k % 2)
        return 0

    lax.fori_loop(0, t * TOP_K // ISSUE_UNROLL, body, 0)
    for _ in range(TOP_K):
        pltpu.make_async_copy(x_ref, o_hbm.at[pl.ds(0, x_ref.shape[0])], sem).wait()


def _dispatch(x_tiles, dest_flat, pad_lo, pad_hi, n_rows):
    n = x_tiles.shape[0] // ROW_SUBLANES
    t = min(DISPATCH_T, n)
    grid_spec = pltpu.PrefetchScalarGridSpec(
        num_scalar_prefetch=3,
        grid=(n // t,),
        in_specs=[pl.BlockSpec((t * ROW_SUBLANES, LANES), lambda i, *_: (i, 0))],
        out_specs=pl.BlockSpec(memory_space=pl.ANY),
        scratch_shapes=[pltpu.VMEM((MOE_BM * ROW_SUBLANES, LANES), F32),
                        pltpu.SemaphoreType.DMA(()), pltpu.SemaphoreType.DMA(())],
    )
    return pl.pallas_call(
        _dispatch_kernel,
        grid_spec=grid_spec,
        out_shape=jax.ShapeDtypeStruct((n_rows * ROW_SUBLANES, LANES), F32),
        compiler_params=_cparams("arbitrary"),
        name="moe_dispatch",
    )(dest_flat, pad_lo, pad_hi, x_tiles)


def _expert_kernel(blk_e_ref, n_used_ref, run_start_ref, run_parity_ref, next_e_ref,
                   x_ref, wgu_hbm, bgu_ref, wdn_hbm, bdn_ref,
                   o_ref, wgu_raw, wdn_raw, wgu_bf, wdn_bf, wsem):
    i = pl.program_id(0)
    n_used = n_used_ref[0]
    e = blk_e_ref[i]
    parity = run_parity_ref[i]

    def weight_copies(expert, p):
        return (pltpu.make_async_copy(wgu_hbm.at[expert], wgu_raw.at[p], wsem.at[p, 0]),
                pltpu.make_async_copy(wdn_hbm.at[expert], wdn_raw.at[p], wsem.at[p, 1]))

    @pl.when((i == 0) & (run_start_ref[0] == 1))
    def _():
        for c in weight_copies(e, parity):
            c.start()

    @pl.when(run_start_ref[i] == 1)
    def _():
        for c in weight_copies(e, parity):
            c.wait()
        wgu_bf[...] = wgu_raw[parity].astype(BF16)
        wdn_bf[...] = wdn_raw[parity].astype(BF16)
        e_next = next_e_ref[i]

        @pl.when(e_next >= 0)
        def _():
            for c in weight_copies(e_next, 1 - parity):
                c.start()

    @pl.when(i < n_used)
    def _():
        x = jnp.concatenate(_from_token_tiles(x_ref, MOE_BM), axis=1).astype(BF16)
        h = jnp.dot(x, wgu_bf[...], preferred_element_type=F32) + bgu_ref[...]
        gate = jnp.minimum(h[:, :D_FF], SWIGLU_LIMIT)
        up = jnp.clip(h[:, D_FF:], -SWIGLU_LIMIT, SWIGLU_LIMIT)
        glu = gate * (1.0 / (1.0 + jnp.exp(-SWIGLU_ALPHA * gate)))
        act = ((up + 1.0) * glu).astype(BF16)
        _to_token_tiles(o_ref, jnp.dot(act, wdn_bf[...], preferred_element_type=F32) + bdn_ref[...])

    @pl.when(i >= n_used)
    def _():
        o_ref[...] = jnp.zeros_like(o_ref)


def _expert_ffn(xs, blk_e, n_used, w_gu, b_gu, w_dn, b_dn):
    n_rows = xs.shape[0] // ROW_SUBLANES
    d = w_gu.shape[1]
    n_blocks = n_rows // MOE_BM
    ne = w_gu.shape[0]
    blk = jnp.arange(n_blocks, dtype=I32)
    prev_e = jnp.concatenate([blk_e[:1], blk_e[:-1]])
    run_start = (blk < n_used[0]) & ((blk == 0) | (blk_e != prev_e))
    run_parity = ((jnp.cumsum(run_start.astype(I32)) - 1) % 2).astype(I32)
    first_start_from = lax.cummin(jnp.where(run_start, blk, n_blocks)[::-1])[::-1]
    next_start = jnp.concatenate([first_start_from[1:], jnp.full((1,), n_blocks, I32)])
    next_e = jnp.where(next_start < n_blocks, blk_e[jnp.minimum(next_start, n_blocks - 1)], -1).astype(I32)
    idx = lambda f: (lambda i, *prefetch: f(i, prefetch[0]))
    grid_spec = pltpu.PrefetchScalarGridSpec(
        num_scalar_prefetch=5,
        grid=(n_blocks,),
        in_specs=[
            pl.BlockSpec((MOE_BM * ROW_SUBLANES, LANES),
                         lambda i, be, nu, *_: (jnp.maximum(jnp.minimum(i, nu[0] - 1), 0), 0)),
            pl.BlockSpec(memory_space=pl.ANY),
            pl.BlockSpec((None, 1, 2 * D_FF), idx(lambda i, be: (be[i], 0, 0))),
            pl.BlockSpec(memory_space=pl.ANY),
            pl.BlockSpec((None, 1, d), idx(lambda i, be: (be[i], 0, 0))),
        ],
        out_specs=pl.BlockSpec((MOE_BM * ROW_SUBLANES, LANES), idx(lambda i, be: (i, 0))),
        scratch_shapes=[pltpu.VMEM((2, d, 2 * D_FF), F32), pltpu.VMEM((2, D_FF, d), F32),
                        pltpu.VMEM((d, 2 * D_FF), BF16), pltpu.VMEM((D_FF, d), BF16),
                        pltpu.SemaphoreType.DMA((2, 2))],
    )
    return pl.pallas_call(
        _expert_kernel,
        grid_spec=grid_spec,
        out_shape=jax.ShapeDtypeStruct((n_rows * ROW_SUBLANES, LANES), F32),
        compiler_params=_cparams("arbitrary"),
        name="moe_experts",
    )(blk_e, n_used, run_start.astype(I32), run_parity, next_e,
      xs, w_gu, b_gu.reshape(ne, 1, 2 * D_FF), w_dn, b_dn.reshape(ne, 1, d))


def _combine_kernel(dest_ref, y_hbm, gate_ref, x_ref, g_ref, b_ref, o_ref, buf_ref, sem):
    i = pl.program_id(0)
    t = x_ref.shape[0]
    slot = i % 2

    def gather(step, s):
        base = step * (t * TOP_K)

        def body(g, _):
            for u in range(ISSUE_UNROLL // TOP_K):
                r = g * (ISSUE_UNROLL // TOP_K) + u
                for k in range(TOP_K):
                    row = dest_ref[base + r * TOP_K + k]
                    _row_copy(y_hbm, buf_ref.at[s, k], row, r, sem.at[s]).start(priority=k % 2)
            return 0
        lax.fori_loop(0, t * TOP_K // ISSUE_UNROLL, body, 0)

    @pl.when(i == 0)
    def _():
        gather(0, 0)

    @pl.when(i + 1 < pl.num_programs(0))
    def _():
        gather(i + 1, 1 - slot)

    for k in range(TOP_K):
        _rows_wait(y_hbm, buf_ref.at[slot, k], sem.at[slot])
    gate = gate_ref[...]
    f_tiles = None
    for k in range(TOP_K):
        y_tiles = [gate[:, k:k + 1] * yt for yt in _from_token_tiles(buf_ref.at[slot, k], t)]
        f_tiles = y_tiles if f_tiles is None else [a + b for a, b in zip(f_tiles, y_tiles)]
    f = jnp.concatenate(f_tiles, axis=1)
    o_ref[...] = _layer_norm(DEEPNORM_ALPHA * x_ref[...] + f, g_ref[...], b_ref[...])


def _combine(ys, dest_flat, gates, x2d, ln_g, ln_b):
    n, d = x2d.shape
    t = GATHER_T
    g2, b2 = ln_g.reshape(1, d), ln_b.reshape(1, d)
    full = lambda a: pl.BlockSpec(a.shape, lambda i, dest: (0, 0))
    grid_spec = pltpu.PrefetchScalarGridSpec(
        num_scalar_prefetch=1,
        grid=(n // t,),
        in_specs=[pl.BlockSpec(memory_space=pl.ANY),
                  pl.BlockSpec((t, LANES), lambda i, dest: (i, 0)),
                  pl.BlockSpec((t, d), lambda i, dest: (i, 0)),
                  full(g2), full(b2)],
        out_specs=pl.BlockSpec((t, d), lambda i, dest: (i, 0)),
        scratch_shapes=[pltpu.VMEM((2, TOP_K, t * ROW_SUBLANES, LANES), F32), pltpu.SemaphoreType.DMA((2,))],
    )
    return pl.pallas_call(
        _combine_kernel,
        grid_spec=grid_spec,
        out_shape=jax.ShapeDtypeStruct((n, d), F32),
        compiler_params=_cparams("arbitrary"),
        name="moe_combine",
    )(dest_flat, ys, gates, x2d, g2, b2)


def _moe_block(x1, x1_tiles, eidx, gates, rank, counts, w_gu, b_gu, w_dn, b_dn, ln_g, ln_b):
    n = x1.shape[0]
    e_sel = eidx[:, :TOP_K]
    cnt = counts[0, :N_EXPERTS].astype(I32)
    padded = ((cnt + MOE_BM - 1) // MOE_BM) * MOE_BM
    pad_end = jnp.cumsum(padded)
    pad_start = pad_end - padded
    expert_ids = jnp.arange(N_EXPERTS, dtype=I32)
    dest = jnp.sum(jnp.where(e_sel[..., None] == expert_ids, pad_start, 0), axis=-1) + rank[:, :TOP_K]
    n_rows = ((n * TOP_K + N_EXPERTS * (MOE_BM - 1) + MOE_BM - 1) // MOE_BM) * MOE_BM
    n_blocks = n_rows // MOE_BM
    dest_flat = dest.reshape(-1)
    tail_hi = pad_end[-1] + MOE_BM * (1 + jnp.arange(n_blocks - (n * TOP_K) // MOE_BM, dtype=I32))
    tail_ok = tail_hi <= n_rows
    pad_lo = jnp.concatenate([pad_start + cnt, jnp.where(tail_ok, tail_hi - MOE_BM, MOE_BM)]).astype(I32)
    pad_hi = jnp.concatenate([pad_end, jnp.where(tail_ok, tail_hi, MOE_BM)]).astype(I32)
    blk_start = jnp.arange(n_blocks, dtype=I32) * MOE_BM
    blk_e = jnp.minimum(jnp.sum((pad_end[None, :] <= blk_start[:, None]).astype(I32), axis=1), N_EXPERTS - 1)
    n_used = (pad_end[-1:] // MOE_BM).astype(I32)
    xs = _dispatch(x1_tiles, dest_flat, pad_lo, pad_hi, n_rows)
    ys = _expert_ffn(xs, blk_e, n_used, w_gu, b_gu, w_dn, b_dn)
    return _combine(ys, dest_flat, gates, x1, ln_g, ln_b)


def _pad_cols(w, width):
    return jnp.pad(w, ((0, 0), (0, width - w.shape[1])))


def kernel(x, w_in_0, conv_w_0, w_out_0, ln_mix_g_0, ln_mix_b_0, router_w_0, router_b_0, w_gu_0, b_gu_0, w_dn_0, b_dn_0, ln_ffn_g_0, ln_ffn_b_0, w_in_1, lam_q1_1, lam_k1_1, lam_q2_1, lam_k2_1, subln_g_1, w_out_1, ln_mix_g_1, ln_mix_b_1, router_w_1, router_b_1, w_gu_1, b_gu_1, w_dn_1, b_dn_1, ln_ffn_g_1, ln_ffn_b_1):
    batch, seq, d = x.shape
    x0 = x.reshape(batch * seq, d)

    n_f32 = 3 * CONV_W
    n_attn = 3 * N_HEADS_SPARSE * D_HEAD + N_IDX_HEADS * D_IDX + D_IDX
    w_a = _pad_cols(jnp.concatenate([w_in_0[:, :n_f32], w_in_0[:, n_f32 + n_attn:]], axis=1),
                    n_f32 + LANES)
    w_b = _pad_cols(w_in_0[:, n_f32:n_f32 + n_attn], 3 * N_HEADS_SPARSE * D_HEAD + N_IDX_HEADS * D_IDX + LANES)
    ha, hb = _project(x0, [w_a.astype(BF16), w_b.astype(BF16)], [F32, BF16])
    ya = _short_conv(ha, conv_w_0, batch, seq)
    yb = _dsa_attention(ha, hb, batch, seq)
    x1, x1t, eidx, gates, rank, counts = _mix_out(ya, yb, w_out_0, x0, ln_mix_g_0, ln_mix_b_0, router_w_0, router_b_0)
    x2 = _moe_block(x1, x1t, eidx, gates, rank, counts, w_gu_0, b_gu_0, w_dn_0, b_dn_0, ln_ffn_g_0, ln_ffn_b_0)

    (hc,) = _project(x2, [w_in_1.astype(BF16)], [BF16])
    lam_vecs = jnp.stack([lam_q1_1, lam_k1_1, lam_q2_1, lam_k2_1]).astype(F32)
    yc = _diff_attention(hc, lam_vecs, subln_g_1.reshape(1, -1).astype(F32), batch, seq, 1)
    yd = _dilated_attention(hc, batch, seq)
    x3, x3t, eidx, gates, rank, counts = _mix_out(yc, yd, w_out_1, x2, ln_mix_g_1, ln_mix_b_1, router_w_1, router_b_1)
    x4 = _moe_block(x3, x3t, eidx, gates, rank, counts, w_gu_1, b_gu_1, w_dn_1, b_dn_1, ln_ffn_g_1, ln_ffn_b_1)
    return x4.reshape(batch, seq, d)
```
